```python
import jax
import jax.numpy as jnp
from jax import lax
import numpy as np

D_MODEL = 2048
BATCH = 2
SEQ = 4096
DEPTH = 1

RET_HEADS = 4
RET_DV = D_MODEL // RET_HEADS
RET_DK = RET_DV // 2
RET_CHUNK = 256
RET_THETA = 10000.0
ATT_HEAD_DIM = 64
ATT_HEADS = D_MODEL // ATT_HEAD_DIM
ATT_KV_HEADS = ATT_HEADS // 8
WINDOW = 128
ATT_BLOCK = WINDOW
ROPE_THETA = 500000.0
ROPE_DIM = ATT_HEAD_DIM // 4
D_FF = ((-(-8 * D_MODEL // 3) + 255) // 256) * 256
DEEPNORM_ALPHA = (2.0 * DEPTH) ** 0.25
DEEPNORM_BETA = (8.0 * DEPTH) ** -0.25
LN_EPS = 1e-5
GN_EPS = 1e-5

kernel_name = 'hybrid_retention_swa_sink_deepnorm'


def _in_proj_sizes():
    return (RET_HEADS * RET_DK, RET_HEADS * RET_DK, RET_HEADS * RET_DV, RET_HEADS * RET_DV,
            ATT_HEADS * ATT_HEAD_DIM, ATT_KV_HEADS * ATT_HEAD_DIM, ATT_KV_HEADS * ATT_HEAD_DIM,
            D_MODEL, D_MODEL)


def _in_proj_scales():
    b = DEEPNORM_BETA
    return (1.0, 1.0, b, 1.0, 1.0, 1.0, b, 1.0, 1.0)


def layer_norm(x, g, b):
    xf = x.astype(jnp.float32)
    mu = jnp.mean(xf, axis=-1, keepdims=True)
    var = jnp.mean(jnp.square(xf - mu), axis=-1, keepdims=True)
    return ((xf - mu) * lax.rsqrt(var + LN_EPS)).astype(x.dtype) * g + b


def rotary_tables(positions, inv_freq, dtype):
    ang = positions.astype(jnp.float32)[..., None] * inv_freq
    return (jnp.cos(ang)[:, :, None, :].astype(dtype),
            jnp.sin(ang)[:, :, None, :].astype(dtype))


def apply_rotary(x, cos, sin):
    x1, x2 = jnp.split(x, 2, axis=-1)
    return jnp.concatenate([x1 * cos - x2 * sin, x2 * cos + x1 * sin], axis=-1)


def retention_branch(q, k, v, g, gn_g, gn_b, positions):
    bsz, seq, nh, dk = q.shape
    dv = v.shape[-1]
    inv_freq = 1.0 / (RET_THETA ** jnp.linspace(0.0, 1.0, dk // 2, dtype=jnp.float32))
    cos, sin = rotary_tables(positions, inv_freq, q.dtype)
    q = apply_rotary(q, cos, sin) * (dk ** -0.5)
    k = apply_rotary(k, cos, sin)
    log_gamma = jnp.log(1.0 - jnp.exp2(-5.0 - jnp.arange(nh, dtype=jnp.float32)))
    c = RET_CHUNK
    n = seq // c
    qc = q.reshape(bsz, n, c, nh, dk)
    kc = k.reshape(bsz, n, c, nh, dk)
    vc = v.reshape(bsz, n, c, nh, dv)
    idx = jnp.arange(c, dtype=jnp.float32)
    diff = idx[:, None] - idx[None, :]
    intra_decay = jnp.where(diff >= 0.0,
                            jnp.exp(log_gamma[:, None, None] * jnp.maximum(diff, 0.0)), 0.0)
    s = jnp.einsum('bnchd,bnmhd->bnhcm', qc, kc) * intra_decay.astype(q.dtype)
    inner = jnp.einsum('bnhcm,bnmhe->bnche', s, vc)
    k_decay = jnp.exp(log_gamma[:, None] * (c - 1.0 - idx)[None, :])
    q_decay = jnp.exp(log_gamma[:, None] * (idx + 1.0)[None, :])
    chunk_decay = jnp.exp(log_gamma * c)[None, :, None, None]
    kv = jnp.einsum('bnmhd,hm,bnmhe->nbhde', kc.astype(jnp.float32), k_decay,
                    vc.astype(jnp.float32))

    def step(state, kv_chunk):
        return state * chunk_decay + kv_chunk, state

    _, prev = lax.scan(step, jnp.zeros(kv.shape[1:], jnp.float32), kv)
    cross = jnp.einsum('bnchd,hc,nbhde->bnche', qc.astype(jnp.float32), q_decay, prev)
    y = (inner.astype(jnp.float32) + cross).reshape(bsz, seq, nh, dv)
    mu = jnp.mean(y, axis=-1, keepdims=True)
    var = jnp.mean(jnp.square(y - mu), axis=-1, keepdims=True)
    y = ((y - mu) * lax.rsqrt(var + GN_EPS)).reshape(bsz, seq, nh * dv).astype(v.dtype)
    y = y * gn_g + gn_b
    return jax.nn.silu(g) * y


def sliding_window_sink_attention(q, k, v, sinks, positions):
    bsz, seq, hq, d = q.shape
    hkv = k.shape[2]
    grp = hq // hkv
    inv_freq = 1.0 / (ROPE_THETA ** (jnp.arange(0, ROPE_DIM, 2, dtype=jnp.float32) / ROPE_DIM))
    cos, sin = rotary_tables(positions, inv_freq, q.dtype)
    q = jnp.concatenate([apply_rotary(q[..., :ROPE_DIM], cos, sin), q[..., ROPE_DIM:]], axis=-1)
    k = jnp.concatenate([apply_rotary(k[..., :ROPE_DIM], cos, sin), k[..., ROPE_DIM:]], axis=-1)
    bq = ATT_BLOCK
    n = seq // bq
    qb = q.reshape(bsz, n, bq, hkv, grp, d)
    kb = k.reshape(bsz, n, bq, hkv, d)
    vb = v.reshape(bsz, n, bq, hkv, d)
    kpad = jnp.zeros_like(kb[:, :1])
    vpad = jnp.zeros_like(vb[:, :1])
    kw = jnp.concatenate([jnp.concatenate([kpad, kb[:, :-1]], axis=1), kb], axis=2)
    vw = jnp.concatenate([jnp.concatenate([vpad, vb[:, :-1]], axis=1), vb], axis=2)
    s = jnp.einsum('bnqhgd,bnkhd->bnhgqk', qb, kw).astype(jnp.float32) * (d ** -0.5)
    qi = jnp.arange(bq)[:, None]
    kj = jnp.arange(2 * bq)[None, :]
    dist = qi + bq - kj
    band = (dist >= 0) & (dist < WINDOW)
    valid = (jnp.arange(n)[:, None, None] > 0) | (kj[None] >= bq)
    mask = band[None] & valid
    s = jnp.where(mask[None, :, None, None], s, -jnp.inf)
    sink = sinks.astype(jnp.float32).reshape(hkv, grp)[:, :, None, None]
    m = jnp.maximum(jnp.max(s, axis=-1, keepdims=True), sink)
    p = jnp.exp(s - m)
    p = p / (jnp.sum(p, axis=-1, keepdims=True) + jnp.exp(sink - m))
    o = jnp.einsum('bnhgqk,bnkhd->bnqhgd', p.astype(v.dtype), vw)
    return o.reshape(bsz, seq, hq * d)


def setup_inputs(seed: int = 0) -> dict:
    key = jax.random.key(seed)
    ks = jax.random.split(key, 16)
    f32 = jnp.float32
    x = jax.random.normal(ks[0], (BATCH, SEQ, D_MODEL), f32)
    start = jax.random.randint(ks[1], (BATCH, 1), 0, 1024, dtype=jnp.int32)
    positions = start + jnp.arange(SEQ, dtype=jnp.int32)[None, :]
    sizes = _in_proj_sizes()
    scales = _in_proj_scales()
    seg_keys = jax.random.split(ks[2], len(sizes))
    w_in = jnp.concatenate(
        [jax.random.normal(sk, (DEPTH, D_MODEL, sz), f32) * (D_MODEL ** -0.5 * sc)
         for sk, sz, sc in zip(seg_keys, sizes, scales)], axis=-1)
    ret_gn_g = 1.0 + 0.02 * jax.random.normal(ks[3], (DEPTH, RET_HEADS * RET_DV), f32)
    ret_gn_b = 0.02 * jax.random.normal(ks[4], (DEPTH, RET_HEADS * RET_DV), f32)
    att_sinks = 0.5 * jax.random.normal(ks[5], (DEPTH, ATT_HEADS), f32)
    w_out = jax.random.normal(ks[6], (DEPTH, D_MODEL, D_MODEL), f32) * (D_MODEL ** -0.5 * DEEPNORM_BETA)
    ln1_g = 1.0 + 0.02 * jax.random.normal(ks[7], (DEPTH, D_MODEL), f32)
    ln1_b = 0.02 * jax.random.normal(ks[8], (DEPTH, D_MODEL), f32)
    w_gate = jax.random.normal(ks[9], (DEPTH, D_MODEL, D_FF), f32) * (D_MODEL ** -0.5 * DEEPNORM_BETA)
    w_up = jax.random.normal(ks[10], (DEPTH, D_MODEL, D_FF), f32) * (D_MODEL ** -0.5 * DEEPNORM_BETA)
    w_down = jax.random.normal(ks[11], (DEPTH, D_FF, D_MODEL), f32) * (D_FF ** -0.5 * DEEPNORM_BETA)
    ln2_g = 1.0 + 0.02 * jax.random.normal(ks[12], (DEPTH, D_MODEL), f32)
    ln2_b = 0.02 * jax.random.normal(ks[13], (DEPTH, D_MODEL), f32)
    return {'x': x, 'positions': positions, 'w_in': w_in, 'ret_gn_g': ret_gn_g,
            'ret_gn_b': ret_gn_b, 'att_sinks': att_sinks, 'w_out': w_out,
            'ln1_g': ln1_g, 'ln1_b': ln1_b, 'w_gate': w_gate, 'w_up': w_up,
            'w_down': w_down, 'ln2_g': ln2_g, 'ln2_b': ln2_b}


def reference(x, positions, w_in, ret_gn_g, ret_gn_b, att_sinks, w_out, ln1_g, ln1_b,
              w_gate, w_up, w_down, ln2_g, ln2_b):
    bsz, seq, _ = x.shape
    splits = np.cumsum(_in_proj_sizes())[:-1].tolist()
    h = x
    for l in range(DEPTH):
        proj = h @ w_in[l]
        rq, rk, rv, rg, aq, ak, av, gate_a, gate_b = jnp.split(proj, splits, axis=-1)
        ret = retention_branch(rq.reshape(bsz, seq, RET_HEADS, RET_DK),
                               rk.reshape(bsz, seq, RET_HEADS, RET_DK),
                               rv.reshape(bsz, seq, RET_HEADS, RET_DV),
                               rg, ret_gn_g[l], ret_gn_b[l], positions)
        att = sliding_window_sink_attention(aq.reshape(bsz, seq, ATT_HEADS, ATT_HEAD_DIM),
                                            ak.reshape(bsz, seq, ATT_KV_HEADS, ATT_HEAD_DIM),
                                            av.reshape(bsz, seq, ATT_KV_HEADS, ATT_HEAD_DIM),
                                            att_sinks[l], positions)
        merged = jax.nn.sigmoid(gate_a) * ret + jax.nn.sigmoid(gate_b) * att
        h = layer_norm(DEEPNORM_ALPHA * h + merged @ w_out[l], ln1_g[l], ln1_b[l])
        ffn = (jax.nn.silu(h @ w_gate[l]) * (h @ w_up[l])) @ w_down[l]
        h = layer_norm(DEEPNORM_ALPHA * h + ffn, ln2_g[l], ln2_b[l])
    return h
```

```python
import functools

import jax
import jax.numpy as jnp
from jax import lax
from jax.experimental import pallas as pl
from jax.experimental.pallas import tpu as pltpu

F32 = jnp.float32
BF16 = jnp.bfloat16

D_MODEL = 2048
RET_HEADS = 4
RET_DV = D_MODEL // RET_HEADS
RET_DK = RET_DV // 2
RET_CHUNK = 256
RET_THETA = 10000.0
ATT_HEAD_DIM = 64
ATT_HEADS = D_MODEL // ATT_HEAD_DIM
ATT_KV_HEADS = ATT_HEADS // 8
ATT_GROUP = ATT_HEADS // ATT_KV_HEADS
WINDOW = 128
ATT_BLOCK = WINDOW
ROPE_THETA = 500000.0
ROPE_DIM = ATT_HEAD_DIM // 4
D_FF = 5632
LN_EPS = 1e-5
GN_EPS = 1e-5

LANES = 128

OFF_RQ = 0
OFF_RK = OFF_RQ + RET_HEADS * RET_DK
OFF_RV = OFF_RK + RET_HEADS * RET_DK
OFF_RG = OFF_RV + RET_HEADS * RET_DV
OFF_AQ = OFF_RG + RET_HEADS * RET_DV
OFF_AK = OFF_AQ + ATT_HEADS * ATT_HEAD_DIM
OFF_AV = OFF_AK + ATT_KV_HEADS * ATT_HEAD_DIM
OFF_GA = OFF_AV + ATT_KV_HEADS * ATT_HEAD_DIM
OFF_GB = OFF_GA + D_MODEL
N_PROJ = OFF_GB + D_MODEL

MIB = 1024 * 1024


def _params(sem, vmem_mib):
    return pltpu.CompilerParams(dimension_semantics=sem, vmem_limit_bytes=vmem_mib * MIB)


def _sigmoid(v):
    return 1.0 / (1.0 + jnp.exp(-v))


def _layer_norm(z, g, b):
    mu = jnp.mean(z, axis=-1, keepdims=True)
    d = z - mu
    var = jnp.mean(d * d, axis=-1, keepdims=True)
    return d * lax.rsqrt(var + LN_EPS) * g + b


def _tables_kernel(pos_ref, fr_ref, fa_ref, cr_ref, sr_ref, ca_ref, s1_ref, s2_ref):
    pos = pos_ref[...].astype(F32)
    ang_r = pos * fr_ref[...]
    cr_ref[...] = jnp.cos(ang_r)
    sr_ref[...] = jnp.sin(ang_r)
    ang_a = pos * fa_ref[...]
    sa = jnp.sin(ang_a)
    d = lax.broadcasted_iota(jnp.int32, ang_a.shape, 1) & (ATT_HEAD_DIM - 1)
    half = ROPE_DIM // 2
    ca_ref[...] = jnp.cos(ang_a)
    s1_ref[...] = jnp.where(d < half, -sa, 0.0)
    s2_ref[...] = jnp.where((d >= half) & (d < ROPE_DIM), sa, 0.0)


def _rotary_tables(positions):
    t = positions.size
    tm = 1024
    pos_b = jnp.broadcast_to(positions.reshape(t, 1), (t, LANES))
    inv_r = 1.0 / (RET_THETA ** jnp.linspace(0.0, 1.0, RET_DK // 2, dtype=F32))
    inv_a = 1.0 / (ROPE_THETA ** (jnp.arange(0, ROPE_DIM, 2, dtype=F32) / ROPE_DIM))
    d = jnp.arange(LANES) % ATT_HEAD_DIM
    fa = jnp.where(d < ROPE_DIM, inv_a[d % (ROPE_DIM // 2)], 0.0).astype(F32)
    row = pl.BlockSpec((tm, LANES), lambda i: (i, 0))
    vec = pl.BlockSpec((1, LANES), lambda i: (0, 0))
    out = jax.ShapeDtypeStruct((t, LANES), F32)
    return pl.pallas_call(
        _tables_kernel,
        grid=(t // tm,),
        in_specs=[row, vec, vec],
        out_specs=[row] * 5,
        out_shape=[out] * 5,
        compiler_params=_params(("parallel",), 32),
        name="rotary_tables",
    )(pos_b, inv_r.reshape(1, LANES), fa.reshape(1, LANES))


def _matmul_kernel(x_ref, w_ref, o_ref):
    o_ref[...] = jnp.dot(x_ref[...], w_ref[...], preferred_element_type=F32).astype(o_ref.dtype)


def _in_proj(xb, wb):
    t, k = xb.shape
    n = wb.shape[1]
    tm, tn = 1024, 1280
    return pl.pallas_call(
        _matmul_kernel,
        grid=(t // tm, n // tn),
        in_specs=[pl.BlockSpec((tm, k), lambda i, j: (i, 0)),
                  pl.BlockSpec((k, tn), lambda i, j: (0, j))],
        out_specs=pl.BlockSpec((tm, tn), lambda i, j: (i, j)),
        out_shape=jax.ShapeDtypeStruct((t, n), BF16),
        compiler_params=_params(("parallel", "parallel"), 48),
        name="in_proj",
    )(xb, wb)


def _retention_kernel(q_ref, k_ref, v_ref, g_ref, ga_ref, cos_ref, sin_ref, gng_ref, gnb_ref,
                      o_ref, state_ref, dmask_ref, qdec_ref, kdec_ref, cdec_ref):
    c = RET_CHUNK
    h = pl.program_id(1)
    n = pl.program_id(2)

    @pl.when(n == 0)
    def _init():
        state_ref[...] = jnp.zeros_like(state_ref)
        def log_gamma(shape):
            hv = jnp.full(shape, h, jnp.int32).astype(F32)
            return jnp.log(1.0 - jnp.exp2(-5.0 - hv))

        ii = lax.broadcasted_iota(jnp.int32, (c, c), 0)
        jj = lax.broadcasted_iota(jnp.int32, (c, c), 1)
        diff = (ii - jj).astype(F32)
        dmask_ref[...] = jnp.where(
            diff >= 0.0, jnp.exp(log_gamma((c, c)) * jnp.maximum(diff, 0.0)), 0.0)
        idx = lax.broadcasted_iota(jnp.int32, (c, LANES), 0).astype(F32)
        lgl = log_gamma((c, LANES))
        qdec_ref[...] = jnp.exp(lgl * (idx + 1.0))
        kdec_ref[...] = jnp.exp(lgl * (c - 1.0 - idx))
        cdec_ref[...] = jnp.exp(log_gamma((8, LANES)) * float(c))

    cos = cos_ref[...]
    sin = sin_ref[...]
    half = RET_DK // 2

    def rot(ref):
        a = ref[...].astype(F32)
        a1, a2 = a[:, :half], a[:, half:]
        return a1 * cos - a2 * sin, a2 * cos + a1 * sin

    q1, q2 = rot(q_ref)
    k1, k2 = rot(k_ref)
    scale = RET_DK ** -0.5
    qd = qdec_ref[...]
    kd = kdec_ref[...]
    qr = jnp.concatenate([q1, q2], axis=-1) * scale
    kr = jnp.concatenate([k1, k2], axis=-1)
    qrd = (qr * jnp.concatenate([qd, qd], axis=-1)).astype(BF16)
    krd = (kr * jnp.concatenate([kd, kd], axis=-1)).astype(BF16)
    qrb = qr.astype(BF16)
    krb = kr.astype(BF16)
    v = v_ref[...]

    s = lax.dot_general(qrb, krb, (((1,), (1,)), ((), ())), preferred_element_type=F32)
    s = s * dmask_ref[...]
    inner = jnp.dot(s.astype(BF16), v, preferred_element_type=F32)
    state = state_ref[...]
    cross = jnp.dot(qrd, state.astype(BF16), preferred_element_type=F32)
    kv = lax.dot_general(krd, v, (((0,), (0,)), ((), ())), preferred_element_type=F32)
    state_ref[...] = state * cdec_ref[0:1, 0:1] + kv

    y = inner + cross
    mu = jnp.mean(y, axis=-1, keepdims=True)
    d = y - mu
    var = jnp.mean(d * d, axis=-1, keepdims=True)
    yn = d * lax.rsqrt(var + GN_EPS) * gng_ref[...] + gnb_ref[...]
    g = g_ref[...].astype(F32)
    ga = ga_ref[...].astype(F32)
    o_ref[...] = (_sigmoid(ga) * (g * _sigmoid(g) * yn)).astype(o_ref.dtype)


def _retention(proj, cos_r, sin_r, gn_g, gn_b, bsz, seq):
    t = bsz * seq
    c = RET_CHUNK
    nc = seq // c

    def rows(width, col0):
        return pl.BlockSpec((c, width), lambda b, h, n: (b * nc + n, col0 + h))

    tab = pl.BlockSpec((c, LANES), lambda b, h, n: (b * nc + n, 0))
    gn = pl.BlockSpec((1, RET_DV), lambda b, h, n: (0, h))
    return pl.pallas_call(
        _retention_kernel,
        grid=(bsz, RET_HEADS, nc),
        in_specs=[rows(RET_DK, OFF_RQ // RET_DK), rows(RET_DK, OFF_RK // RET_DK),
                  rows(RET_DV, OFF_RV // RET_DV), rows(RET_DV, OFF_RG // RET_DV),
                  rows(RET_DV, OFF_GA // RET_DV), tab, tab, gn, gn],
        out_specs=pl.BlockSpec((c, RET_DV), lambda b, h, n: (b * nc + n, h)),
        out_shape=jax.ShapeDtypeStruct((t, D_MODEL), BF16),
        scratch_shapes=[pltpu.VMEM((RET_DK, RET_DV), F32), pltpu.VMEM((c, c), F32),
                        pltpu.VMEM((c, LANES), F32), pltpu.VMEM((c, LANES), F32),
                        pltpu.VMEM((8, LANES), F32)],
        compiler_params=_params(("parallel", "parallel", "arbitrary"), 32),
        name="retention",
    )(proj, proj, proj, proj, proj, cos_r, sin_r, gn_g.reshape(1, -1), gn_b.reshape(1, -1))


def _rope_lanes(a, ca, s1, s2):
    outs = []
    for c0 in range(0, a.shape[1], LANES):
        ac = a[:, c0:c0 + LANES]
        up = pltpu.roll(ac, LANES - ROPE_DIM // 2, 1)
        dn = pltpu.roll(ac, ROPE_DIM // 2, 1)
        outs.append(ac * ca + up * s1 + dn * s2)
    return jnp.concatenate(outs, axis=-1)


def _attention_kernel(sink_ref, q_ref, kp_ref, kc_ref, vp_ref, vc_ref,
                      gb0_ref, gb1_ref, gb2_ref, gb3_ref,
                      ca_ref, s1_ref, s2_ref, cap_ref, s1p_ref, s2p_ref, o_ref):
    i = pl.program_id(1)
    bq = ATT_BLOCK
    hd = ATT_HEAD_DIM
    ca, s1, s2 = ca_ref[...], s1_ref[...], s2_ref[...]
    q = (_rope_lanes(q_ref[...].astype(F32), ca, s1, s2) * (hd ** -0.5)).astype(BF16)
    kc = _rope_lanes(kc_ref[...].astype(F32), ca, s1, s2).astype(BF16)
    kp = _rope_lanes(kp_ref[...].astype(F32), cap_ref[...], s1p_ref[...], s2p_ref[...]).astype(BF16)
    vp, vc = vp_ref[...], vc_ref[...]

    qi = lax.broadcasted_iota(jnp.int32, (bq, 2 * bq), 0)
    kj = lax.broadcasted_iota(jnp.int32, (bq, 2 * bq), 1)
    dist = qi + bq - kj
    kmin = jnp.where(i > 0, 0, bq)
    mask = (dist >= 0) & (dist < WINDOW) & (kj >= kmin)

    gb_refs = (gb0_ref, gb1_ref, gb2_ref, gb3_ref)
    for g in range(ATT_KV_HEADS):
        kcat = jnp.concatenate([kp[:, g * hd:(g + 1) * hd], kc[:, g * hd:(g + 1) * hd]], axis=0)
        vcat = jnp.concatenate([vp[:, g * hd:(g + 1) * hd], vc[:, g * hd:(g + 1) * hd]], axis=0)
        qg = jnp.concatenate(
            [q[:, (g * ATT_GROUP + j) * hd:(g * ATT_GROUP + j + 1) * hd] for j in range(ATT_GROUP)],
            axis=0)
        s = lax.dot_general(qg, kcat, (((1,), (1,)), ((), ())), preferred_element_type=F32)
        ps, ls = [], []
        for j in range(ATT_GROUP):
            sj = jnp.where(mask, s[j * bq:(j + 1) * bq], -jnp.inf)
            sink = sink_ref[g * ATT_GROUP + j]
            m = jnp.maximum(jnp.max(sj, axis=-1, keepdims=True), sink)
            p = jnp.exp(sj - m)
            ls.append(jnp.sum(p, axis=-1, keepdims=True) + jnp.exp(sink - m))
            ps.append(p.astype(BF16))
        o = jnp.dot(jnp.concatenate(ps, axis=0), vcat, preferred_element_type=F32)
        og = jnp.concatenate([o[j * bq:(j + 1) * bq] / ls[j] for j in range(ATT_GROUP)], axis=-1)
        gate = _sigmoid(gb_refs[g][...].astype(F32))
        o_ref[:, g * ATT_GROUP * hd:(g + 1) * ATT_GROUP * hd] = (gate * og).astype(o_ref.dtype)


def _attention(proj, sinks, ca, s1, s2, bsz, seq):
    t = bsz * seq
    bq = ATT_BLOCK
    nb = seq // bq
    kvw = ATT_KV_HEADS * ATT_HEAD_DIM
    gw = ATT_GROUP * ATT_HEAD_DIM

    def cur(width, col):
        return pl.BlockSpec((bq, width), lambda b, i: (b * nb + i, col))

    def prev(width, col):
        return pl.BlockSpec((bq, width), lambda b, i: (b * nb + jnp.maximum(i - 1, 0), col))

    in_specs = [pl.BlockSpec(memory_space=pltpu.SMEM),
                cur(D_MODEL, OFF_AQ // D_MODEL),
                prev(kvw, OFF_AK // kvw), cur(kvw, OFF_AK // kvw),
                prev(kvw, OFF_AV // kvw), cur(kvw, OFF_AV // kvw)]
    in_specs += [cur(gw, OFF_GB // gw + g) for g in range(ATT_KV_HEADS)]
    in_specs += [cur(LANES, 0)] * 3 + [prev(LANES, 0)] * 3
    return pl.pallas_call(
        _attention_kernel,
        grid=(bsz, nb),
        in_specs=in_specs,
        out_specs=pl.BlockSpec((bq, D_MODEL), lambda b, i: (b * nb + i, 0)),
        out_shape=jax.ShapeDtypeStruct((t, D_MODEL), BF16),
        compiler_params=_params(("parallel", "arbitrary"), 32),
        name="attention",
    )(sinks, proj, proj, proj, proj, proj, proj, proj, proj, proj, ca, s1, s2, ca, s1, s2)


def _out_proj_kernel(alpha, ret_ref, att_ref, x_ref, w_ref, g_ref, b_ref, h_ref):
    merged = (ret_ref[...].astype(F32) + att_ref[...].astype(F32)).astype(BF16)
    y = jnp.dot(merged, w_ref[...], preferred_element_type=F32)
    h_ref[...] = _layer_norm(alpha * x_ref[...] + y, g_ref[...], b_ref[...])


def _out_proj(ret, att, x2, wb, ln_g, ln_b, alpha):
    t, d = x2.shape
    tm = 512
    row = pl.BlockSpec((tm, d), lambda i: (i, 0))
    vec = pl.BlockSpec((1, d), lambda i: (0, 0))
    return pl.pallas_call(
        functools.partial(_out_proj_kernel, alpha),
        grid=(t // tm,),
        in_specs=[row, row, row, pl.BlockSpec((d, d), lambda i: (0, 0)), vec, vec],
        out_specs=row,
        out_shape=jax.ShapeDtypeStruct((t, d), F32),
        compiler_params=_params(("parallel",), 48),
        name="out_proj_ln",
    )(ret, att, x2, wb, ln_g.reshape(1, d), ln_b.reshape(1, d))


def _ffn_kernel(alpha, h_ref, wg_ref, wu_ref, wd_ref, g_ref, b_ref, o_ref, hb_ref):
    f = pl.program_id(1)

    @pl.when(f == 0)
    def _init():
        h = h_ref[...]
        hb_ref[...] = h.astype(BF16)
        o_ref[...] = alpha * h

    hb = hb_ref[...]
    gt = jnp.dot(hb, wg_ref[...], preferred_element_type=F32)
    ut = jnp.dot(hb, wu_ref[...], preferred_element_type=F32)
    a = (gt * _sigmoid(gt) * ut).astype(BF16)
    o_ref[...] += jnp.dot(a, wd_ref[...], preferred_element_type=F32)

    @pl.when(f == pl.num_programs(1) - 1)
    def _finish():
        o_ref[...] = _layer_norm(o_ref[...], g_ref[...], b_ref[...])


def _ffn(h1, wgb, wub, wdb, ln_g, ln_b, alpha):
    t, d = h1.shape
    dff = wgb.shape[1]
    tm, tf = 512, 512
    row = pl.BlockSpec((tm, d), lambda i, f: (i, 0))
    vec = pl.BlockSpec((1, d), lambda i, f: (0, 0))
    return pl.pallas_call(
        functools.partial(_ffn_kernel, alpha),
        grid=(t // tm, dff // tf),
        in_specs=[row,
                  pl.BlockSpec((d, tf), lambda i, f: (0, f)),
                  pl.BlockSpec((d, tf), lambda i, f: (0, f)),
                  pl.BlockSpec((tf, d), lambda i, f: (f, 0)),
                  vec, vec],
        out_specs=row,
        out_shape=jax.ShapeDtypeStruct((t, d), F32),
        scratch_shapes=[pltpu.VMEM((tm, d), BF16)],
        compiler_params=_params(("parallel", "arbitrary"), 48),
        name="ffn_ln",
    )(h1, wgb, wub, wdb, ln_g.reshape(1, d), ln_b.reshape(1, d))


def kernel(x, positions, w_in, ret_gn_g, ret_gn_b, att_sinks, w_out, ln1_g, ln1_b,
           w_gate, w_up, w_down, ln2_g, ln2_b):
    bsz, seq, d = x.shape
    depth = w_in.shape[0]
    alpha = (2.0 * depth) ** 0.25
    cos_r, sin_r, ca, s1, s2 = _rotary_tables(positions)
    h = x.reshape(bsz * seq, d)
    for l in range(depth):
        proj = _in_proj(h.astype(BF16), w_in[l].astype(BF16))
        ret = _retention(proj, cos_r, sin_r, ret_gn_g[l], ret_gn_b[l], bsz, seq)
        att = _attention(proj, att_sinks[l], ca, s1, s2, bsz, seq)
        h = _out_proj(ret, att, h, w_out[l].astype(BF16), ln1_g[l], ln1_b[l], alpha)
        h = _ffn(h, w_gate[l].astype(BF16), w_up[l].astype(BF16), w_down[l].astype(BF16),
                 ln2_g[l], ln2_b[l], alpha)
    return h.reshape(bsz, seq, d)
```

```python
import functools
import math

import jax
import jax.numpy as jnp
from jax import lax
from jax.experimental import pallas as pl
from jax.experimental.pallas import tpu as pltpu

F32 = jnp.float32
BF16 = jnp.bfloat16

D_MODEL = 2048
RET_HEADS = 4
RET_DV = D_MODEL // RET_HEADS
RET_DK = RET_DV // 2
RET_CHUNK = 256
RET_THETA = 10000.0
ATT_HEAD_DIM = 64
ATT_HEADS = D_MODEL // ATT_HEAD_DIM
ATT_KV_HEADS = ATT_HEADS // 8
ATT_GROUP = ATT_HEADS // ATT_KV_HEADS
WINDOW = 128
ATT_BLOCK = WINDOW
ROPE_THETA = 500000.0
ROPE_DIM = ATT_HEAD_DIM // 4
LN_EPS = 1e-5
GN_EPS = 1e-5

LANES = 128
LOG2E = math.log2(math.e)

OFF_RQ = 0
OFF_RK = OFF_RQ + RET_HEADS * RET_DK
OFF_RV = OFF_RK + RET_HEADS * RET_DK
OFF_RG = OFF_RV + RET_HEADS * RET_DV
OFF_AQ = OFF_RG + RET_HEADS * RET_DV
OFF_AK = OFF_AQ + ATT_HEADS * ATT_HEAD_DIM
OFF_AV = OFF_AK + ATT_KV_HEADS * ATT_HEAD_DIM
OFF_GA = OFF_AV + ATT_KV_HEADS * ATT_HEAD_DIM
OFF_GB = OFF_GA + D_MODEL
KV_W = ATT_KV_HEADS * ATT_HEAD_DIM

MIB = 1024 * 1024


def _params(sem, vmem_mib):
    return pltpu.CompilerParams(dimension_semantics=sem, vmem_limit_bytes=vmem_mib * MIB)


def _sigmoid(v):
    return 1.0 / (1.0 + jnp.exp(-v))


def _layer_norm(z, g, b):
    mu = jnp.mean(z, axis=-1, keepdims=True)
    d = z - mu
    var = jnp.mean(d * d, axis=-1, keepdims=True)
    return d * lax.rsqrt(var + LN_EPS) * g + b


def _tables_kernel(pos_ref, fr_ref, fa_ref, cr_ref, sr_ref, ca_ref, s1_ref, s2_ref):
    pos = pos_ref[...].astype(F32)
    ang_r = pos * fr_ref[...]
    cr_ref[...] = jnp.cos(ang_r)
    sr_ref[...] = jnp.sin(ang_r)
    ang_a = pos * fa_ref[...]
    sa = jnp.sin(ang_a)
    d = lax.broadcasted_iota(jnp.int32, ang_a.shape, 1) & (ATT_HEAD_DIM - 1)
    half = ROPE_DIM // 2
    ca_ref[...] = jnp.cos(ang_a)
    s1_ref[...] = jnp.where(d < half, -sa, 0.0)
    s2_ref[...] = jnp.where((d >= half) & (d < ROPE_DIM), sa, 0.0)


def _rotary_tables(positions):
    t = positions.size
    tm = 1024
    pos_b = jnp.broadcast_to(positions.reshape(t, 1), (t, LANES))
    inv_r = 1.0 / (RET_THETA ** jnp.linspace(0.0, 1.0, RET_DK // 2, dtype=F32))
    inv_a = 1.0 / (ROPE_THETA ** (jnp.arange(0, ROPE_DIM, 2, dtype=F32) / ROPE_DIM))
    d = jnp.arange(LANES) % ATT_HEAD_DIM
    fa = jnp.where(d < ROPE_DIM, inv_a[d % (ROPE_DIM // 2)], 0.0).astype(F32)
    row = pl.BlockSpec((tm, LANES), lambda i: (i, 0))
    vec = pl.BlockSpec((1, LANES), lambda i: (0, 0))
    out = jax.ShapeDtypeStruct((t, LANES), F32)
    return pl.pallas_call(
        _tables_kernel,
        grid=(t // tm,),
        in_specs=[row, vec, vec],
        out_specs=[row] * 5,
        out_shape=[out] * 5,
        compiler_params=_params(("parallel",), 32),
        name="rotary_tables",
    )(pos_b, inv_r.reshape(1, LANES), fa.reshape(1, LANES))


def _ep_plain(acc, j):
    return acc


def _ep_swish(acc, j):
    return acc * _sigmoid(acc)


def _ep_sigmoid(acc, j):
    return _sigmoid(acc)


def _ep_ret_rotary(acc, j, cos_ref, sin_ref):
    scale = jnp.where(j == 0, RET_DK ** -0.5, 1.0)
    cos = cos_ref[...] * scale
    sin = sin_ref[...] * scale
    half = RET_DK // 2
    outs = []
    for c0 in range(0, acc.shape[1], RET_DK):
        a1 = acc[:, c0:c0 + half]
        a2 = acc[:, c0 + half:c0 + RET_DK]
        outs += [a1 * cos - a2 * sin, a2 * cos + a1 * sin]
    return jnp.concatenate(outs, axis=-1)


def _rope_lanes(a, ca, s1, s2):
    outs = []
    for c0 in range(0, a.shape[1], LANES):
        ac = a[:, c0:c0 + LANES]
        up = pltpu.roll(ac, LANES - ROPE_DIM // 2, 1)
        dn = pltpu.roll(ac, ROPE_DIM // 2, 1)
        outs.append(ac * ca + up * s1 + dn * s2)
    return jnp.concatenate(outs, axis=-1)


def _ep_att_q(acc, j, ca_ref, s1_ref, s2_ref):
    c = (ATT_HEAD_DIM ** -0.5) * LOG2E
    return _rope_lanes(acc, ca_ref[...] * c, s1_ref[...] * c, s2_ref[...] * c)


def _ep_att_kv(acc, j, ca_ref, s1_ref, s2_ref):
    k = _rope_lanes(acc[:, :KV_W], ca_ref[...], s1_ref[...], s2_ref[...])
    return jnp.concatenate([k, acc[:, KV_W:]], axis=-1)


def _proj_kernel(epilogue, x_ref, w_ref, *rest):
    *tab_refs, o_ref = rest
    acc = jnp.dot(x_ref[...], w_ref[...].astype(BF16), preferred_element_type=F32)
    o_ref[...] = epilogue(acc, pl.program_id(1), *tab_refs).astype(o_ref.dtype)


def _proj_segment(xb, w, col0, width, tn, epilogue, tables=(), name="in_proj"):
    t, k = xb.shape
    tm = 1024
    assert col0 % tn == 0 and width % tn == 0
    tab = pl.BlockSpec((tm, LANES), lambda i, j: (i, 0))
    return pl.pallas_call(
        functools.partial(_proj_kernel, epilogue),
        grid=(t // tm, width // tn),
        in_specs=[pl.BlockSpec((tm, k), lambda i, j: (i, 0)),
                  pl.BlockSpec((k, tn), lambda i, j: (0, col0 // tn + j))] + [tab] * len(tables),
        out_specs=pl.BlockSpec((tm, tn), lambda i, j: (i, j)),
        out_shape=jax.ShapeDtypeStruct((t, width), BF16),
        compiler_params=_params(("parallel", "arbitrary"), 56),
        name=name,
    )(xb, w, *tables)


def _retention_kernel(q_ref, k_ref, v_ref, gs_ref, ga_ref, gng_ref, gnb_ref,
                      o_ref, state_ref, dmask_ref, qdec_ref, kdec_ref, cdec_ref):
    c = RET_CHUNK
    h = pl.program_id(1)
    n = pl.program_id(2)

    @pl.when(n == 0)
    def _init():
        state_ref[...] = jnp.zeros_like(state_ref)

        def log_gamma(shape):
            hv = jnp.full(shape, h, jnp.int32).astype(F32)
            return jnp.log(1.0 - jnp.exp2(-5.0 - hv))

        ii = lax.broadcasted_iota(jnp.int32, (c, c), 0)
        jj = lax.broadcasted_iota(jnp.int32, (c, c), 1)
        diff = (ii - jj).astype(F32)
        dmask_ref[...] = jnp.where(
            diff >= 0.0, jnp.exp(log_gamma((c, c)) * jnp.maximum(diff, 0.0)), 0.0)
        idx = lax.broadcasted_iota(jnp.int32, (c, RET_DK), 0).astype(F32)
        lgl = log_gamma((c, RET_DK))
        qdec_ref[...] = jnp.exp(lgl * (idx + 1.0))
        kdec_ref[...] = jnp.exp(lgl * (c - 1.0 - idx))
        cdec_ref[...] = jnp.exp(log_gamma((8, LANES)) * float(c))

    qb = q_ref[...]
    kb = k_ref[...]
    v = v_ref[...]
    qd = (qb.astype(F32) * qdec_ref[...]).astype(BF16)
    kd = (kb.astype(F32) * kdec_ref[...]).astype(BF16)

    s = lax.dot_general(qb, kb, (((1,), (1,)), ((), ())), preferred_element_type=F32)
    s = s * dmask_ref[...]
    inner = jnp.dot(s.astype(BF16), v, preferred_element_type=F32)
    state = state_ref[...]
    cross = jnp.dot(qd, state.astype(BF16), preferred_element_type=F32)
    kv = lax.dot_general(kd, v, (((0,), (0,)), ((), ())), preferred_element_type=F32)
    state_ref[...] = state * cdec_ref[0:1, 0:1] + kv

    y = inner + cross
    mu = jnp.mean(y, axis=-1, keepdims=True)
    d = y - mu
    var = jnp.mean(d * d, axis=-1, keepdims=True)
    yn = d * lax.rsqrt(var + GN_EPS) * gng_ref[...] + gnb_ref[...]
    gate = ga_ref[...].astype(F32) * gs_ref[...].astype(F32)
    o_ref[...] = (gate * yn).astype(o_ref.dtype)


def _retention(rqk, rv, rgs, gates, gn_g, gn_b, bsz, seq):
    t = bsz * seq
    c = RET_CHUNK
    nc = seq // c

    def rows(width, col0):
        return pl.BlockSpec((c, width), lambda b, h, n: (b * nc + n, col0 + h))

    gn = pl.BlockSpec((1, RET_DV), lambda b, h, n: (0, h))
    return pl.pallas_call(
        _retention_kernel,
        grid=(bsz, RET_HEADS, nc),
        in_specs=[rows(RET_DK, 0), rows(RET_DK, RET_HEADS), rows(RET_DV, 0), rows(RET_DV, 0),
                  rows(RET_DV, 0), gn, gn],
        out_specs=pl.BlockSpec((c, RET_DV), lambda b, h, n: (b * nc + n, h)),
        out_shape=jax.ShapeDtypeStruct((t, D_MODEL), BF16),
        scratch_shapes=[pltpu.VMEM((RET_DK, RET_DV), F32), pltpu.VMEM((c, c), F32),
                        pltpu.VMEM((c, RET_DK), F32), pltpu.VMEM((c, RET_DK), F32),
                        pltpu.VMEM((8, LANES), F32)],
        compiler_params=_params(("parallel", "parallel", "arbitrary"), 32),
        name="retention",
    )(rqk, rqk, rv, rgs, gates, gn_g.reshape(1, -1), gn_b.reshape(1, -1))


def _pair_blocks(prev, cur, g):
    c0 = (g // 2) * LANES
    kk = jnp.concatenate([prev[:, c0:c0 + LANES], cur[:, c0:c0 + LANES]], axis=0).astype(F32)
    lane = lax.broadcasted_iota(jnp.int32, kk.shape, 1)
    hd = ATT_HEAD_DIM
    own = jnp.where((lane >= hd) if g % 2 else (lane < hd), kk, 0.0)
    other = pltpu.roll(own, hd, 1)
    lo, hi = (other, own) if g % 2 else (own, other)
    return lo.astype(BF16), hi.astype(BF16)


def _attention_kernel(sink_ref, q_ref, kp_ref, kc_ref, vp_ref, vc_ref, gb_ref, o_ref):
    i = pl.program_id(1)
    bq = ATT_BLOCK
    hd = ATT_HEAD_DIM
    nk = 2 * bq
    pairs = ATT_GROUP // 2
    kp, kc, vp, vc = kp_ref[...], kc_ref[...], vp_ref[...], vc_ref[...]

    qi = lax.broadcasted_iota(jnp.int32, (bq, nk), 0)
    kj = lax.broadcasted_iota(jnp.int32, (bq, nk), 1)
    dist = qi + bq - kj
    kmin = jnp.where(i > 0, 0, bq)
    mask = (dist >= 0) & (dist < WINDOW) & (kj >= kmin)
    lane = lax.broadcasted_iota(jnp.int32, (bq, LANES), 1)
    lane_k = lax.broadcasted_iota(jnp.int32, (nk, LANES), 1)
    ones_lo = jnp.where(lane_k < hd, 1.0, 0.0).astype(BF16)
    ones_hi = jnp.where(lane_k >= hd, 1.0, 0.0).astype(BF16)

    for g in range(ATT_KV_HEADS):
        k_lo, k_hi = _pair_blocks(kp, kc, g)
        v_lo, v_hi = _pair_blocks(vp, vc, g)
        kbd = jnp.concatenate([k_lo, k_hi], axis=0)
        vbd = jnp.concatenate([jnp.concatenate([v_lo, ones_lo], axis=1),
                               jnp.concatenate([v_hi, ones_hi], axis=1)], axis=0)
        col0 = g * ATT_GROUP * hd
        qp = jnp.concatenate(
            [q_ref[:, col0 + p * LANES:col0 + (p + 1) * LANES] for p in range(pairs)], axis=0)
        s = lax.dot_general(qp, kbd, (((1,), (1,)), ((), ())), preferred_element_type=F32)
        ps, sink_terms = [], []
        for p in range(pairs):
            halves, ms = [], []
            for e in range(2):
                sj = jnp.where(mask, s[p * bq:(p + 1) * bq, e * nk:(e + 1) * nk], -jnp.inf)
                sink = sink_ref[g * ATT_GROUP + 2 * p + e] * LOG2E
                m = jnp.maximum(jnp.max(sj, axis=-1, keepdims=True), sink)
                halves.append(jnp.exp2(sj - m).astype(BF16))
                ms.append((m, sink))
            ps.append(jnp.concatenate(halves, axis=1))
            sel = lane < hd
            sink_terms.append(jnp.exp2(jnp.where(sel, ms[0][1], ms[1][1])
                                       - jnp.where(sel, ms[0][0], ms[1][0])))
        o = jnp.dot(jnp.concatenate(ps, axis=0), vbd, preferred_element_type=F32)
        for p in range(pairs):
            num = o[p * bq:(p + 1) * bq, :LANES]
            den = o[p * bq:(p + 1) * bq, LANES:] + sink_terms[p]
            c1 = col0 + p * LANES
            gate = gb_ref[:, c1:c1 + LANES].astype(F32)
            o_ref[:, c1:c1 + LANES] = (gate * num / den).astype(o_ref.dtype)


def _attention(aq, akv, gates, sinks, bsz, seq):
    t = bsz * seq
    bq = ATT_BLOCK
    nb = seq // bq

    def cur(width, col):
        return pl.BlockSpec((bq, width), lambda b, i: (b * nb + i, col))

    def prev(width, col):
        return pl.BlockSpec((bq, width), lambda b, i: (b * nb + jnp.maximum(i - 1, 0), col))

    return pl.pallas_call(
        _attention_kernel,
        grid=(bsz, nb),
        in_specs=[pl.BlockSpec(memory_space=pltpu.SMEM), cur(D_MODEL, 0),
                  prev(KV_W, 0), cur(KV_W, 0), prev(KV_W, 1), cur(KV_W, 1), cur(D_MODEL, 1)],
        out_specs=pl.BlockSpec((bq, D_MODEL), lambda b, i: (b * nb + i, 0)),
        out_shape=jax.ShapeDtypeStruct((t, D_MODEL), BF16),
        compiler_params=_params(("parallel", "arbitrary"), 32),
        name="attention",
    )(sinks, aq, akv, akv, akv, akv, gates)


def _out_proj_kernel(alpha, ret_ref, att_ref, x_ref, w_ref, g_ref, b_ref, h_ref):
    merged = (ret_ref[...].astype(F32) + att_ref[...].astype(F32)).astype(BF16)
    y = jnp.dot(merged, w_ref[...], preferred_element_type=F32)
    h_ref[...] = _layer_norm(alpha * x_ref[...] + y, g_ref[...], b_ref[...])


def _out_proj(ret, att, x2, wb, ln_g, ln_b, alpha):
    t, d = x2.shape
    tm = 512
    row = pl.BlockSpec((tm, d), lambda i: (i, 0))
    vec = pl.BlockSpec((1, d), lambda i: (0, 0))
    return pl.pallas_call(
        functools.partial(_out_proj_kernel, alpha),
        grid=(t // tm,),
        in_specs=[row, row, row, pl.BlockSpec((d, d), lambda i: (0, 0)), vec, vec],
        out_specs=row,
        out_shape=jax.ShapeDtypeStruct((t, d), F32),
        compiler_params=_params(("parallel",), 48),
        name="out_proj_ln",
    )(ret, att, x2, wb, ln_g.reshape(1, d), ln_b.reshape(1, d))


def _ffn_kernel(alpha, h_ref, wg_ref, wu_ref, wd_ref, g_ref, b_ref, o_ref, hb_ref):
    f = pl.program_id(1)

    @pl.when(f == 0)
    def _init():
        h = h_ref[...]
        hb_ref[...] = h.astype(BF16)
        o_ref[...] = alpha * h

    hb = hb_ref[...]
    gt = jnp.dot(hb, wg_ref[...], preferred_element_type=F32)
    ut = jnp.dot(hb, wu_ref[...], preferred_element_type=F32)
    a = (gt * _sigmoid(gt) * ut).astype(BF16)
    o_ref[...] += jnp.dot(a, wd_ref[...], preferred_element_type=F32)

    @pl.when(f == pl.num_programs(1) - 1)
    def _finish():
        o_ref[...] = _layer_norm(o_ref[...], g_ref[...], b_ref[...])


def _ffn(h1, wgb, wub, wdb, ln_g, ln_b, alpha):
    t, d = h1.shape
    dff = wgb.shape[1]
    tm, tf = 512, 512
    row = pl.BlockSpec((tm, d), lambda i, f: (i, 0))
    vec = pl.BlockSpec((1, d), lambda i, f: (0, 0))
    return pl.pallas_call(
        functools.partial(_ffn_kernel, alpha),
        grid=(t // tm, dff // tf),
        in_specs=[row,
                  pl.BlockSpec((d, tf), lambda i, f: (0, f)),
                  pl.BlockSpec((d, tf), lambda i, f: (0, f)),
                  pl.BlockSpec((tf, d), lambda i, f: (f, 0)),
                  vec, vec],
        out_specs=row,
        out_shape=jax.ShapeDtypeStruct((t, d), F32),
        scratch_shapes=[pltpu.VMEM((tm, d), BF16)],
        compiler_params=_params(("parallel", "arbitrary"), 48),
        name="ffn_ln",
    )(h1, wgb, wub, wdb, ln_g.reshape(1, d), ln_b.reshape(1, d))


def kernel(x, positions, w_in, ret_gn_g, ret_gn_b, att_sinks, w_out, ln1_g, ln1_b,
           w_gate, w_up, w_down, ln2_g, ln2_b):
    bsz, seq, d = x.shape
    depth = w_in.shape[0]
    alpha = (2.0 * depth) ** 0.25
    cos_r, sin_r, ca, s1, s2 = _rotary_tables(positions)
    h = x.reshape(bsz * seq, d)
    for l in range(depth):
        hb = h.astype(BF16)
        w = w_in[l]
        rqk = _proj_segment(hb, w, OFF_RQ, OFF_RV - OFF_RQ, 1024, _ep_ret_rotary,
                            (cos_r, sin_r), "in_proj_ret_qk")
        rv = _proj_segment(hb, w, OFF_RV, D_MODEL, 1024, _ep_plain, (), "in_proj_ret_v")
        rgs = _proj_segment(hb, w, OFF_RG, D_MODEL, 1024, _ep_swish, (), "in_proj_ret_g")
        aq = _proj_segment(hb, w, OFF_AQ, D_MODEL, 1024, _ep_att_q, (ca, s1, s2), "in_proj_att_q")
        akv = _proj_segment(hb, w, OFF_AK, 2 * KV_W, 2 * KV_W, _ep_att_kv, (ca, s1, s2),
                            "in_proj_att_kv")
        gates = _proj_segment(hb, w, OFF_GA, 2 * D_MODEL, 512, _ep_sigmoid, (), "in_proj_gates")
        ret = _retention(rqk, rv, rgs, gates, ret_gn_g[l], ret_gn_b[l], bsz, seq)
        att = _attention(aq, akv, gates, att_sinks[l], bsz, seq)
        h = _out_proj(ret, att, h, w_out[l].astype(BF16), ln1_g[l], ln1_b[l], alpha)
        h = _ffn(h, w_gate[l].astype(BF16), w_up[l].astype(BF16), w_down[l].astype(BF16),
                 ln2_g[l], ln2_b[l], alpha)
    return h.reshape(bsz, seq, d)
```

```python
import functools
import math

import jax
import jax.numpy as jnp
from jax import lax
from jax.experimental import pallas as pl
from jax.experimental.pallas import tpu as pltpu

F32 = jnp.float32
BF16 = jnp.bfloat16

D_MODEL = 2048
RET_HEADS = 4
RET_DV = D_MODEL // RET_HEADS
RET_DK = RET_DV // 2
RET_CHUNK = 256
RET_THETA = 10000.0
ATT_HEAD_DIM = 64
ATT_HEADS = D_MODEL // ATT_HEAD_DIM
ATT_KV_HEADS = ATT_HEADS // 8
ATT_GROUP = ATT_HEADS // ATT_KV_HEADS
WINDOW = 128
ATT_BLOCK = WINDOW
ROPE_THETA = 500000.0
ROPE_DIM = ATT_HEAD_DIM // 4
LN_EPS = 1e-5
GN_EPS = 1e-5

LANES = 128
LOG2E = math.log2(math.e)

OFF_RQ = 0
OFF_RK = OFF_RQ + RET_HEADS * RET_DK
OFF_RV = OFF_RK + RET_HEADS * RET_DK
OFF_RG = OFF_RV + RET_HEADS * RET_DV
OFF_AQ = OFF_RG + RET_HEADS * RET_DV
OFF_AK = OFF_AQ + ATT_HEADS * ATT_HEAD_DIM
OFF_AV = OFF_AK + ATT_KV_HEADS * ATT_HEAD_DIM
OFF_GA = OFF_AV + ATT_KV_HEADS * ATT_HEAD_DIM
OFF_GB = OFF_GA + D_MODEL
KV_W = ATT_KV_HEADS * ATT_HEAD_DIM

MIB = 1024 * 1024


def _params(sem, vmem_mib):
    return pltpu.CompilerParams(dimension_semantics=sem, vmem_limit_bytes=vmem_mib * MIB)


def _sigmoid(v):
    return 1.0 / (1.0 + jnp.exp(-v))


def _layer_norm(z, g, b):
    mu = jnp.mean(z, axis=-1, keepdims=True)
    d = z - mu
    var = jnp.mean(d * d, axis=-1, keepdims=True)
    return d * lax.rsqrt(var + LN_EPS) * g + b


def _tables_kernel(pos_ref, fr_ref, fa_ref, cr_ref, sr_ref, ca_ref, s1_ref, s2_ref):
    pos = pos_ref[...].astype(F32)
    ang_r = pos * fr_ref[...]
    cr_ref[...] = jnp.cos(ang_r)
    sr_ref[...] = jnp.sin(ang_r)
    ang_a = pos * fa_ref[...]
    sa = jnp.sin(ang_a)
    d = lax.broadcasted_iota(jnp.int32, ang_a.shape, 1) & (ATT_HEAD_DIM - 1)
    half = ROPE_DIM // 2
    ca_ref[...] = jnp.cos(ang_a)
    s1_ref[...] = jnp.where(d < half, -sa, 0.0)
    s2_ref[...] = jnp.where((d >= half) & (d < ROPE_DIM), sa, 0.0)


def _rotary_tables(positions):
    t = positions.size
    tm = 1024
    pos_b = jnp.broadcast_to(positions.reshape(t, 1), (t, LANES))
    inv_r = 1.0 / (RET_THETA ** jnp.linspace(0.0, 1.0, RET_DK // 2, dtype=F32))
    inv_a = 1.0 / (ROPE_THETA ** (jnp.arange(0, ROPE_DIM, 2, dtype=F32) / ROPE_DIM))
    d = jnp.arange(LANES) % ATT_HEAD_DIM
    fa = jnp.where(d < ROPE_DIM, inv_a[d % (ROPE_DIM // 2)], 0.0).astype(F32)
    row = pl.BlockSpec((tm, LANES), lambda i: (i, 0))
    vec = pl.BlockSpec((1, LANES), lambda i: (0, 0))
    out = jax.ShapeDtypeStruct((t, LANES), F32)
    return pl.pallas_call(
        _tables_kernel,
        grid=(t // tm,),
        in_specs=[row, vec, vec],
        out_specs=[row] * 5,
        out_shape=[out] * 5,
        compiler_params=_params(("parallel",), 32),
        name="rotary_tables",
    )(pos_b, inv_r.reshape(1, LANES), fa.reshape(1, LANES))


def _ep_plain(acc, j):
    return acc


def _ep_swish(acc, j):
    return acc * _sigmoid(acc)


def _ep_sigmoid(acc, j):
    return _sigmoid(acc)


def _ep_ret_rotary(acc, j, cos_ref, sin_ref):
    scale = jnp.where(j == 0, RET_DK ** -0.5, 1.0)
    cos = cos_ref[...] * scale
    sin = sin_ref[...] * scale
    half = RET_DK // 2
    outs = []
    for c0 in range(0, acc.shape[1], RET_DK):
        a1 = acc[:, c0:c0 + half]
        a2 = acc[:, c0 + half:c0 + RET_DK]
        outs += [a1 * cos - a2 * sin, a2 * cos + a1 * sin]
    return jnp.concatenate(outs, axis=-1)


def _rope_lanes(a, ca, s1, s2):
    outs = []
    for c0 in range(0, a.shape[1], LANES):
        ac = a[:, c0:c0 + LANES]
        up = pltpu.roll(ac, LANES - ROPE_DIM // 2, 1)
        dn = pltpu.roll(ac, ROPE_DIM // 2, 1)
        outs.append(ac * ca + up * s1 + dn * s2)
    return jnp.concatenate(outs, axis=-1)


def _ep_att_q(acc, j, ca_ref, s1_ref, s2_ref):
    c = (ATT_HEAD_DIM ** -0.5) * LOG2E
    return _rope_lanes(acc, ca_ref[...] * c, s1_ref[...] * c, s2_ref[...] * c)


def _ep_att_kv(acc, j, ca_ref, s1_ref, s2_ref):
    k = _rope_lanes(acc[:, :KV_W], ca_ref[...], s1_ref[...], s2_ref[...])
    return jnp.concatenate([k, acc[:, KV_W:]], axis=-1)


def _proj_kernel(epilogue, x_ref, w_ref, *rest):
    *tab_refs, o_ref, wb_ref = rest

    @pl.when(pl.program_id(1) == 0)
    def _cast_weights():
        wb_ref[...] = w_ref[...].astype(BF16)

    acc = jnp.dot(x_ref[...], wb_ref[...], preferred_element_type=F32)
    o_ref[...] = epilogue(acc, pl.program_id(0), *tab_refs).astype(o_ref.dtype)


def _proj_segment(xb, w, col0, width, tn, epilogue, tables=(), name="in_proj"):
    t, k = xb.shape
    tm = 1024
    assert col0 % LANES == 0 and width % tn == 0
    tab = pl.BlockSpec((tm, LANES), lambda j, i: (i, 0))
    return pl.pallas_call(
        functools.partial(_proj_kernel, epilogue),
        grid=(width // tn, t // tm),
        in_specs=[pl.BlockSpec((tm, k), lambda j, i: (i, 0)),
                  pl.BlockSpec((pl.Element(k), pl.Element(tn)),
                               lambda j, i: (0, pl.multiple_of(col0 + j * tn, LANES)))]
                 + [tab] * len(tables),
        out_specs=pl.BlockSpec((tm, tn), lambda j, i: (i, j)),
        out_shape=jax.ShapeDtypeStruct((t, width), BF16),
        scratch_shapes=[pltpu.VMEM((k, tn), BF16)],
        compiler_params=_params(("parallel", "arbitrary"), 56),
        name=name,
    )(xb, w, *tables)


def _retention_kernel(q_ref, k_ref, v_ref, gs_ref, ga_ref, gng_ref, gnb_ref,
                      o_ref, state_ref, dmask_ref, qdec_ref, kdec_ref, cdec_ref):
    c = RET_CHUNK

    @pl.when(pl.program_id(1) == 0)
    def _init():
        state_ref[...] = jnp.zeros_like(state_ref)
        ii = lax.broadcasted_iota(jnp.int32, (c, c), 0)
        jj = lax.broadcasted_iota(jnp.int32, (c, c), 1)
        diff = (ii - jj).astype(F32)
        idx = lax.broadcasted_iota(jnp.int32, (c, RET_DK), 0).astype(F32)
        for h in range(RET_HEADS):
            def log_gamma(shape):
                return jnp.log(1.0 - jnp.exp2(-5.0 - jnp.full(shape, float(h), F32)))

            dmask_ref[h] = jnp.where(
                diff >= 0.0, jnp.exp(log_gamma((c, c)) * jnp.maximum(diff, 0.0)), 0.0)
            lgl = log_gamma((c, RET_DK))
            qdec_ref[h] = jnp.exp(lgl * (idx + 1.0))
            kdec_ref[h] = jnp.exp(lgl * (c - 1.0 - idx))
            cdec_ref[h] = jnp.exp(log_gamma((8, LANES)) * float(c))

    for h in range(RET_HEADS):
        ksl = slice(h * RET_DK, (h + 1) * RET_DK)
        vsl = slice(h * RET_DV, (h + 1) * RET_DV)
        qb = q_ref[:, ksl]
        kb = k_ref[:, ksl]
        v = v_ref[:, vsl]
        qd = (qb.astype(F32) * qdec_ref[h]).astype(BF16)
        kd = (kb.astype(F32) * kdec_ref[h]).astype(BF16)

        s = lax.dot_general(qb, kb, (((1,), (1,)), ((), ())), preferred_element_type=F32)
        s = s * dmask_ref[h]
        inner = jnp.dot(s.astype(BF16), v, preferred_element_type=F32)
        state = state_ref[h]
        cross = jnp.dot(qd, state.astype(BF16), preferred_element_type=F32)
        kv = lax.dot_general(kd, v, (((0,), (0,)), ((), ())), preferred_element_type=F32)
        state_ref[h] = state * cdec_ref[h, 0:1, 0:1] + kv

        y = inner + cross
        mu = jnp.mean(y, axis=-1, keepdims=True)
        d = y - mu
        var = jnp.mean(d * d, axis=-1, keepdims=True)
        yn = d * lax.rsqrt(var + GN_EPS) * gng_ref[:, vsl] + gnb_ref[:, vsl]
        gate = ga_ref[:, vsl].astype(F32) * gs_ref[:, vsl].astype(F32)
        o_ref[:, vsl] = (gate * yn).astype(o_ref.dtype)


def _retention(rqk, rv, rgs, gates, gn_g, gn_b, bsz, seq):
    t = bsz * seq
    c = RET_CHUNK
    nc = seq // c
    nh = RET_HEADS

    def rows(width, col):
        return pl.BlockSpec((c, width), lambda b, n: (b * nc + n, col))

    gn = pl.BlockSpec((1, D_MODEL), lambda b, n: (0, 0))
    return pl.pallas_call(
        _retention_kernel,
        grid=(bsz, nc),
        in_specs=[rows(nh * RET_DK, 0), rows(nh * RET_DK, 1), rows(D_MODEL, 0), rows(D_MODEL, 0),
                  rows(D_MODEL, 0), gn, gn],
        out_specs=rows(D_MODEL, 0),
        out_shape=jax.ShapeDtypeStruct((t, D_MODEL), BF16),
        scratch_shapes=[pltpu.VMEM((nh, RET_DK, RET_DV), F32), pltpu.VMEM((nh, c, c), F32),
                        pltpu.VMEM((nh, c, RET_DK), F32), pltpu.VMEM((nh, c, RET_DK), F32),
                        pltpu.VMEM((nh, 8, LANES), F32)],
        compiler_params=_params(("parallel", "arbitrary"), 32),
        name="retention",
    )(rqk, rqk, rv, rgs, gates, gn_g.reshape(1, -1), gn_b.reshape(1, -1))


def _pair_blocks(prev, cur, g):
    c0 = (g // 2) * LANES
    kk = jnp.concatenate([prev[:, c0:c0 + LANES], cur[:, c0:c0 + LANES]], axis=0).astype(F32)
    lane = lax.broadcasted_iota(jnp.int32, kk.shape, 1)
    hd = ATT_HEAD_DIM
    own = jnp.where((lane >= hd) if g % 2 else (lane < hd), kk, 0.0)
    other = pltpu.roll(own, hd, 1)
    lo, hi = (other, own) if g % 2 else (own, other)
    return lo.astype(BF16), hi.astype(BF16)


def _attention_kernel(sink_ref, q_ref, kp_ref, kc_ref, vp_ref, vc_ref, gb_ref, o_ref):
    i = pl.program_id(1)
    bq = ATT_BLOCK
    hd = ATT_HEAD_DIM
    nk = 2 * bq
    pairs = ATT_GROUP // 2
    kp, kc, vp, vc = kp_ref[...], kc_ref[...], vp_ref[...], vc_ref[...]

    qi = lax.broadcasted_iota(jnp.int32, (bq, nk), 0)
    kj = lax.broadcasted_iota(jnp.int32, (bq, nk), 1)
    dist = qi + bq - kj
    kmin = jnp.where(i > 0, 0, bq)
    mask = (dist >= 0) & (dist < WINDOW) & (kj >= kmin)
    lane = lax.broadcasted_iota(jnp.int32, (bq, LANES), 1)
    lane_k = lax.broadcasted_iota(jnp.int32, (nk, LANES), 1)
    ones_lo = jnp.where(lane_k < hd, 1.0, 0.0).astype(BF16)
    ones_hi = jnp.where(lane_k >= hd, 1.0, 0.0).astype(BF16)

    for g in range(ATT_KV_HEADS):
        k_lo, k_hi = _pair_blocks(kp, kc, g)
        v_lo, v_hi = _pair_blocks(vp, vc, g)
        kbd = jnp.concatenate([k_lo, k_hi], axis=0)
        vbd = jnp.concatenate([jnp.concatenate([v_lo, ones_lo], axis=1),
                               jnp.concatenate([v_hi, ones_hi], axis=1)], axis=0)
        col0 = g * ATT_GROUP * hd
        qp = jnp.concatenate(
            [q_ref[:, col0 + p * LANES:col0 + (p + 1) * LANES] for p in range(pairs)], axis=0)
        s = lax.dot_general(qp, kbd, (((1,), (1,)), ((), ())), preferred_element_type=F32)
        ps, sink_terms = [], []
        for p in range(pairs):
            halves, ms = [], []
            for e in range(2):
                sj = jnp.where(mask, s[p * bq:(p + 1) * bq, e * nk:(e + 1) * nk], -jnp.inf)
                sink = sink_ref[g * ATT_GROUP + 2 * p + e] * LOG2E
                m = jnp.maximum(jnp.max(sj, axis=-1, keepdims=True), sink)
                halves.append(jnp.exp2(sj - m).astype(BF16))
                ms.append((m, sink))
            ps.append(jnp.concatenate(halves, axis=1))
            sel = lane < hd
            sink_terms.append(jnp.exp2(jnp.where(sel, ms[0][1], ms[1][1])
                                       - jnp.where(sel, ms[0][0], ms[1][0])))
        o = jnp.dot(jnp.concatenate(ps, axis=0), vbd, preferred_element_type=F32)
        for p in range(pairs):
            num = o[p * bq:(p + 1) * bq, :LANES]
            den = o[p * bq:(p + 1) * bq, LANES:] + sink_terms[p]
            c1 = col0 + p * LANES
            gate = gb_ref[:, c1:c1 + LANES].astype(F32)
            o_ref[:, c1:c1 + LANES] = (gate * num / den).astype(o_ref.dtype)


def _attention(aq, akv, gates, sinks, bsz, seq):
    t = bsz * seq
    bq = ATT_BLOCK
    nb = seq // bq

    def cur(width, col):
        return pl.BlockSpec((bq, width), lambda b, i: (b * nb + i, col))

    def prev(width, col):
        return pl.BlockSpec((bq, width), lambda b, i: (b * nb + jnp.maximum(i - 1, 0), col))

    return pl.pallas_call(
        _attention_kernel,
        grid=(bsz, nb),
        in_specs=[pl.BlockSpec(memory_space=pltpu.SMEM), cur(D_MODEL, 0),
                  prev(KV_W, 0), cur(KV_W, 0), prev(KV_W, 1), cur(KV_W, 1), cur(D_MODEL, 1)],
        out_specs=pl.BlockSpec((bq, D_MODEL), lambda b, i: (b * nb + i, 0)),
        out_shape=jax.ShapeDtypeStruct((t, D_MODEL), BF16),
        compiler_params=_params(("parallel", "arbitrary"), 32),
        name="attention",
    )(sinks, aq, akv, akv, akv, akv, gates)


def _out_proj_kernel(alpha, ret_ref, att_ref, x_ref, w_ref, g_ref, b_ref, h_ref, hb_ref):
    half = h_ref.shape[0] // 2
    for r0 in (0, half):
        rs = slice(r0, r0 + half)
        merged = (ret_ref[rs, :].astype(F32) + att_ref[rs, :].astype(F32)).astype(BF16)
        y = jnp.dot(merged, w_ref[...], preferred_element_type=F32)
        h = _layer_norm(alpha * x_ref[rs, :] + y, g_ref[...], b_ref[...])
        h_ref[rs, :] = h
        hb_ref[rs, :] = h.astype(BF16)


def _out_proj(ret, att, x2, wb, ln_g, ln_b, alpha):
    t, d = x2.shape
    tm = 512
    row = pl.BlockSpec((tm, d), lambda i: (i, 0))
    vec = pl.BlockSpec((1, d), lambda i: (0, 0))
    return pl.pallas_call(
        functools.partial(_out_proj_kernel, alpha),
        grid=(t // tm,),
        in_specs=[row, row, row, pl.BlockSpec((d, d), lambda i: (0, 0)), vec, vec],
        out_specs=[row, row],
        out_shape=[jax.ShapeDtypeStruct((t, d), F32), jax.ShapeDtypeStruct((t, d), BF16)],
        compiler_params=_params(("parallel",), 48),
        name="out_proj_ln",
    )(ret, att, x2, wb, ln_g.reshape(1, d), ln_b.reshape(1, d))


def _ffn_kernel(alpha, hb_ref, h_hbm, wg_ref, wu_ref, wd_ref, g_ref, b_ref, o_ref, hres_ref, sem):
    i = pl.program_id(0)
    f = pl.program_id(1)
    tm = o_ref.shape[0]
    res_copy = pltpu.make_async_copy(h_hbm.at[pl.ds(i * tm, tm), :], hres_ref, sem)

    @pl.when(f == 0)
    def _init():
        res_copy.start()
        o_ref[...] = jnp.zeros_like(o_ref)

    hb = hb_ref[...]
    gt = jnp.dot(hb, wg_ref[...].astype(BF16), preferred_element_type=F32)
    ut = jnp.dot(hb, wu_ref[...].astype(BF16), preferred_element_type=F32)
    a = (gt * _sigmoid(gt) * ut).astype(BF16)
    o_ref[...] += jnp.dot(a, wd_ref[...].astype(BF16), preferred_element_type=F32)

    @pl.when(f == pl.num_programs(1) - 1)
    def _finish():
        res_copy.wait()
        o_ref[...] = _layer_norm(alpha * hres_ref[...] + o_ref[...], g_ref[...], b_ref[...])


def _ffn(h1, h1b, wg, wu, wd, ln_g, ln_b, alpha):
    t, d = h1.shape
    dff = wg.shape[1]
    tm, tf = 1024, 256
    row = pl.BlockSpec((tm, d), lambda i, f: (i, 0))
    vec = pl.BlockSpec((1, d), lambda i, f: (0, 0))
    return pl.pallas_call(
        functools.partial(_ffn_kernel, alpha),
        grid=(t // tm, dff // tf),
        in_specs=[row,
                  pl.BlockSpec(memory_space=pl.ANY),
                  pl.BlockSpec((d, tf), lambda i, f: (0, f)),
                  pl.BlockSpec((d, tf), lambda i, f: (0, f)),
                  pl.BlockSpec((tf, d), lambda i, f: (f, 0)),
                  vec, vec],
        out_specs=row,
        out_shape=jax.ShapeDtypeStruct((t, d), F32),
        scratch_shapes=[pltpu.VMEM((tm, d), F32), pltpu.SemaphoreType.DMA(())],
        compiler_params=_params(("parallel", "arbitrary"), 56),
        name="ffn_ln",
    )(h1b, h1, wg, wu, wd, ln_g.reshape(1, d), ln_b.reshape(1, d))


def kernel(x, positions, w_in, ret_gn_g, ret_gn_b, att_sinks, w_out, ln1_g, ln1_b,
           w_gate, w_up, w_down, ln2_g, ln2_b):
    bsz, seq, d = x.shape
    depth = w_in.shape[0]
    alpha = (2.0 * depth) ** 0.25
    cos_r, sin_r, ca, s1, s2 = _rotary_tables(positions)
    h = x.reshape(bsz * seq, d)
    for l in range(depth):
        hb = h.astype(BF16)
        w = w_in[l]
        rqk = _proj_segment(hb, w, OFF_RQ, OFF_RV - OFF_RQ, 1024, _ep_ret_rotary,
                            (cos_r, sin_r), "in_proj_ret_qk")
        rv = _proj_segment(hb, w, OFF_RV, D_MODEL, 1024, _ep_plain, (), "in_proj_ret_v")
        rgs = _proj_segment(hb, w, OFF_RG, D_MODEL, 1024, _ep_swish, (), "in_proj_ret_g")
        aq = _proj_segment(hb, w, OFF_AQ, D_MODEL, 1024, _ep_att_q, (ca, s1, s2), "in_proj_att_q")
        akv = _proj_segment(hb, w, OFF_AK, 2 * KV_W, 2 * KV_W, _ep_att_kv, (ca, s1, s2),
                            "in_proj_att_kv")
        gates = _proj_segment(hb, w, OFF_GA, 2 * D_MODEL, 1024, _ep_sigmoid, (), "in_proj_gates")
        ret = _retention(rqk, rv, rgs, gates, ret_gn_g[l], ret_gn_b[l], bsz, seq)
        att = _attention(aq, akv, gates, att_sinks[l], bsz, seq)
        h, hb = _out_proj(ret, att, h, w_out[l].astype(BF16), ln1_g[l], ln1_b[l], alpha)
        h = _ffn(h, hb, w_gate[l], w_up[l], w_down[l], ln2_g[l], ln2_b[l], alpha)
    return h.reshape(bsz, seq, d)
```

```python
import functools
import math

import jax
import jax.numpy as jnp
from jax import lax
from jax.experimental import pallas as pl
from jax.experimental.pallas import tpu as pltpu

F32 = jnp.float32
BF16 = jnp.bfloat16

D_MODEL = 2048
RET_HEADS = 4
RET_DV = D_MODEL // RET_HEADS
RET_DK = RET_DV // 2
RET_CHUNK = 256
RET_THETA = 10000.0
ATT_HEAD_DIM = 64
ATT_HEADS = D_MODEL // ATT_HEAD_DIM
ATT_KV_HEADS = ATT_HEADS // 8
ATT_GROUP = ATT_HEADS // ATT_KV_HEADS
WINDOW = 128
ATT_BLOCK = WINDOW
ROPE_THETA = 500000.0
ROPE_DIM = ATT_HEAD_DIM // 4
LN_EPS = 1e-5
GN_EPS = 1e-5

LANES = 128
LOG2E = math.log2(math.e)

OFF_RQ = 0
OFF_RK = OFF_RQ + RET_HEADS * RET_DK
OFF_RV = OFF_RK + RET_HEADS * RET_DK
OFF_RG = OFF_RV + RET_HEADS * RET_DV
OFF_AQ = OFF_RG + RET_HEADS * RET_DV
OFF_AK = OFF_AQ + ATT_HEADS * ATT_HEAD_DIM
OFF_AV = OFF_AK + ATT_KV_HEADS * ATT_HEAD_DIM
OFF_GA = OFF_AV + ATT_KV_HEADS * ATT_HEAD_DIM
OFF_GB = OFF_GA + D_MODEL
KV_W = ATT_KV_HEADS * ATT_HEAD_DIM

MIB = 1024 * 1024


def _params(sem, vmem_mib, flags=None):
    return pltpu.CompilerParams(dimension_semantics=sem, vmem_limit_bytes=vmem_mib * MIB,
                                flags=flags)


def _sigmoid(v):
    return 1.0 / (1.0 + jnp.exp(-v))


def _layer_norm(z, g, b):
    mu = jnp.mean(z, axis=-1, keepdims=True)
    d = z - mu
    var = jnp.mean(d * d, axis=-1, keepdims=True)
    return d * lax.rsqrt(var + LN_EPS) * g + b


def _prep_kernel(pos_ref, fr_ref, fa_ref, x_ref, cr_ref, sr_ref, ca_ref, s1_ref, s2_ref, xb_ref):
    xb_ref[...] = x_ref[...].astype(BF16)
    pos = pos_ref[...].astype(F32)
    ang_r = pos * fr_ref[...]
    cr_ref[...] = jnp.cos(ang_r)
    sr_ref[...] = jnp.sin(ang_r)
    ang_a = pos * fa_ref[...]
    sa = jnp.sin(ang_a)
    d = lax.broadcasted_iota(jnp.int32, ang_a.shape, 1) & (ATT_HEAD_DIM - 1)
    half = ROPE_DIM // 2
    ca_ref[...] = jnp.cos(ang_a)
    s1_ref[...] = jnp.where(d < half, -sa, 0.0)
    s2_ref[...] = jnp.where((d >= half) & (d < ROPE_DIM), sa, 0.0)


def _prep(positions, x2):
    t, dm = x2.shape
    tm = 1024
    pos_b = jnp.broadcast_to(positions.reshape(t, 1), (t, LANES))
    inv_r = 1.0 / (RET_THETA ** jnp.linspace(0.0, 1.0, RET_DK // 2, dtype=F32))
    inv_a = 1.0 / (ROPE_THETA ** (jnp.arange(0, ROPE_DIM, 2, dtype=F32) / ROPE_DIM))
    d = jnp.arange(LANES) % ATT_HEAD_DIM
    fa = jnp.where(d < ROPE_DIM, inv_a[d % (ROPE_DIM // 2)], 0.0).astype(F32)
    row = pl.BlockSpec((tm, LANES), lambda i: (i, 0))
    xrow = pl.BlockSpec((tm, dm), lambda i: (i, 0))
    vec = pl.BlockSpec((1, LANES), lambda i: (0, 0))
    tab = jax.ShapeDtypeStruct((t, LANES), F32)
    return pl.pallas_call(
        _prep_kernel,
        grid=(t // tm,),
        in_specs=[row, vec, vec, xrow],
        out_specs=[row] * 5 + [xrow],
        out_shape=[tab] * 5 + [jax.ShapeDtypeStruct((t, dm), BF16)],
        compiler_params=_params(("parallel",), 40),
        name="rotary_tables_xcast",
    )(pos_b, inv_r.reshape(1, LANES), fa.reshape(1, LANES), x2)


def _ep_plain(acc, j):
    return acc


def _ep_swish(acc, j):
    return acc * _sigmoid(acc)


def _ep_sigmoid(acc, j):
    return _sigmoid(acc)


def _ep_ret_rotary(acc, j, cos_ref, sin_ref):
    scale = jnp.where(j == 0, RET_DK ** -0.5, 1.0)
    cos = cos_ref[...] * scale
    sin = sin_ref[...] * scale
    half = RET_DK // 2
    outs = []
    for c0 in range(0, acc.shape[1], RET_DK):
        a1 = acc[:, c0:c0 + half]
        a2 = acc[:, c0 + half:c0 + RET_DK]
        outs += [a1 * cos - a2 * sin, a2 * cos + a1 * sin]
    return jnp.concatenate(outs, axis=-1)


def _rope_lanes(a, ca, s1, s2):
    outs = []
    for c0 in range(0, a.shape[1], LANES):
        ac = a[:, c0:c0 + LANES]
        up = pltpu.roll(ac, LANES - ROPE_DIM // 2, 1)
        dn = pltpu.roll(ac, ROPE_DIM // 2, 1)
        outs.append(ac * ca + up * s1 + dn * s2)
    return jnp.concatenate(outs, axis=-1)


def _ep_att_q(acc, j, ca_ref, s1_ref, s2_ref):
    c = (ATT_HEAD_DIM ** -0.5) * LOG2E
    return _rope_lanes(acc, ca_ref[...] * c, s1_ref[...] * c, s2_ref[...] * c)


def _ep_att_kv(acc, j, ca_ref, s1_ref, s2_ref):
    k = _rope_lanes(acc[:, :KV_W], ca_ref[...], s1_ref[...], s2_ref[...])
    return jnp.concatenate([k, acc[:, KV_W:]], axis=-1)


def _proj_kernel(epilogue, x_ref, w_ref, *rest):
    *tab_refs, o_ref, wb_ref = rest

    @pl.when(pl.program_id(1) == 0)
    def _cast_weights():
        wb_ref[...] = w_ref[...].astype(BF16)

    acc = jnp.dot(x_ref[...], wb_ref[...], preferred_element_type=F32)
    o_ref[...] = epilogue(acc, pl.program_id(0), *tab_refs).astype(o_ref.dtype)


def _proj_segment(xb, w, col0, width, tn, epilogue, tables=(), name="in_proj"):
    t, k = xb.shape
    tm = 1024
    assert col0 % LANES == 0 and width % tn == 0
    tab = pl.BlockSpec((tm, LANES), lambda j, i: (i, 0))
    return pl.pallas_call(
        functools.partial(_proj_kernel, epilogue),
        grid=(width // tn, t // tm),
        in_specs=[pl.BlockSpec((tm, k), lambda j, i: (i, 0)),
                  pl.BlockSpec((pl.Element(k), pl.Element(tn)),
                               lambda j, i: (0, pl.multiple_of(col0 + j * tn, LANES)))]
                 + [tab] * len(tables),
        out_specs=pl.BlockSpec((tm, tn), lambda j, i: (i, j)),
        out_shape=jax.ShapeDtypeStruct((t, width), BF16),
        scratch_shapes=[pltpu.VMEM((k, tn), BF16)],
        compiler_params=_params(("parallel", "arbitrary"), 56),
        name=name,
    )(xb, w, *tables)


def _retention_kernel(q_ref, k_ref, v_ref, gs_ref, ga_ref, gng_ref, gnb_ref,
                      o_ref, state_ref, dmask_ref, qdec_ref, kdec_ref, cdec_ref):
    c = RET_CHUNK

    @pl.when(pl.program_id(1) == 0)
    def _init():
        state_ref[...] = jnp.zeros_like(state_ref)
        ii = lax.broadcasted_iota(jnp.int32, (c, c), 0)
        jj = lax.broadcasted_iota(jnp.int32, (c, c), 1)
        diff = (ii - jj).astype(F32)
        idx = lax.broadcasted_iota(jnp.int32, (c, RET_DK), 0).astype(F32)
        for h in range(RET_HEADS):
            def log_gamma(shape):
                return jnp.log(1.0 - jnp.exp2(-5.0 - jnp.full(shape, float(h), F32)))

            dmask_ref[h] = jnp.where(
                diff >= 0.0, jnp.exp(log_gamma((c, c)) * jnp.maximum(diff, 0.0)), 0.0)
            lgl = log_gamma((c, RET_DK))
            qdec_ref[h] = jnp.exp(lgl * (idx + 1.0))
            kdec_ref[h] = jnp.exp(lgl * (c - 1.0 - idx))
            cdec_ref[h] = jnp.exp(log_gamma((8, LANES)) * float(c))

    for h in range(RET_HEADS):
        ksl = slice(h * RET_DK, (h + 1) * RET_DK)
        vsl = slice(h * RET_DV, (h + 1) * RET_DV)
        qb = q_ref[:, ksl]
        kb = k_ref[:, ksl]
        v = v_ref[:, vsl]
        qd = (qb.astype(F32) * qdec_ref[h]).astype(BF16)
        kd = (kb.astype(F32) * kdec_ref[h]).astype(BF16)

        s = lax.dot_general(qb, kb, (((1,), (1,)), ((), ())), preferred_element_type=F32)
        s = s * dmask_ref[h]
        inner = jnp.dot(s.astype(BF16), v, preferred_element_type=F32)
        state = state_ref[h]
        cross = jnp.dot(qd, state.astype(BF16), preferred_element_type=F32)
        kv = lax.dot_general(kd, v, (((0,), (0,)), ((), ())), preferred_element_type=F32)
        state_ref[h] = state * cdec_ref[h, 0:1, 0:1] + kv

        y = inner + cross
        mu = jnp.mean(y, axis=-1, keepdims=True)
        d = y - mu
        var = jnp.mean(d * d, axis=-1, keepdims=True)
        yn = d * lax.rsqrt(var + GN_EPS) * gng_ref[:, vsl] + gnb_ref[:, vsl]
        gate = ga_ref[:, vsl].astype(F32) * gs_ref[:, vsl].astype(F32)
        o_ref[:, vsl] = (gate * yn).astype(o_ref.dtype)


def _retention(rqk, rv, rgs, gates, gn_g, gn_b, bsz, seq):
    t = bsz * seq
    c = RET_CHUNK
    nc = seq // c
    nh = RET_HEADS

    def rows(width, col):
        return pl.BlockSpec((c, width), lambda b, n: (b * nc + n, col))

    gn = pl.BlockSpec((1, D_MODEL), lambda b, n: (0, 0))
    return pl.pallas_call(
        _retention_kernel,
        grid=(bsz, nc),
        in_specs=[rows(nh * RET_DK, 0), rows(nh * RET_DK, 1), rows(D_MODEL, 0), rows(D_MODEL, 0),
                  rows(D_MODEL, 0), gn, gn],
        out_specs=rows(D_MODEL, 0),
        out_shape=jax.ShapeDtypeStruct((t, D_MODEL), BF16),
        scratch_shapes=[pltpu.VMEM((nh, RET_DK, RET_DV), F32), pltpu.VMEM((nh, c, c), F32),
                        pltpu.VMEM((nh, c, RET_DK), F32), pltpu.VMEM((nh, c, RET_DK), F32),
                        pltpu.VMEM((nh, 8, LANES), F32)],
        compiler_params=_params(("parallel", "arbitrary"), 32),
        name="retention",
    )(rqk, rqk, rv, rgs, gates, gn_g.reshape(1, -1), gn_b.reshape(1, -1))


SINK_LANES = 8


def _pair_blocks(prev, cur, g):
    c0 = (g // 2) * LANES
    kk = jnp.concatenate([prev[:, c0:c0 + LANES], cur[:, c0:c0 + LANES]], axis=0).astype(F32)
    lane = lax.broadcasted_iota(jnp.int32, kk.shape, 1)
    row = lax.broadcasted_iota(jnp.int32, kk.shape, 0)
    hd = ATT_HEAD_DIM
    own = jnp.where(((lane >= hd) if g % 2 else (lane < hd)) & (row != 0), kk, 0.0)
    other = pltpu.roll(own, hd, 1)
    lo, hi = (other, own) if g % 2 else (own, other)
    return lo.astype(BF16), hi.astype(BF16)


def _attention_consts(sink_ref, qaug_ref, kaug_ref, vones_ref):
    nk = 2 * ATT_BLOCK
    row_q = lax.broadcasted_iota(jnp.int32, qaug_ref.shape, 0)
    lane_q = lax.broadcasted_iota(jnp.int32, qaug_ref.shape, 1)
    pair_of_lane = (lane_q & (SINK_LANES - 1)) >> 1
    onehot = (lane_q < 2 * SINK_LANES) & (pair_of_lane == row_q // ATT_BLOCK)
    qaug_ref[...] = jnp.where(onehot, 1.0, 0.0).astype(BF16)
    row_k = lax.broadcasted_iota(jnp.int32, vones_ref.shape, 0)
    lane_k = lax.broadcasted_iota(jnp.int32, vones_ref.shape, 1)
    vones_ref[...] = jnp.where((row_k < nk) == (lane_k < ATT_HEAD_DIM), 1.0, 0.0).astype(BF16)
    for g in range(ATT_KV_HEADS):
        vals = jnp.zeros(vones_ref.shape, F32)
        for j in range(ATT_GROUP):
            sk = sink_ref[g * ATT_GROUP + j] * LOG2E
            vals = jnp.where((lane_k == j) & (row_k == (j % 2) * nk), sk, vals)
        hi = vals.astype(BF16).astype(F32)
        kaug_ref[g] = (hi + pltpu.roll(vals - hi, SINK_LANES, 1)).astype(BF16)


def _attention_kernel(last, sink_ref, q_ref, kp_ref, kc_ref, vp_ref, vc_ref, gb_ref, ret_ref, o_ref,
                      qaug_ref, kaug_ref, vones_ref, *p_refs):
    i = pl.program_id(1)
    bq = ATT_BLOCK
    nk = 2 * bq
    pairs = ATT_GROUP // 2

    @pl.when((pl.program_id(0) == 0) & (i == 0))
    def _init():
        _attention_consts(sink_ref, qaug_ref, kaug_ref, vones_ref)

    def scores(slot):
        kp, kc = kp_ref[...], kc_ref[...]
        qi = lax.broadcasted_iota(jnp.int32, (bq, nk), 0)
        kj = lax.broadcasted_iota(jnp.int32, (bq, nk), 1)
        dist = qi + bq - kj
        kmin = jnp.where(i > 0, 0, bq)
        valid = ((dist >= 0) & (dist < WINDOW) & (kj >= kmin)) | (kj == 0)
        bias = jnp.where(valid, 0.0, -jnp.inf)
        qaug = qaug_ref[...]
        for g in range(ATT_KV_HEADS):
            k_lo, k_hi = _pair_blocks(kp, kc, g)
            kfull = jnp.concatenate([jnp.concatenate([k_lo, k_hi], axis=0), kaug_ref[g]], axis=1)
            col0 = g * ATT_GROUP * ATT_HEAD_DIM
            qp = jnp.concatenate(
                [q_ref[:, col0 + p * LANES:col0 + (p + 1) * LANES] for p in range(pairs)], axis=0)
            qfull = jnp.concatenate([qp, qaug], axis=1)
            s = lax.dot_general(qfull, kfull, (((1,), (1,)), ((), ())),
                                preferred_element_type=F32)
            for p in range(pairs):
                for e in range(2):
                    sj = s[p * bq:(p + 1) * bq, e * nk:(e + 1) * nk] + bias
                    m = jnp.max(sj, axis=-1, keepdims=True)
                    p_refs[slot][g, p * bq:(p + 1) * bq, e * nk:(e + 1) * nk] = (
                        jnp.exp2(sj - m).astype(BF16))

    def values(slot):
        vp, vc = vp_ref[...], vc_ref[...]
        vones = vones_ref[...]
        for g in range(ATT_KV_HEADS):
            v_lo, v_hi = _pair_blocks(vp, vc, g)
            vfull = jnp.concatenate([jnp.concatenate([v_lo, v_hi], axis=0), vones], axis=1)
            o = jnp.dot(p_refs[slot][g], vfull, preferred_element_type=F32)
            col0 = g * ATT_GROUP * ATT_HEAD_DIM
            for p in range(pairs):
                num = o[p * bq:(p + 1) * bq, :LANES]
                den = o[p * bq:(p + 1) * bq, LANES:]
                c1 = col0 + p * LANES
                gate = gb_ref[:, c1:c1 + LANES].astype(F32)
                ret = ret_ref[:, c1:c1 + LANES].astype(F32)
                o_ref[:, c1:c1 + LANES] = (gate * num / den + ret).astype(o_ref.dtype)

    @pl.when(i == 0)
    def _first():
        scores(0)

    for par in (0, 1):
        @pl.when((i > 0) & (i < last) & (lax.rem(i, 2) == par))
        def _steady():
            values(1 - par)
            scores(par)

    @pl.when(i == last)
    def _last():
        values((last - 1) % 2)


def _attention(aq, akv, gates, ret, sinks, bsz, seq):
    t = bsz * seq
    bq = ATT_BLOCK
    nb = seq // bq
    rows = (ATT_GROUP // 2) * bq

    def spec(width, col, lag, back):
        def index(b, i):
            blk = jnp.clip(i - lag, 0, nb - 1)
            return (b * nb + jnp.maximum(blk - back, 0), col)
        return pl.BlockSpec((bq, width), index)

    return pl.pallas_call(
        functools.partial(_attention_kernel, nb),
        grid=(bsz, nb + 1),
        in_specs=[pl.BlockSpec(memory_space=pltpu.SMEM), spec(D_MODEL, 0, 0, 0),
                  spec(KV_W, 0, 0, 1), spec(KV_W, 0, 0, 0),
                  spec(KV_W, 1, 1, 1), spec(KV_W, 1, 1, 0), spec(D_MODEL, 1, 1, 0),
                  spec(D_MODEL, 0, 1, 0)],
        out_specs=spec(D_MODEL, 0, 1, 0),
        out_shape=jax.ShapeDtypeStruct((t, D_MODEL), BF16),
        scratch_shapes=[pltpu.VMEM((rows, LANES), BF16),
                        pltpu.VMEM((ATT_KV_HEADS, 2 * 2 * bq, LANES), BF16),
                        pltpu.VMEM((2 * 2 * bq, LANES), BF16),
                        pltpu.VMEM((ATT_KV_HEADS, rows, 2 * 2 * bq), BF16),
                        pltpu.VMEM((ATT_KV_HEADS, rows, 2 * 2 * bq), BF16)],
        compiler_params=_params(("arbitrary", "arbitrary"), 32),
        name="attention",
    )(sinks, aq, akv, akv, akv, akv, gates, ret)


OUT_PROJ_ROW_CHUNKS = 2


def _out_proj_kernel(alpha, m_ref, x_ref, w_ref, g_ref, b_ref, h_ref, hb_ref):
    rows = h_ref.shape[0] // OUT_PROJ_ROW_CHUNKS
    for c in range(OUT_PROJ_ROW_CHUNKS):
        rs = slice(c * rows, (c + 1) * rows)
        y = jnp.dot(m_ref[rs, :], w_ref[...], preferred_element_type=F32)
        h = _layer_norm(alpha * x_ref[rs, :] + y, g_ref[...], b_ref[...])
        h_ref[rs, :] = h
        hb_ref[rs, :] = h.astype(BF16)


def _out_proj(merged, x2, wb, ln_g, ln_b, alpha):
    t, d = x2.shape
    tm = 512
    row = pl.BlockSpec((tm, d), lambda i: (i, 0))
    vec = pl.BlockSpec((1, d), lambda i: (0, 0))
    return pl.pallas_call(
        functools.partial(_out_proj_kernel, alpha),
        grid=(t // tm,),
        in_specs=[row, row, pl.BlockSpec((d, d), lambda i: (0, 0)), vec, vec],
        out_specs=[row, row],
        out_shape=[jax.ShapeDtypeStruct((t, d), F32), jax.ShapeDtypeStruct((t, d), BF16)],
        compiler_params=_params(("parallel",), 48),
        name="out_proj_ln",
    )(merged, x2, wb, ln_g.reshape(1, d), ln_b.reshape(1, d))


def _ffn_kernel(alpha, hb_ref, h_hbm, wg_ref, wu_ref, wd_ref, g_ref, b_ref, o_ref, hres_ref, sem):
    i = pl.program_id(0)
    f = pl.program_id(1)
    tm = o_ref.shape[0]
    res_copy = pltpu.make_async_copy(h_hbm.at[pl.ds(i * tm, tm), :], hres_ref, sem)

    @pl.when(f == 0)
    def _init():
        res_copy.start()
        o_ref[...] = jnp.zeros_like(o_ref)

    hb = hb_ref[...]
    gt = jnp.dot(hb, wg_ref[...].astype(BF16), preferred_element_type=F32)
    ut = jnp.dot(hb, wu_ref[...].astype(BF16), preferred_element_type=F32)
    a = (gt * _sigmoid(gt) * ut).astype(BF16)
    o_ref[...] += jnp.dot(a, wd_ref[...].astype(BF16), preferred_element_type=F32)

    @pl.when(f == pl.num_programs(1) - 1)
    def _finish():
        res_copy.wait()
        o_ref[...] = _layer_norm(alpha * hres_ref[...] + o_ref[...], g_ref[...], b_ref[...])


def _ffn(h1, h1b, wg, wu, wd, ln_g, ln_b, alpha):
    t, d = h1.shape
    dff = wg.shape[1]
    tm, tf = 1024, 256
    row = pl.BlockSpec((tm, d), lambda i, f: (i, 0))
    vec = pl.BlockSpec((1, d), lambda i, f: (0, 0))
    return pl.pallas_call(
        functools.partial(_ffn_kernel, alpha),
        grid=(t // tm, dff // tf),
        in_specs=[row,
                  pl.BlockSpec(memory_space=pl.ANY),
                  pl.BlockSpec((d, tf), lambda i, f: (0, f)),
                  pl.BlockSpec((d, tf), lambda i, f: (0, f)),
                  pl.BlockSpec((tf, d), lambda i, f: (f, 0)),
                  vec, vec],
        out_specs=row,
        out_shape=jax.ShapeDtypeStruct((t, d), F32),
        scratch_shapes=[pltpu.VMEM((tm, d), F32), pltpu.SemaphoreType.DMA(())],
        compiler_params=_params(("parallel", "arbitrary"), 56),
        name="ffn_ln",
    )(h1b, h1, wg, wu, wd, ln_g.reshape(1, d), ln_b.reshape(1, d))


def kernel(x, positions, w_in, ret_gn_g, ret_gn_b, att_sinks, w_out, ln1_g, ln1_b,
           w_gate, w_up, w_down, ln2_g, ln2_b):
    bsz, seq, d = x.shape
    depth = w_in.shape[0]
    alpha = (2.0 * depth) ** 0.25
    h = x.reshape(bsz * seq, d)
    cos_r, sin_r, ca, s1, s2, hb = _prep(positions, h)
    for l in range(depth):
        if l > 0:
            hb = h.astype(BF16)
        w = w_in[l]
        rqk = _proj_segment(hb, w, OFF_RQ, OFF_RV - OFF_RQ, 1024, _ep_ret_rotary,
                            (cos_r, sin_r), "in_proj_ret_qk")
        rv = _proj_segment(hb, w, OFF_RV, D_MODEL, 1024, _ep_plain, (), "in_proj_ret_v")
        rgs = _proj_segment(hb, w, OFF_RG, D_MODEL, 1024, _ep_swish, (), "in_proj_ret_g")
        aq = _proj_segment(hb, w, OFF_AQ, D_MODEL, 1024, _ep_att_q, (ca, s1, s2), "in_proj_att_q")
        akv = _proj_segment(hb, w, OFF_AK, 2 * KV_W, 2 * KV_W, _ep_att_kv, (ca, s1, s2),
                            "in_proj_att_kv")
        gates = _proj_segment(hb, w, OFF_GA, 2 * D_MODEL, 1024, _ep_sigmoid, (), "in_proj_gates")
        ret = _retention(rqk, rv, rgs, gates, ret_gn_g[l], ret_gn_b[l], bsz, seq)
        merged = _attention(aq, akv, gates, ret, att_sinks[l], bsz, seq)
        h, hb = _out_proj(merged, h, w_out[l].astype(BF16), ln1_g[l], ln1_b[l], alpha)
        h = _ffn(h, hb, w_gate[l], w_up[l], w_down[l], ln2_g[l], ln2_b[l], alpha)
    return h.reshape(bsz, seq, d)
```

```python
import functools
import math

import jax
import jax.numpy as jnp
from jax import lax
from jax.experimental import pallas as pl
from jax.experimental.pallas import tpu as pltpu

F32 = jnp.float32
BF16 = jnp.bfloat16

D_MODEL = 2048
RET_HEADS = 4
RET_DV = D_MODEL // RET_HEADS
RET_DK = RET_DV // 2
RET_CHUNK = 256
RET_THETA = 10000.0
ATT_HEAD_DIM = 64
ATT_HEADS = D_MODEL // ATT_HEAD_DIM
ATT_KV_HEADS = ATT_HEADS // 8
ATT_GROUP = ATT_HEADS // ATT_KV_HEADS
WINDOW = 128
ATT_BLOCK = WINDOW
ROPE_THETA = 500000.0
ROPE_DIM = ATT_HEAD_DIM // 4
LN_EPS = 1e-5
GN_EPS = 1e-5

LANES = 128
LOG2E = math.log2(math.e)

OFF_RQ = 0
OFF_RK = OFF_RQ + RET_HEADS * RET_DK
OFF_RV = OFF_RK + RET_HEADS * RET_DK
OFF_RG = OFF_RV + RET_HEADS * RET_DV
OFF_AQ = OFF_RG + RET_HEADS * RET_DV
OFF_AK = OFF_AQ + ATT_HEADS * ATT_HEAD_DIM
OFF_AV = OFF_AK + ATT_KV_HEADS * ATT_HEAD_DIM
OFF_GA = OFF_AV + ATT_KV_HEADS * ATT_HEAD_DIM
OFF_GB = OFF_GA + D_MODEL
KV_W = ATT_KV_HEADS * ATT_HEAD_DIM

MIB = 1024 * 1024


def _params(sem, vmem_mib, flags=None):
    return pltpu.CompilerParams(dimension_semantics=sem, vmem_limit_bytes=vmem_mib * MIB,
                                flags=flags)


def _sigmoid(v):
    return 1.0 / (1.0 + jnp.exp(-v))


def _layer_norm(z, g, b):
    mu = jnp.mean(z, axis=-1, keepdims=True)
    d = z - mu
    var = jnp.mean(d * d, axis=-1, keepdims=True)
    return d * lax.rsqrt(var + LN_EPS) * g + b


def _prep_kernel(pos_ref, fr_ref, fa_ref, x_ref, cr_ref, sr_ref, ca_ref, s1_ref, s2_ref, xb_ref):
    xb_ref[...] = x_ref[...].astype(BF16)
    pos = pos_ref[...].astype(F32)
    ang_r = pos * fr_ref[...]
    cr_ref[...] = jnp.cos(ang_r)
    sr_ref[...] = jnp.sin(ang_r)
    ang_a = pos * fa_ref[...]
    sa = jnp.sin(ang_a)
    d = lax.broadcasted_iota(jnp.int32, ang_a.shape, 1) & (ATT_HEAD_DIM - 1)
    half = ROPE_DIM // 2
    ca_ref[...] = jnp.cos(ang_a)
    s1_ref[...] = jnp.where(d < half, -sa, 0.0)
    s2_ref[...] = jnp.where((d >= half) & (d < ROPE_DIM), sa, 0.0)


def _prep(positions, x2):
    t, dm = x2.shape
    tm = 1024
    pos_b = jnp.broadcast_to(positions.reshape(t, 1), (t, LANES))
    inv_r = 1.0 / (RET_THETA ** jnp.linspace(0.0, 1.0, RET_DK // 2, dtype=F32))
    inv_a = 1.0 / (ROPE_THETA ** (jnp.arange(0, ROPE_DIM, 2, dtype=F32) / ROPE_DIM))
    d = jnp.arange(LANES) % ATT_HEAD_DIM
    fa = jnp.where(d < ROPE_DIM, inv_a[d % (ROPE_DIM // 2)], 0.0).astype(F32)
    row = pl.BlockSpec((tm, LANES), lambda i: (i, 0))
    xrow = pl.BlockSpec((tm, dm), lambda i: (i, 0))
    vec = pl.BlockSpec((1, LANES), lambda i: (0, 0))
    tab = jax.ShapeDtypeStruct((t, LANES), F32)
    return pl.pallas_call(
        _prep_kernel,
        grid=(t // tm,),
        in_specs=[row, vec, vec, xrow],
        out_specs=[row] * 5 + [xrow],
        out_shape=[tab] * 5 + [jax.ShapeDtypeStruct((t, dm), BF16)],
        compiler_params=_params(("parallel",), 40),
        name="rotary_tables_xcast",
    )(pos_b, inv_r.reshape(1, LANES), fa.reshape(1, LANES), x2)


def _ep_plain(acc, j):
    return acc


def _ep_swish(acc, j):
    return acc * _sigmoid(acc)


def _ep_sigmoid(acc, j):
    return _sigmoid(acc)


def _ep_ret_rotary(acc, j, cos_ref, sin_ref):
    scale = jnp.where(j == 0, RET_DK ** -0.5, 1.0)
    cos = cos_ref[...] * scale
    sin = sin_ref[...] * scale
    half = RET_DK // 2
    outs = []
    for c0 in range(0, acc.shape[1], RET_DK):
        a1 = acc[:, c0:c0 + half]
        a2 = acc[:, c0 + half:c0 + RET_DK]
        outs += [a1 * cos - a2 * sin, a2 * cos + a1 * sin]
    return jnp.concatenate(outs, axis=-1)


def _rope_lanes(a, ca, s1, s2):
    outs = []
    for c0 in range(0, a.shape[1], LANES):
        ac = a[:, c0:c0 + LANES]
        up = pltpu.roll(ac, LANES - ROPE_DIM // 2, 1)
        dn = pltpu.roll(ac, ROPE_DIM // 2, 1)
        outs.append(ac * ca + up * s1 + dn * s2)
    return jnp.concatenate(outs, axis=-1)


def _ep_att_q(acc, j, ca_ref, s1_ref, s2_ref):
    c = (ATT_HEAD_DIM ** -0.5) * LOG2E
    return _rope_lanes(acc, ca_ref[...] * c, s1_ref[...] * c, s2_ref[...] * c)


def _ep_att_kv(acc, j, ca_ref, s1_ref, s2_ref):
    k = _rope_lanes(acc[:, :KV_W], ca_ref[...], s1_ref[...], s2_ref[...])
    return jnp.concatenate([k, acc[:, KV_W:]], axis=-1)


def _proj_kernel(epilogue, x_ref, w_ref, *rest):
    *tab_refs, o_ref, wb_ref = rest

    @pl.when(pl.program_id(1) == 0)
    def _cast_weights():
        wb_ref[...] = w_ref[...].astype(BF16)

    acc = jnp.dot(x_ref[...], wb_ref[...], preferred_element_type=F32)
    o_ref[...] = epilogue(acc, pl.program_id(0), *tab_refs).astype(o_ref.dtype)


def _proj_segment(xb, w, col0, width, tn, epilogue, tables=(), name="in_proj"):
    t, k = xb.shape
    tm = 1024
    assert col0 % LANES == 0 and width % tn == 0
    tab = pl.BlockSpec((tm, LANES), lambda j, i: (i, 0))
    return pl.pallas_call(
        functools.partial(_proj_kernel, epilogue),
        grid=(width // tn, t // tm),
        in_specs=[pl.BlockSpec((tm, k), lambda j, i: (i, 0)),
                  pl.BlockSpec((pl.Element(k), pl.Element(tn)),
                               lambda j, i: (0, pl.multiple_of(col0 + j * tn, LANES)))]
                 + [tab] * len(tables),
        out_specs=pl.BlockSpec((tm, tn), lambda j, i: (i, j)),
        out_shape=jax.ShapeDtypeStruct((t, width), BF16),
        scratch_shapes=[pltpu.VMEM((k, tn), BF16)],
        compiler_params=_params(("parallel", "arbitrary"), 56),
        name=name,
    )(xb, w, *tables)


def _retention_kernel(q_ref, k_ref, v_ref, gs_ref, ga_ref, gng_ref, gnb_ref, w_ref,
                      o_ref, wb_ref, state_ref, dmask_ref, qdec_ref, kdec_ref, cdec_ref):
    c = RET_CHUNK
    wb_ref[...] = w_ref[...].astype(BF16)

    @pl.when(pl.program_id(1) == 0)
    def _init():
        state_ref[...] = jnp.zeros_like(state_ref)
        ii = lax.broadcasted_iota(jnp.int32, (c, c), 0)
        jj = lax.broadcasted_iota(jnp.int32, (c, c), 1)
        diff = (ii - jj).astype(F32)
        idx = lax.broadcasted_iota(jnp.int32, (c, RET_DK), 0).astype(F32)
        for h in range(RET_HEADS):
            def log_gamma(shape):
                return jnp.log(1.0 - jnp.exp2(-5.0 - jnp.full(shape, float(h), F32)))

            dmask_ref[h] = jnp.where(
                diff >= 0.0, jnp.exp(log_gamma((c, c)) * jnp.maximum(diff, 0.0)), 0.0)
            lgl = log_gamma((c, RET_DK))
            qdec_ref[h] = jnp.exp(lgl * (idx + 1.0))
            kdec_ref[h] = jnp.exp(lgl * (c - 1.0 - idx))
            cdec_ref[h] = jnp.exp(log_gamma((8, LANES)) * float(c))

    for h in range(RET_HEADS):
        ksl = slice(h * RET_DK, (h + 1) * RET_DK)
        vsl = slice(h * RET_DV, (h + 1) * RET_DV)
        qb = q_ref[:, ksl]
        kb = k_ref[:, ksl]
        v = v_ref[:, vsl]
        qd = (qb.astype(F32) * qdec_ref[h]).astype(BF16)
        kd = (kb.astype(F32) * kdec_ref[h]).astype(BF16)

        s = lax.dot_general(qb, kb, (((1,), (1,)), ((), ())), preferred_element_type=F32)
        s = s * dmask_ref[h]
        inner = jnp.dot(s.astype(BF16), v, preferred_element_type=F32)
        state = state_ref[h]
        cross = jnp.dot(qd, state.astype(BF16), preferred_element_type=F32)
        kv = lax.dot_general(kd, v, (((0,), (0,)), ((), ())), preferred_element_type=F32)
        state_ref[h] = state * cdec_ref[h, 0:1, 0:1] + kv

        y = inner + cross
        mu = jnp.mean(y, axis=-1, keepdims=True)
        d = y - mu
        var = jnp.mean(d * d, axis=-1, keepdims=True)
        yn = d * lax.rsqrt(var + GN_EPS) * gng_ref[:, vsl] + gnb_ref[:, vsl]
        gate = ga_ref[:, vsl].astype(F32) * gs_ref[:, vsl].astype(F32)
        o_ref[:, vsl] = (gate * yn).astype(o_ref.dtype)


def _ride_specs(w, n_slabs, first_step, step_of):
    rows = w.shape[0] // n_slabs
    assert rows * n_slabs == w.shape[0] and rows % 16 == 0
    spec = pl.BlockSpec((rows, w.shape[1]),
                        lambda *ids: (jnp.clip(step_of(*ids) - first_step, 0, n_slabs - 1), 0))
    return spec, jax.ShapeDtypeStruct(w.shape, BF16)


def _retention(rqk, rv, rgs, gates, gn_g, gn_b, w_ride, bsz, seq):
    t = bsz * seq
    c = RET_CHUNK
    nc = seq // c
    nh = RET_HEADS
    ride_spec, ride_shape = _ride_specs(w_ride, bsz * nc, 0, lambda b, n: b * nc + n)

    def rows(width, col):
        return pl.BlockSpec((c, width), lambda b, n: (b * nc + n, col))

    gn = pl.BlockSpec((1, D_MODEL), lambda b, n: (0, 0))
    return pl.pallas_call(
        _retention_kernel,
        grid=(bsz, nc),
        in_specs=[rows(nh * RET_DK, 0), rows(nh * RET_DK, 1), rows(D_MODEL, 0), rows(D_MODEL, 0),
                  rows(D_MODEL, 0), gn, gn, ride_spec],
        out_specs=[rows(D_MODEL, 0), ride_spec],
        out_shape=[jax.ShapeDtypeStruct((t, D_MODEL), BF16), ride_shape],
        scratch_shapes=[pltpu.VMEM((nh, RET_DK, RET_DV), F32), pltpu.VMEM((nh, c, c), F32),
                        pltpu.VMEM((nh, c, RET_DK), F32), pltpu.VMEM((nh, c, RET_DK), F32),
                        pltpu.VMEM((nh, 8, LANES), F32)],
        compiler_params=_params(("arbitrary", "arbitrary"), 32),
        name="retention",
    )(rqk, rqk, rv, rgs, gates, gn_g.reshape(1, -1), gn_b.reshape(1, -1), w_ride)


SINK_LANES = 8


def _pair_blocks(prev, cur, g):
    c0 = (g // 2) * LANES
    kk = jnp.concatenate([prev[:, c0:c0 + LANES], cur[:, c0:c0 + LANES]], axis=0).astype(F32)
    lane = lax.broadcasted_iota(jnp.int32, kk.shape, 1)
    row = lax.broadcasted_iota(jnp.int32, kk.shape, 0)
    hd = ATT_HEAD_DIM
    own = jnp.where(((lane >= hd) if g % 2 else (lane < hd)) & (row != 0), kk, 0.0)
    other = pltpu.roll(own, hd, 1)
    lo, hi = (other, own) if g % 2 else (own, other)
    return lo.astype(BF16), hi.astype(BF16)


def _attention_consts(sink_ref, qaug_ref, kaug_ref, vones_ref):
    nk = 2 * ATT_BLOCK
    row_q = lax.broadcasted_iota(jnp.int32, qaug_ref.shape, 0)
    lane_q = lax.broadcasted_iota(jnp.int32, qaug_ref.shape, 1)
    pair_of_lane = (lane_q & (SINK_LANES - 1)) >> 1
    onehot = (lane_q < 2 * SINK_LANES) & (pair_of_lane == row_q // ATT_BLOCK)
    qaug_ref[...] = jnp.where(onehot, 1.0, 0.0).astype(BF16)
    row_k = lax.broadcasted_iota(jnp.int32, vones_ref.shape, 0)
    lane_k = lax.broadcasted_iota(jnp.int32, vones_ref.shape, 1)
    vones_ref[...] = jnp.where((row_k < nk) == (lane_k < ATT_HEAD_DIM), 1.0, 0.0).astype(BF16)
    for g in range(ATT_KV_HEADS):
        vals = jnp.zeros(vones_ref.shape, F32)
        for j in range(ATT_GROUP):
            sk = sink_ref[g * ATT_GROUP + j] * LOG2E
            vals = jnp.where((lane_k == j) & (row_k == (j % 2) * nk), sk, vals)
        hi = vals.astype(BF16).astype(F32)
        kaug_ref[g] = (hi + pltpu.roll(vals - hi, SINK_LANES, 1)).astype(BF16)


def _attention_kernel(last, sink_ref, q_ref, kp_ref, kc_ref, vp_ref, vc_ref, gb_ref, ret_ref,
                      w1_ref, w2_ref, o_ref, w1b_ref, w2b_ref,
                      qaug_ref, kaug_ref, vones_ref, *p_refs):
    i = pl.program_id(1)
    w1b_ref[...] = w1_ref[...].astype(BF16)
    w2b_ref[...] = w2_ref[...].astype(BF16)
    bq = ATT_BLOCK
    nk = 2 * bq
    pairs = ATT_GROUP // 2

    @pl.when((pl.program_id(0) == 0) & (i == 0))
    def _init():
        _attention_consts(sink_ref, qaug_ref, kaug_ref, vones_ref)

    def scores(slot):
        kp, kc = kp_ref[...], kc_ref[...]
        qi = lax.broadcasted_iota(jnp.int32, (bq, nk), 0)
        kj = lax.broadcasted_iota(jnp.int32, (bq, nk), 1)
        dist = qi + bq - kj
        kmin = jnp.where(i > 0, 0, bq)
        valid = ((dist >= 0) & (dist < WINDOW) & (kj >= kmin)) | (kj == 0)
        bias = jnp.where(valid, 0.0, -jnp.inf)
        qaug = qaug_ref[...]
        for g in range(ATT_KV_HEADS):
            k_lo, k_hi = _pair_blocks(kp, kc, g)
            kfull = jnp.concatenate([jnp.concatenate([k_lo, k_hi], axis=0), kaug_ref[g]], axis=1)
            col0 = g * ATT_GROUP * ATT_HEAD_DIM
            qp = jnp.concatenate(
                [q_ref[:, col0 + p * LANES:col0 + (p + 1) * LANES] for p in range(pairs)], axis=0)
            qfull = jnp.concatenate([qp, qaug], axis=1)
            s = lax.dot_general(qfull, kfull, (((1,), (1,)), ((), ())),
                                preferred_element_type=F32)
            for p in range(pairs):
                for e in range(2):
                    sj = s[p * bq:(p + 1) * bq, e * nk:(e + 1) * nk] + bias
                    m = jnp.max(sj, axis=-1, keepdims=True)
                    p_refs[slot][g, p * bq:(p + 1) * bq, e * nk:(e + 1) * nk] = (
                        jnp.exp2(sj - m).astype(BF16))

    def values(slot):
        vp, vc = vp_ref[...], vc_ref[...]
        vones = vones_ref[...]
        for g in range(ATT_KV_HEADS):
            v_lo, v_hi = _pair_blocks(vp, vc, g)
            vfull = jnp.concatenate([jnp.concatenate([v_lo, v_hi], axis=0), vones], axis=1)
            o = jnp.dot(p_refs[slot][g], vfull, preferred_element_type=F32)
            col0 = g * ATT_GROUP * ATT_HEAD_DIM
            for p in range(pairs):
                num = o[p * bq:(p + 1) * bq, :LANES]
                den = o[p * bq:(p + 1) * bq, LANES:]
                c1 = col0 + p * LANES
                gate = gb_ref[:, c1:c1 + LANES].astype(F32)
                ret = ret_ref[:, c1:c1 + LANES].astype(F32)
                o_ref[:, c1:c1 + LANES] = (gate * num / den + ret).astype(o_ref.dtype)

    @pl.when(i == 0)
    def _first():
        scores(0)

    for par in (0, 1):
        @pl.when((i > 0) & (i < last) & (lax.rem(i, 2) == par))
        def _steady():
            values(1 - par)
            scores(par)

    @pl.when(i == last)
    def _last():
        values((last - 1) % 2)


def _attention(aq, akv, gates, ret, sinks, w_ride1, w_ride2, bsz, seq):
    t = bsz * seq
    bq = ATT_BLOCK
    nb = seq // bq
    rows = (ATT_GROUP // 2) * bq
    n_slabs = 1 << ((bsz * (nb + 1) // 2).bit_length() - 1)
    flat = lambda b, i: b * (nb + 1) + i
    ride1, shape1 = _ride_specs(w_ride1, n_slabs, 0, flat)
    ride2, shape2 = _ride_specs(w_ride2, n_slabs, n_slabs, flat)

    def spec(width, col, lag, back):
        def index(b, i):
            blk = jnp.clip(i - lag, 0, nb - 1)
            return (b * nb + jnp.maximum(blk - back, 0), col)
        return pl.BlockSpec((bq, width), index)

    return pl.pallas_call(
        functools.partial(_attention_kernel, nb),
        grid=(bsz, nb + 1),
        in_specs=[pl.BlockSpec(memory_space=pltpu.SMEM), spec(D_MODEL, 0, 0, 0),
                  spec(KV_W, 0, 0, 1), spec(KV_W, 0, 0, 0),
                  spec(KV_W, 1, 1, 1), spec(KV_W, 1, 1, 0), spec(D_MODEL, 1, 1, 0),
                  spec(D_MODEL, 0, 1, 0), ride1, ride2],
        out_specs=[spec(D_MODEL, 0, 1, 0), ride1, ride2],
        out_shape=[jax.ShapeDtypeStruct((t, D_MODEL), BF16), shape1, shape2],
        scratch_shapes=[pltpu.VMEM((rows, LANES), BF16),
                        pltpu.VMEM((ATT_KV_HEADS, 2 * 2 * bq, LANES), BF16),
                        pltpu.VMEM((2 * 2 * bq, LANES), BF16),
                        pltpu.VMEM((ATT_KV_HEADS, rows, 2 * 2 * bq), BF16),
                        pltpu.VMEM((ATT_KV_HEADS, rows, 2 * 2 * bq), BF16)],
        compiler_params=_params(("arbitrary", "arbitrary"), 32),
        name="attention",
    )(sinks, aq, akv, akv, akv, akv, gates, ret, w_ride1, w_ride2)


OUT_PROJ_ROW_CHUNKS = 2


def _out_proj_kernel(alpha, m_ref, x_ref, w_ref, g_ref, b_ref, h_ref, hb_ref):
    rows = h_ref.shape[0] // OUT_PROJ_ROW_CHUNKS
    for c in range(OUT_PROJ_ROW_CHUNKS):
        rs = slice(c * rows, (c + 1) * rows)
        y = jnp.dot(m_ref[rs, :], w_ref[...], preferred_element_type=F32)
        h = _layer_norm(alpha * x_ref[rs, :] + y, g_ref[...], b_ref[...])
        h_ref[rs, :] = h
        hb_ref[rs, :] = h.astype(BF16)


def _out_proj(merged, x2, wb, ln_g, ln_b, alpha):
    t, d = x2.shape
    tm = 512
    row = pl.BlockSpec((tm, d), lambda i: (i, 0))
    vec = pl.BlockSpec((1, d), lambda i: (0, 0))
    return pl.pallas_call(
        functools.partial(_out_proj_kernel, alpha),
        grid=(t // tm,),
        in_specs=[row, row, pl.BlockSpec((d, d), lambda i: (0, 0)), vec, vec],
        out_specs=[row, row],
        out_shape=[jax.ShapeDtypeStruct((t, d), F32), jax.ShapeDtypeStruct((t, d), BF16)],
        compiler_params=_params(("parallel",), 48),
        name="out_proj_ln",
    )(merged, x2, wb, ln_g.reshape(1, d), ln_b.reshape(1, d))


def _ffn_kernel(alpha, hb_ref, h_hbm, wg_ref, wu_ref, wd_ref, g_ref, b_ref, o_ref, hres_ref, sem):
    i = pl.program_id(0)
    f = pl.program_id(1)
    tm = o_ref.shape[0]
    res_copy = pltpu.make_async_copy(h_hbm.at[pl.ds(i * tm, tm), :], hres_ref, sem)

    @pl.when(f == 0)
    def _init():
        res_copy.start()
        o_ref[...] = jnp.zeros_like(o_ref)

    hb = hb_ref[...]
    gt = jnp.dot(hb, wg_ref[...], preferred_element_type=F32)
    ut = jnp.dot(hb, wu_ref[...], preferred_element_type=F32)
    a = (gt * _sigmoid(gt) * ut).astype(BF16)
    o_ref[...] += jnp.dot(a, wd_ref[...], preferred_element_type=F32)

    @pl.when(f == pl.num_programs(1) - 1)
    def _finish():
        res_copy.wait()
        o_ref[...] = _layer_norm(alpha * hres_ref[...] + o_ref[...], g_ref[...], b_ref[...])


def _ffn(h1, h1b, wg, wu, wd, ln_g, ln_b, alpha):
    t, d = h1.shape
    dff = wg.shape[1]
    tm, tf = 1024, 512
    row = pl.BlockSpec((tm, d), lambda i, f: (i, 0))
    vec = pl.BlockSpec((1, d), lambda i, f: (0, 0))
    return pl.pallas_call(
        functools.partial(_ffn_kernel, alpha),
        grid=(t // tm, dff // tf),
        in_specs=[row,
                  pl.BlockSpec(memory_space=pl.ANY),
                  pl.BlockSpec((d, tf), lambda i, f: (0, f)),
                  pl.BlockSpec((d, tf), lambda i, f: (0, f)),
                  pl.BlockSpec((tf, d), lambda i, f: (f, 0)),
                  vec, vec],
        out_specs=row,
        out_shape=jax.ShapeDtypeStruct((t, d), F32),
        scratch_shapes=[pltpu.VMEM((tm, d), F32), pltpu.SemaphoreType.DMA(())],
        compiler_params=_params(("parallel", "arbitrary"), 60),
        name="ffn_ln",
    )(h1b, h1, wg, wu, wd, ln_g.reshape(1, d), ln_b.reshape(1, d))


def kernel(x, positions, w_in, ret_gn_g, ret_gn_b, att_sinks, w_out, ln1_g, ln1_b,
           w_gate, w_up, w_down, ln2_g, ln2_b):
    bsz, seq, d = x.shape
    depth = w_in.shape[0]
    alpha = (2.0 * depth) ** 0.25
    h = x.reshape(bsz * seq, d)
    cos_r, sin_r, ca, s1, s2, hb = _prep(positions, h)
    for l in range(depth):
        if l > 0:
            hb = h.astype(BF16)
        w = w_in[l]
        rqk = _proj_segment(hb, w, OFF_RQ, OFF_RV - OFF_RQ, 1024, _ep_ret_rotary,
                            (cos_r, sin_r), "in_proj_ret_qk")
        rv = _proj_segment(hb, w, OFF_RV, D_MODEL, 1024, _ep_plain, (), "in_proj_ret_v")
        rgs = _proj_segment(hb, w, OFF_RG, D_MODEL, 1024, _ep_swish, (), "in_proj_ret_g")
        aq = _proj_segment(hb, w, OFF_AQ, D_MODEL, 1024, _ep_att_q, (ca, s1, s2), "in_proj_att_q")
        akv = _proj_segment(hb, w, OFF_AK, 2 * KV_W, 2 * KV_W, _ep_att_kv, (ca, s1, s2),
                            "in_proj_att_kv")
        gates = _proj_segment(hb, w, OFF_GA, 2 * D_MODEL, 1024, _ep_sigmoid, (), "in_proj_gates")
        ret, wdb = _retention(rqk, rv, rgs, gates, ret_gn_g[l], ret_gn_b[l], w_down[l], bsz, seq)
        merged, wgb, wub = _attention(aq, akv, gates, ret, att_sinks[l], w_gate[l], w_up[l],
                                      bsz, seq)
        h, hb = _out_proj(merged, h, w_out[l].astype(BF16), ln1_g[l], ln1_b[l], alpha)
        h = _ffn(h, hb, wgb, wub, wdb, ln2_g[l], ln2_b[l], alpha)
    return h.reshape(bsz, seq, d)
```

```python
import functools
import math

import jax
import jax.numpy as jnp
from jax import lax
from jax.experimental import pallas as pl
from jax.experimental.pallas import tpu as pltpu

F32 = jnp.float32
BF16 = jnp.bfloat16

D_MODEL = 2048
RET_HEADS = 4
RET_DV = D_MODEL // RET_HEADS
RET_DK = RET_DV // 2
RET_CHUNK = 256
RET_THETA = 10000.0
ATT_HEAD_DIM = 64
ATT_HEADS = D_MODEL // ATT_HEAD_DIM
ATT_KV_HEADS = ATT_HEADS // 8
ATT_GROUP = ATT_HEADS // ATT_KV_HEADS
WINDOW = 128
ATT_BLOCK = WINDOW
ROPE_THETA = 500000.0
ROPE_DIM = ATT_HEAD_DIM // 4
LN_EPS = 1e-5
GN_EPS = 1e-5

LANES = 128
LOG2E = math.log2(math.e)

OFF_RQ = 0
OFF_RK = OFF_RQ + RET_HEADS * RET_DK
OFF_RV = OFF_RK + RET_HEADS * RET_DK
OFF_RG = OFF_RV + RET_HEADS * RET_DV
OFF_AQ = OFF_RG + RET_HEADS * RET_DV
OFF_AK = OFF_AQ + ATT_HEADS * ATT_HEAD_DIM
OFF_AV = OFF_AK + ATT_KV_HEADS * ATT_HEAD_DIM
OFF_GA = OFF_AV + ATT_KV_HEADS * ATT_HEAD_DIM
OFF_GB = OFF_GA + D_MODEL
KV_W = ATT_KV_HEADS * ATT_HEAD_DIM

MIB = 1024 * 1024


def _params(sem, vmem_mib, flags=None):
    return pltpu.CompilerParams(dimension_semantics=sem, vmem_limit_bytes=vmem_mib * MIB,
                                flags=flags)


def _sigmoid(v):
    return 1.0 / (1.0 + jnp.exp(-v))


def _layer_norm(z, g, b):
    mu = jnp.mean(z, axis=-1, keepdims=True)
    d = z - mu
    var = jnp.mean(d * d, axis=-1, keepdims=True)
    return d * lax.rsqrt(var + LN_EPS) * g + b


def _prep_kernel(pos_ref, fr_ref, fa_ref, x_ref, cr_ref, sr_ref, ca_ref, s1_ref, s2_ref, xb_ref):
    xb_ref[...] = x_ref[...].astype(BF16)
    pos = pos_ref[...].astype(F32)
    ang_r = pos * fr_ref[...]
    cr_ref[...] = jnp.cos(ang_r)
    sr_ref[...] = jnp.sin(ang_r)
    ang_a = pos * fa_ref[...]
    sa = jnp.sin(ang_a)
    d = lax.broadcasted_iota(jnp.int32, ang_a.shape, 1) & (ATT_HEAD_DIM - 1)
    half = ROPE_DIM // 2
    ca_ref[...] = jnp.cos(ang_a)
    s1_ref[...] = jnp.where(d < half, -sa, 0.0)
    s2_ref[...] = jnp.where((d >= half) & (d < ROPE_DIM), sa, 0.0)


def _prep(positions, x2):
    t, dm = x2.shape
    tm = 1024
    pos_b = jnp.broadcast_to(positions.reshape(t, 1), (t, LANES))
    inv_r = 1.0 / (RET_THETA ** jnp.linspace(0.0, 1.0, RET_DK // 2, dtype=F32))
    inv_a = 1.0 / (ROPE_THETA ** (jnp.arange(0, ROPE_DIM, 2, dtype=F32) / ROPE_DIM))
    d = jnp.arange(LANES) % ATT_HEAD_DIM
    fa = jnp.where(d < ROPE_DIM, inv_a[d % (ROPE_DIM // 2)], 0.0).astype(F32)
    row = pl.BlockSpec((tm, LANES), lambda i: (i, 0))
    xrow = pl.BlockSpec((tm, dm), lambda i: (i, 0))
    vec = pl.BlockSpec((1, LANES), lambda i: (0, 0))
    tab = jax.ShapeDtypeStruct((t, LANES), F32)
    return pl.pallas_call(
        _prep_kernel,
        grid=(t // tm,),
        in_specs=[row, vec, vec, xrow],
        out_specs=[row] * 5 + [xrow],
        out_shape=[tab] * 5 + [jax.ShapeDtypeStruct((t, dm), BF16)],
        compiler_params=_params(("parallel",), 40),
        name="rotary_tables_xcast",
    )(pos_b, inv_r.reshape(1, LANES), fa.reshape(1, LANES), x2)


def _ep_plain(acc, j):
    return acc


def _ep_swish(acc, j):
    return acc * _sigmoid(acc)


def _ep_sigmoid(acc, j):
    return _sigmoid(acc)


def _ep_ret_rotary(acc, j, cos_ref, sin_ref):
    scale = jnp.where(j == 0, RET_DK ** -0.5, 1.0)
    cos = cos_ref[...] * scale
    sin = sin_ref[...] * scale
    half = RET_DK // 2
    outs = []
    for c0 in range(0, acc.shape[1], RET_DK):
        a1 = acc[:, c0:c0 + half]
        a2 = acc[:, c0 + half:c0 + RET_DK]
        outs += [a1 * cos - a2 * sin, a2 * cos + a1 * sin]
    return jnp.concatenate(outs, axis=-1)


def _rope_lanes(a, ca, s1, s2):
    outs = []
    for c0 in range(0, a.shape[1], LANES):
        ac = a[:, c0:c0 + LANES]
        up = pltpu.roll(ac, LANES - ROPE_DIM // 2, 1)
        dn = pltpu.roll(ac, ROPE_DIM // 2, 1)
        outs.append(ac * ca + up * s1 + dn * s2)
    return jnp.concatenate(outs, axis=-1)


def _ep_att_q(acc, j, ca_ref, s1_ref, s2_ref):
    c = (ATT_HEAD_DIM ** -0.5) * LOG2E
    return _rope_lanes(acc, ca_ref[...] * c, s1_ref[...] * c, s2_ref[...] * c)


def _ep_att_kv(acc, j, ca_ref, s1_ref, s2_ref):
    k = _rope_lanes(acc[:, :KV_W], ca_ref[...], s1_ref[...], s2_ref[...])
    return jnp.concatenate([k, acc[:, KV_W:]], axis=-1)


def _proj_kernel(epilogue, n_tab, x_ref, w_ref, *rest):
    tab_refs, rest = rest[:n_tab], rest[n_tab:]
    ride_ref, o_ref, ride_out_ref, wb_ref = rest if len(rest) == 4 else (None, rest[0], None, rest[1])

    @pl.when(pl.program_id(1) == 0)
    def _cast_weights():
        wb_ref[...] = w_ref[...].astype(BF16)

    if ride_ref is not None:
        ride_out_ref[...] = ride_ref[...].astype(BF16)
    acc = jnp.dot(x_ref[...], wb_ref[...], preferred_element_type=F32)
    o_ref[...] = epilogue(acc, pl.program_id(0), *tab_refs).astype(o_ref.dtype)


def _proj_segment(xb, w, col0, width, tn, epilogue, tables=(), name="in_proj", ride=None):
    t, k = xb.shape
    tm = 1024
    nt = t // tm
    assert col0 % LANES == 0 and width % tn == 0
    tab = pl.BlockSpec((tm, LANES), lambda j, i: (i, 0))
    in_specs = [pl.BlockSpec((tm, k), lambda j, i: (i, 0)),
                pl.BlockSpec((pl.Element(k), pl.Element(tn)),
                             lambda j, i: (0, pl.multiple_of(col0 + j * tn, LANES)))]
    in_specs += [tab] * len(tables)
    out_specs = [pl.BlockSpec((tm, tn), lambda j, i: (i, j))]
    out_shape = [jax.ShapeDtypeStruct((t, width), BF16)]
    operands = [xb, w, *tables]
    if ride is not None:
        steps = (width // tn) * nt
        slab = ride.shape[0] // steps
        assert slab * steps == ride.shape[0] and slab % 16 == 0
        ride_spec = pl.BlockSpec((slab, ride.shape[1]), lambda j, i: (j * nt + i, 0))
        in_specs.append(ride_spec)
        out_specs.append(ride_spec)
        out_shape.append(jax.ShapeDtypeStruct(ride.shape, BF16))
        operands.append(ride)
    outs = pl.pallas_call(
        functools.partial(_proj_kernel, epilogue, len(tables)),
        grid=(width // tn, nt),
        in_specs=in_specs,
        out_specs=out_specs,
        out_shape=out_shape,
        scratch_shapes=[pltpu.VMEM((k, tn), BF16)],
        compiler_params=_params(("parallel", "arbitrary"), 56),
        name=name,
    )(*operands)
    return outs[0] if ride is None else outs


def _retention_kernel(q_ref, k_ref, v_ref, gs_ref, ga_ref, gng_ref, gnb_ref,
                      o_ref, state_ref, dmask_ref, qdec_ref, kdec_ref, cdec_ref):
    c = RET_CHUNK

    @pl.when(pl.program_id(1) == 0)
    def _init():
        state_ref[...] = jnp.zeros_like(state_ref)
        ii = lax.broadcasted_iota(jnp.int32, (c, c), 0)
        jj = lax.broadcasted_iota(jnp.int32, (c, c), 1)
        diff = (ii - jj).astype(F32)
        idx = lax.broadcasted_iota(jnp.int32, (c, RET_DK), 0).astype(F32)
        for h in range(RET_HEADS):
            def log_gamma(shape):
                return jnp.log(1.0 - jnp.exp2(-5.0 - jnp.full(shape, float(h), F32)))

            dmask_ref[h] = jnp.where(
                diff >= 0.0, jnp.exp(log_gamma((c, c)) * jnp.maximum(diff, 0.0)), 0.0)
            lgl = log_gamma((c, RET_DK))
            qdec_ref[h] = jnp.exp(lgl * (idx + 1.0))
            kdec_ref[h] = jnp.exp(lgl * (c - 1.0 - idx))
            cdec_ref[h] = jnp.exp(log_gamma((8, LANES)) * float(c))

    for h in range(RET_HEADS):
        ksl = slice(h * RET_DK, (h + 1) * RET_DK)
        vsl = slice(h * RET_DV, (h + 1) * RET_DV)
        qb = q_ref[:, ksl]
        kb = k_ref[:, ksl]
        v = v_ref[:, vsl]
        qd = (qb.astype(F32) * qdec_ref[h]).astype(BF16)
        kd = (kb.astype(F32) * kdec_ref[h]).astype(BF16)

        s = lax.dot_general(qb, kb, (((1,), (1,)), ((), ())), preferred_element_type=F32)
        s = s * dmask_ref[h]
        inner = jnp.dot(s.astype(BF16), v, preferred_element_type=F32)
        state = state_ref[h]
        cross = jnp.dot(qd, state.astype(BF16), preferred_element_type=F32)
        kv = lax.dot_general(kd, v, (((0,), (0,)), ((), ())), preferred_element_type=F32)
        state_ref[h] = state * cdec_ref[h, 0:1, 0:1] + kv

        y = inner + cross
        mu = jnp.mean(y, axis=-1, keepdims=True)
        d = y - mu
        var = jnp.mean(d * d, axis=-1, keepdims=True)
        yn = d * lax.rsqrt(var + GN_EPS) * gng_ref[:, vsl] + gnb_ref[:, vsl]
        gate = ga_ref[:, vsl].astype(F32) * gs_ref[:, vsl].astype(F32)
        o_ref[:, vsl] = (gate * yn).astype(o_ref.dtype)


def _retention(rqk, rv, rgs, gates, gn_g, gn_b, bsz, seq):
    t = bsz * seq
    c = RET_CHUNK
    nc = seq // c
    nh = RET_HEADS

    def rows(width, col):
        return pl.BlockSpec((c, width), lambda b, n: (b * nc + n, col))

    gn = pl.BlockSpec((1, D_MODEL), lambda b, n: (0, 0))
    return pl.pallas_call(
        _retention_kernel,
        grid=(bsz, nc),
        in_specs=[rows(nh * RET_DK, 0), rows(nh * RET_DK, 1), rows(D_MODEL, 0), rows(D_MODEL, 0),
                  rows(D_MODEL, 0), gn, gn],
        out_specs=rows(D_MODEL, 0),
        out_shape=jax.ShapeDtypeStruct((t, D_MODEL), BF16),
        scratch_shapes=[pltpu.VMEM((nh, RET_DK, RET_DV), F32), pltpu.VMEM((nh, c, c), F32),
                        pltpu.VMEM((nh, c, RET_DK), F32), pltpu.VMEM((nh, c, RET_DK), F32),
                        pltpu.VMEM((nh, 8, LANES), F32)],
        compiler_params=_params(("parallel", "arbitrary"), 32),
        name="retention",
    )(rqk, rqk, rv, rgs, gates, gn_g.reshape(1, -1), gn_b.reshape(1, -1))


SINK_LANES = 8


def _pair_blocks(prev, cur, g):
    c0 = (g // 2) * LANES
    kk = jnp.concatenate([prev[:, c0:c0 + LANES], cur[:, c0:c0 + LANES]], axis=0).astype(F32)
    lane = lax.broadcasted_iota(jnp.int32, kk.shape, 1)
    row = lax.broadcasted_iota(jnp.int32, kk.shape, 0)
    hd = ATT_HEAD_DIM
    own = jnp.where(((lane >= hd) if g % 2 else (lane < hd)) & (row != 0), kk, 0.0)
    other = pltpu.roll(own, hd, 1)
    lo, hi = (other, own) if g % 2 else (own, other)
    return lo.astype(BF16), hi.astype(BF16)


def _attention_consts(sink_ref, qaug_ref, kaug_ref, vones_ref):
    nk = 2 * ATT_BLOCK
    row_q = lax.broadcasted_iota(jnp.int32, qaug_ref.shape, 0)
    lane_q = lax.broadcasted_iota(jnp.int32, qaug_ref.shape, 1)
    pair_of_lane = (lane_q & (SINK_LANES - 1)) >> 1
    onehot = (lane_q < 2 * SINK_LANES) & (pair_of_lane == row_q // ATT_BLOCK)
    qaug_ref[...] = jnp.where(onehot, 1.0, 0.0).astype(BF16)
    row_k = lax.broadcasted_iota(jnp.int32, vones_ref.shape, 0)
    lane_k = lax.broadcasted_iota(jnp.int32, vones_ref.shape, 1)
    vones_ref[...] = jnp.where((row_k < nk) == (lane_k < ATT_HEAD_DIM), 1.0, 0.0).astype(BF16)
    for g in range(ATT_KV_HEADS):
        vals = jnp.zeros(vones_ref.shape, F32)
        for j in range(ATT_GROUP):
            sk = sink_ref[g * ATT_GROUP + j] * LOG2E
            vals = jnp.where((lane_k == j) & (row_k == (j % 2) * nk), sk, vals)
        hi = vals.astype(BF16).astype(F32)
        kaug_ref[g] = (hi + pltpu.roll(vals - hi, SINK_LANES, 1)).astype(BF16)


def _attention_kernel(last, sink_ref, q_ref, kp_ref, kc_ref, vp_ref, vc_ref, gb_ref, ret_ref, o_ref,
                      qaug_ref, kaug_ref, vones_ref, *p_refs):
    i = pl.program_id(1)
    bq = ATT_BLOCK
    nk = 2 * bq
    pairs = ATT_GROUP // 2

    @pl.when((pl.program_id(0) == 0) & (i == 0))
    def _init():
        _attention_consts(sink_ref, qaug_ref, kaug_ref, vones_ref)

    def scores(slot):
        kp, kc = kp_ref[...], kc_ref[...]
        qi = lax.broadcasted_iota(jnp.int32, (bq, nk), 0)
        kj = lax.broadcasted_iota(jnp.int32, (bq, nk), 1)
        dist = qi + bq - kj
        kmin = jnp.where(i > 0, 0, bq)
        valid = ((dist >= 0) & (dist < WINDOW) & (kj >= kmin)) | (kj == 0)
        bias = jnp.where(valid, 0.0, -jnp.inf)
        qaug = qaug_ref[...]
        for g in range(ATT_KV_HEADS):
            k_lo, k_hi = _pair_blocks(kp, kc, g)
            kfull = jnp.concatenate([jnp.concatenate([k_lo, k_hi], axis=0), kaug_ref[g]], axis=1)
            col0 = g * ATT_GROUP * ATT_HEAD_DIM
            qp = jnp.concatenate(
                [q_ref[:, col0 + p * LANES:col0 + (p + 1) * LANES] for p in range(pairs)], axis=0)
            qfull = jnp.concatenate([qp, qaug], axis=1)
            s = lax.dot_general(qfull, kfull, (((1,), (1,)), ((), ())),
                                preferred_element_type=F32)
            for p in range(pairs):
                for e in range(2):
                    sj = s[p * bq:(p + 1) * bq, e * nk:(e + 1) * nk] + bias
                    m = jnp.max(sj, axis=-1, keepdims=True)
                    p_refs[slot][g, p * bq:(p + 1) * bq, e * nk:(e + 1) * nk] = (
                        jnp.exp2(sj - m).astype(BF16))

    def values(slot):
        vp, vc = vp_ref[...], vc_ref[...]
        vones = vones_ref[...]
        for g in range(ATT_KV_HEADS):
            v_lo, v_hi = _pair_blocks(vp, vc, g)
            vfull = jnp.concatenate([jnp.concatenate([v_lo, v_hi], axis=0), vones], axis=1)
            o = jnp.dot(p_refs[slot][g], vfull, preferred_element_type=F32)
            col0 = g * ATT_GROUP * ATT_HEAD_DIM
            for p in range(pairs):
                num = o[p * bq:(p + 1) * bq, :LANES]
                den = o[p * bq:(p + 1) * bq, LANES:]
                c1 = col0 + p * LANES
                gate = gb_ref[:, c1:c1 + LANES].astype(F32)
                ret = ret_ref[:, c1:c1 + LANES].astype(F32)
                o_ref[:, c1:c1 + LANES] = (gate * num / den + ret).astype(o_ref.dtype)

    @pl.when(i == 0)
    def _first():
        scores(0)

    for par in (0, 1):
        @pl.when((i > 0) & (i < last) & (lax.rem(i, 2) == par))
        def _steady():
            values(1 - par)
            scores(par)

    @pl.when(i == last)
    def _last():
        values((last - 1) % 2)


def _attention(aq, akv, gates, ret, sinks, bsz, seq):
    t = bsz * seq
    bq = ATT_BLOCK
    nb = seq // bq
    rows = (ATT_GROUP // 2) * bq

    def spec(width, col, lag, back):
        def index(b, i):
            blk = jnp.clip(i - lag, 0, nb - 1)
            return (b * nb + jnp.maximum(blk - back, 0), col)
        return pl.BlockSpec((bq, width), index)

    return pl.pallas_call(
        functools.partial(_attention_kernel, nb),
        grid=(bsz, nb + 1),
        in_specs=[pl.BlockSpec(memory_space=pltpu.SMEM), spec(D_MODEL, 0, 0, 0),
                  spec(KV_W, 0, 0, 1), spec(KV_W, 0, 0, 0),
                  spec(KV_W, 1, 1, 1), spec(KV_W, 1, 1, 0), spec(D_MODEL, 1, 1, 0),
                  spec(D_MODEL, 0, 1, 0)],
        out_specs=spec(D_MODEL, 0, 1, 0),
        out_shape=jax.ShapeDtypeStruct((t, D_MODEL), BF16),
        scratch_shapes=[pltpu.VMEM((rows, LANES), BF16),
                        pltpu.VMEM((ATT_KV_HEADS, 2 * 2 * bq, LANES), BF16),
                        pltpu.VMEM((2 * 2 * bq, LANES), BF16),
                        pltpu.VMEM((ATT_KV_HEADS, rows, 2 * 2 * bq), BF16),
                        pltpu.VMEM((ATT_KV_HEADS, rows, 2 * 2 * bq), BF16)],
        compiler_params=_params(("arbitrary", "arbitrary"), 32),
        name="attention",
    )(sinks, aq, akv, akv, akv, akv, gates, ret)


OUT_PROJ_ROW_CHUNKS = 2


def _out_proj_kernel(alpha, m_ref, x_ref, w_ref, g_ref, b_ref, h_ref, hb_ref):
    rows = h_ref.shape[0] // OUT_PROJ_ROW_CHUNKS
    for c in range(OUT_PROJ_ROW_CHUNKS):
        rs = slice(c * rows, (c + 1) * rows)
        y = jnp.dot(m_ref[rs, :], w_ref[...], preferred_element_type=F32)
        h = _layer_norm(alpha * x_ref[rs, :] + y, g_ref[...], b_ref[...])
        h_ref[rs, :] = h
        hb_ref[rs, :] = h.astype(BF16)


def _out_proj(merged, x2, wb, ln_g, ln_b, alpha):
    t, d = x2.shape
    tm = 512
    row = pl.BlockSpec((tm, d), lambda i: (i, 0))
    vec = pl.BlockSpec((1, d), lambda i: (0, 0))
    return pl.pallas_call(
        functools.partial(_out_proj_kernel, alpha),
        grid=(t // tm,),
        in_specs=[row, row, pl.BlockSpec((d, d), lambda i: (0, 0)), vec, vec],
        out_specs=[row, row],
        out_shape=[jax.ShapeDtypeStruct((t, d), F32), jax.ShapeDtypeStruct((t, d), BF16)],
        compiler_params=_params(("parallel",), 48),
        name="out_proj_ln",
    )(merged, x2, wb, ln_g.reshape(1, d), ln_b.reshape(1, d))


def _ffn_kernel(alpha, hb_ref, h_hbm, wg_ref, wu_ref, wd_ref, g_ref, b_ref, o_ref, hres_ref, sem):
    i = pl.program_id(0)
    f = pl.program_id(1)
    tm = o_ref.shape[0]
    res_copy = pltpu.make_async_copy(h_hbm.at[pl.ds(i * tm, tm), :], hres_ref, sem)

    @pl.when(f == 0)
    def _init():
        res_copy.start()
        o_ref[...] = jnp.zeros_like(o_ref)

    hb = hb_ref[...]
    gt = jnp.dot(hb, wg_ref[...], preferred_element_type=F32)
    ut = jnp.dot(hb, wu_ref[...], preferred_element_type=F32)
    a = (gt * _sigmoid(gt) * ut).astype(BF16)
    o_ref[...] += jnp.dot(a, wd_ref[...], preferred_element_type=F32)

    @pl.when(f == pl.num_programs(1) - 1)
    def _finish():
        res_copy.wait()
        o_ref[...] = _layer_norm(alpha * hres_ref[...] + o_ref[...], g_ref[...], b_ref[...])


def _ffn(h1, h1b, wg, wu, wd, ln_g, ln_b, alpha):
    t, d = h1.shape
    dff = wg.shape[1]
    tm, tf = 1024, 512
    row = pl.BlockSpec((tm, d), lambda i, f: (i, 0))
    vec = pl.BlockSpec((1, d), lambda i, f: (0, 0))
    return pl.pallas_call(
        functools.partial(_ffn_kernel, alpha),
        grid=(t // tm, dff // tf),
        in_specs=[row,
                  pl.BlockSpec(memory_space=pl.ANY),
                  pl.BlockSpec((d, tf), lambda i, f: (0, f)),
                  pl.BlockSpec((d, tf), lambda i, f: (0, f)),
                  pl.BlockSpec((tf, d), lambda i, f: (f, 0)),
                  vec, vec],
        out_specs=row,
        out_shape=jax.ShapeDtypeStruct((t, d), F32),
        scratch_shapes=[pltpu.VMEM((tm, d), F32), pltpu.SemaphoreType.DMA(())],
        compiler_params=_params(("parallel", "arbitrary"), 60),
        name="ffn_ln",
    )(h1b, h1, wg, wu, wd, ln_g.reshape(1, d), ln_b.reshape(1, d))


def kernel(x, positions, w_in, ret_gn_g, ret_gn_b, att_sinks, w_out, ln1_g, ln1_b,
           w_gate, w_up, w_down, ln2_g, ln2_b):
    bsz, seq, d = x.shape
    depth = w_in.shape[0]
    alpha = (2.0 * depth) ** 0.25
    h = x.reshape(bsz * seq, d)
    cos_r, sin_r, ca, s1, s2, hb = _prep(positions, h)
    for l in range(depth):
        if l > 0:
            hb = h.astype(BF16)
        w = w_in[l]
        rqk = _proj_segment(hb, w, OFF_RQ, OFF_RV - OFF_RQ, 1024, _ep_ret_rotary,
                            (cos_r, sin_r), "in_proj_ret_qk")
        rv, wgb = _proj_segment(hb, w, OFF_RV, D_MODEL, 1024, _ep_plain, (), "in_proj_ret_v",
                                ride=w_gate[l])
        rgs, wub = _proj_segment(hb, w, OFF_RG, D_MODEL, 1024, _ep_swish, (), "in_proj_ret_g",
                                 ride=w_up[l])
        aq = _proj_segment(hb, w, OFF_AQ, D_MODEL, 1024, _ep_att_q, (ca, s1, s2), "in_proj_att_q")
        akv = _proj_segment(hb, w, OFF_AK, 2 * KV_W, 2 * KV_W, _ep_att_kv, (ca, s1, s2),
                            "in_proj_att_kv")
        gates, wdb = _proj_segment(hb, w, OFF_GA, 2 * D_MODEL, 1024, _ep_sigmoid, (),
                                   "in_proj_gates", ride=w_down[l])
        ret = _retention(rqk, rv, rgs, gates, ret_gn_g[l], ret_gn_b[l], bsz, seq)
        merged = _attention(aq, akv, gates, ret, att_sinks[l], bsz, seq)
        h, hb = _out_proj(merged, h, w_out[l].astype(BF16), ln1_g[l], ln1_b[l], alpha)
        h = _ffn(h, hb, wgb, wub, wdb, ln2_g[l], ln2_b[l], alpha)
    return h.reshape(bsz, seq, d)
```

```python
import functools
import math

import jax
import jax.numpy as jnp
from jax import lax
from jax.experimental import pallas as pl
from jax.experimental.pallas import tpu as pltpu

F32 = jnp.float32
BF16 = jnp.bfloat16

D_MODEL = 2048
RET_HEADS = 4
RET_DV = D_MODEL // RET_HEADS
RET_DK = RET_DV // 2
RET_CHUNK = 256
RET_THETA = 10000.0
ATT_HEAD_DIM = 64
ATT_HEADS = D_MODEL // ATT_HEAD_DIM
ATT_KV_HEADS = ATT_HEADS // 8
ATT_GROUP = ATT_HEADS // ATT_KV_HEADS
WINDOW = 128
ATT_BLOCK = WINDOW
ROPE_THETA = 500000.0
ROPE_DIM = ATT_HEAD_DIM // 4
LN_EPS = 1e-5
GN_EPS = 1e-5

LANES = 128
LOG2E = math.log2(math.e)

OFF_RQ = 0
OFF_RK = OFF_RQ + RET_HEADS * RET_DK
OFF_RV = OFF_RK + RET_HEADS * RET_DK
OFF_RG = OFF_RV + RET_HEADS * RET_DV
OFF_AQ = OFF_RG + RET_HEADS * RET_DV
OFF_AK = OFF_AQ + ATT_HEADS * ATT_HEAD_DIM
OFF_AV = OFF_AK + ATT_KV_HEADS * ATT_HEAD_DIM
OFF_GA = OFF_AV + ATT_KV_HEADS * ATT_HEAD_DIM
OFF_GB = OFF_GA + D_MODEL
KV_W = ATT_KV_HEADS * ATT_HEAD_DIM

MIB = 1024 * 1024


def _params(sem, vmem_mib, flags=None):
    return pltpu.CompilerParams(dimension_semantics=sem, vmem_limit_bytes=vmem_mib * MIB,
                                flags=flags)


def _sigmoid(v):
    return 0.5 * jnp.tanh(0.5 * v) + 0.5


def _layer_norm(z, g, b):
    mu = jnp.mean(z, axis=-1, keepdims=True)
    d = z - mu
    var = jnp.mean(d * d, axis=-1, keepdims=True)
    return d * lax.rsqrt(var + LN_EPS) * g + b


def _prep_kernel(pos_ref, fr_ref, fa_ref, x_ref, cr_ref, sr_ref, ca_ref, s1_ref, s2_ref, xb_ref):
    xb_ref[...] = x_ref[...].astype(BF16)
    pos = pos_ref[...].astype(F32)
    ang_r = pos * fr_ref[...]
    cr_ref[...] = jnp.cos(ang_r)
    sr_ref[...] = jnp.sin(ang_r)
    ang_a = pos * fa_ref[...]
    sa = jnp.sin(ang_a)
    d = lax.broadcasted_iota(jnp.int32, ang_a.shape, 1) & (ATT_HEAD_DIM - 1)
    half = ROPE_DIM // 2
    ca_ref[...] = jnp.cos(ang_a)
    s1_ref[...] = jnp.where(d < half, -sa, 0.0)
    s2_ref[...] = jnp.where((d >= half) & (d < ROPE_DIM), sa, 0.0)


def _prep(positions, x2):
    t, dm = x2.shape
    tm = 1024
    pos_b = jnp.broadcast_to(positions.reshape(t, 1), (t, LANES))
    inv_r = 1.0 / (RET_THETA ** jnp.linspace(0.0, 1.0, RET_DK // 2, dtype=F32))
    inv_a = 1.0 / (ROPE_THETA ** (jnp.arange(0, ROPE_DIM, 2, dtype=F32) / ROPE_DIM))
    d = jnp.arange(LANES) % ATT_HEAD_DIM
    fa = jnp.where(d < ROPE_DIM, inv_a[d % (ROPE_DIM // 2)], 0.0).astype(F32)
    row = pl.BlockSpec((tm, LANES), lambda i: (i, 0))
    xrow = pl.BlockSpec((tm, dm), lambda i: (i, 0))
    vec = pl.BlockSpec((1, LANES), lambda i: (0, 0))
    tab = jax.ShapeDtypeStruct((t, LANES), F32)
    return pl.pallas_call(
        _prep_kernel,
        grid=(t // tm,),
        in_specs=[row, vec, vec, xrow],
        out_specs=[row] * 5 + [xrow],
        out_shape=[tab] * 5 + [jax.ShapeDtypeStruct((t, dm), BF16)],
        compiler_params=_params(("parallel",), 40),
        name="rotary_tables_xcast",
    )(pos_b, inv_r.reshape(1, LANES), fa.reshape(1, LANES), x2)


def _ep_plain(acc, j):
    return acc


def _ep_swish(acc, j):
    return acc * _sigmoid(acc)


def _ep_sigmoid(acc, j):
    return _sigmoid(acc)


def _ep_ret_rotary(acc, j, cos_ref, sin_ref):
    scale = jnp.where(j == 0, RET_DK ** -0.5, 1.0)
    cos = cos_ref[...] * scale
    sin = sin_ref[...] * scale
    half = RET_DK // 2
    outs = []
    for c0 in range(0, acc.shape[1], RET_DK):
        a1 = acc[:, c0:c0 + half]
        a2 = acc[:, c0 + half:c0 + RET_DK]
        outs += [a1 * cos - a2 * sin, a2 * cos + a1 * sin]
    return jnp.concatenate(outs, axis=-1)


def _rope_lanes(a, ca, s1, s2):
    outs = []
    for c0 in range(0, a.shape[1], LANES):
        ac = a[:, c0:c0 + LANES]
        up = pltpu.roll(ac, LANES - ROPE_DIM // 2, 1)
        dn = pltpu.roll(ac, ROPE_DIM // 2, 1)
        outs.append(ac * ca + up * s1 + dn * s2)
    return jnp.concatenate(outs, axis=-1)


def _ep_att_q(acc, j, ca_ref, s1_ref, s2_ref):
    c = (ATT_HEAD_DIM ** -0.5) * LOG2E
    return _rope_lanes(acc, ca_ref[...] * c, s1_ref[...] * c, s2_ref[...] * c)


def _ep_att_kv(acc, j, ca_ref, s1_ref, s2_ref):
    k = _rope_lanes(acc[:, :KV_W], ca_ref[...], s1_ref[...], s2_ref[...])
    return jnp.concatenate([k, acc[:, KV_W:]], axis=-1)


def _proj_kernel(epilogue, n_tab, x_ref, w_ref, *rest):
    tab_refs, rest = rest[:n_tab], rest[n_tab:]
    ride_ref, o_ref, ride_out_ref, wb_ref = rest if len(rest) == 4 else (None, rest[0], None, rest[1])

    @pl.when(pl.program_id(1) == 0)
    def _cast_weights():
        wb_ref[...] = w_ref[...].astype(BF16)

    if ride_ref is not None:
        ride_out_ref[...] = ride_ref[...].astype(BF16)
    acc = jnp.dot(x_ref[...], wb_ref[...], preferred_element_type=F32)
    o_ref[...] = epilogue(acc, pl.program_id(0), *tab_refs).astype(o_ref.dtype)


def _proj_segment(xb, w, col0, width, tn, epilogue, tables=(), name="in_proj", ride=None):
    t, k = xb.shape
    tm = 1024
    nt = t // tm
    assert col0 % LANES == 0 and width % tn == 0
    tab = pl.BlockSpec((tm, LANES), lambda j, i: (i, 0))
    in_specs = [pl.BlockSpec((tm, k), lambda j, i: (i, 0)),
                pl.BlockSpec((pl.Element(k), pl.Element(tn)),
                             lambda j, i: (0, pl.multiple_of(col0 + j * tn, LANES)))]
    in_specs += [tab] * len(tables)
    out_specs = [pl.BlockSpec((tm, tn), lambda j, i: (i, j))]
    out_shape = [jax.ShapeDtypeStruct((t, width), BF16)]
    operands = [xb, w, *tables]
    if ride is not None:
        steps = (width // tn) * nt
        slab = ride.shape[0] // steps
        assert slab * steps == ride.shape[0] and slab % 16 == 0
        ride_spec = pl.BlockSpec((slab, ride.shape[1]), lambda j, i: (j * nt + i, 0))
        in_specs.append(ride_spec)
        out_specs.append(ride_spec)
        out_shape.append(jax.ShapeDtypeStruct(ride.shape, BF16))
        operands.append(ride)
    outs = pl.pallas_call(
        functools.partial(_proj_kernel, epilogue, len(tables)),
        grid=(width // tn, nt),
        in_specs=in_specs,
        out_specs=out_specs,
        out_shape=out_shape,
        scratch_shapes=[pltpu.VMEM((k, tn), BF16)],
        compiler_params=_params(("parallel", "arbitrary"), 56),
        name=name,
    )(*operands)
    return outs[0] if ride is None else outs


def _retention_kernel(q_ref, k_ref, v_ref, gs_ref, ga_ref, gng_ref, gnb_ref,
                      o_ref, state_ref, dmask_ref, qdec_ref, kdec_ref, cdec_ref):
    c = RET_CHUNK

    @pl.when(pl.program_id(1) == 0)
    def _init():
        state_ref[...] = jnp.zeros_like(state_ref)
        ii = lax.broadcasted_iota(jnp.int32, (c, c), 0)
        jj = lax.broadcasted_iota(jnp.int32, (c, c), 1)
        diff = (ii - jj).astype(F32)
        idx = lax.broadcasted_iota(jnp.int32, (c, RET_DK), 0).astype(F32)
        for h in range(RET_HEADS):
            def log_gamma(shape):
                return jnp.log(1.0 - jnp.exp2(-5.0 - jnp.full(shape, float(h), F32)))

            dmask_ref[h] = jnp.where(
                diff >= 0.0, jnp.exp(log_gamma((c, c)) * jnp.maximum(diff, 0.0)), 0.0)
            lgl = log_gamma((c, RET_DK))
            qdec_ref[h] = jnp.exp(lgl * (idx + 1.0))
            kdec_ref[h] = jnp.exp(lgl * (c - 1.0 - idx))
            cdec_ref[h] = jnp.exp(log_gamma((8, LANES)) * float(c))

    for r0, h in ((r0, h) for r0 in range(0, q_ref.shape[0], c) for h in range(RET_HEADS)):
        rs = slice(r0, r0 + c)
        ksl = slice(h * RET_DK, (h + 1) * RET_DK)
        vsl = slice(h * RET_DV, (h + 1) * RET_DV)
        qb = q_ref[rs, ksl]
        kb = k_ref[rs, ksl]
        v = v_ref[rs, vsl]
        qd = (qb.astype(F32) * qdec_ref[h]).astype(BF16)
        kd = (kb.astype(F32) * kdec_ref[h]).astype(BF16)

        s = lax.dot_general(qb, kb, (((1,), (1,)), ((), ())), preferred_element_type=F32)
        s = s * dmask_ref[h]
        inner = jnp.dot(s.astype(BF16), v, preferred_element_type=F32)
        state = state_ref[h]
        cross = jnp.dot(qd, state.astype(BF16), preferred_element_type=F32)
        kv = lax.dot_general(kd, v, (((0,), (0,)), ((), ())), preferred_element_type=F32)
        state_ref[h] = state * cdec_ref[h, 0:1, 0:1] + kv

        y = inner + cross
        mu = jnp.mean(y, axis=-1, keepdims=True)
        d = y - mu
        var = jnp.mean(d * d, axis=-1, keepdims=True)
        yn = d * lax.rsqrt(var + GN_EPS) * gng_ref[:, vsl] + gnb_ref[:, vsl]
        gate = ga_ref[rs, vsl].astype(F32) * gs_ref[rs, vsl].astype(F32)
        o_ref[rs, vsl] = (gate * yn).astype(o_ref.dtype)


RET_STEP_CHUNKS = 2


def _retention(rqk, rv, rgs, gates, gn_g, gn_b, bsz, seq):
    t = bsz * seq
    c = RET_CHUNK
    rows_per_step = RET_STEP_CHUNKS * c
    nc = seq // rows_per_step
    nh = RET_HEADS

    def rows(width, col):
        return pl.BlockSpec((rows_per_step, width), lambda b, n: (b * nc + n, col))

    gn = pl.BlockSpec((1, D_MODEL), lambda b, n: (0, 0))
    return pl.pallas_call(
        _retention_kernel,
        grid=(bsz, nc),
        in_specs=[rows(nh * RET_DK, 0), rows(nh * RET_DK, 1), rows(D_MODEL, 0), rows(D_MODEL, 0),
                  rows(D_MODEL, 0), gn, gn],
        out_specs=rows(D_MODEL, 0),
        out_shape=jax.ShapeDtypeStruct((t, D_MODEL), BF16),
        scratch_shapes=[pltpu.VMEM((nh, RET_DK, RET_DV), F32), pltpu.VMEM((nh, c, c), F32),
                        pltpu.VMEM((nh, c, RET_DK), F32), pltpu.VMEM((nh, c, RET_DK), F32),
                        pltpu.VMEM((nh, 8, LANES), F32)],
        compiler_params=_params(("parallel", "arbitrary"), 32),
        name="retention",
    )(rqk, rqk, rv, rgs, gates, gn_g.reshape(1, -1), gn_b.reshape(1, -1))


SINK_LANES = 8


def _pair_blocks(prev, cur, g):
    c0 = (g // 2) * LANES
    kk = jnp.concatenate([prev[:, c0:c0 + LANES], cur[:, c0:c0 + LANES]], axis=0).astype(F32)
    lane = lax.broadcasted_iota(jnp.int32, kk.shape, 1)
    row = lax.broadcasted_iota(jnp.int32, kk.shape, 0)
    hd = ATT_HEAD_DIM
    own = jnp.where(((lane >= hd) if g % 2 else (lane < hd)) & (row != 0), kk, 0.0)
    other = pltpu.roll(own, hd, 1)
    lo, hi = (other, own) if g % 2 else (own, other)
    return lo.astype(BF16), hi.astype(BF16)


def _attention_consts(sink_ref, qaug_ref, kaug_ref, vones_ref):
    nk = 2 * ATT_BLOCK
    row_q = lax.broadcasted_iota(jnp.int32, qaug_ref.shape, 0)
    lane_q = lax.broadcasted_iota(jnp.int32, qaug_ref.shape, 1)
    pair_of_lane = (lane_q & (SINK_LANES - 1)) >> 1
    onehot = (lane_q < 2 * SINK_LANES) & (pair_of_lane == row_q // ATT_BLOCK)
    qaug_ref[...] = jnp.where(onehot, 1.0, 0.0).astype(BF16)
    row_k = lax.broadcasted_iota(jnp.int32, vones_ref.shape, 0)
    lane_k = lax.broadcasted_iota(jnp.int32, vones_ref.shape, 1)
    vones_ref[...] = jnp.where((row_k < nk) == (lane_k < ATT_HEAD_DIM), 1.0, 0.0).astype(BF16)
    for g in range(ATT_KV_HEADS):
        vals = jnp.zeros(vones_ref.shape, F32)
        for j in range(ATT_GROUP):
            sk = sink_ref[g * ATT_GROUP + j] * LOG2E
            vals = jnp.where((lane_k == j) & (row_k == (j % 2) * nk), sk, vals)
        hi = vals.astype(BF16).astype(F32)
        kaug_ref[g] = (hi + pltpu.roll(vals - hi, SINK_LANES, 1)).astype(BF16)


def _attention_kernel(last, sink_ref, q_ref, kp_ref, kc_ref, vp_ref, vc_ref, gb_ref, ret_ref, o_ref,
                      qaug_ref, kaug_ref, vones_ref, *p_refs):
    i = pl.program_id(1)
    bq = ATT_BLOCK
    nk = 2 * bq
    pairs = ATT_GROUP // 2

    @pl.when((pl.program_id(0) == 0) & (i == 0))
    def _init():
        _attention_consts(sink_ref, qaug_ref, kaug_ref, vones_ref)

    def scores(slot):
        kp, kc = kp_ref[...], kc_ref[...]
        qi = lax.broadcasted_iota(jnp.int32, (bq, nk), 0)
        kj = lax.broadcasted_iota(jnp.int32, (bq, nk), 1)
        dist = qi + bq - kj
        kmin = jnp.where(i > 0, 0, bq)
        valid = ((dist >= 0) & (dist < WINDOW) & (kj >= kmin)) | (kj == 0)
        bias = jnp.where(valid, 0.0, -jnp.inf)
        qaug = qaug_ref[...]
        for g in range(ATT_KV_HEADS):
            k_lo, k_hi = _pair_blocks(kp, kc, g)
            kfull = jnp.concatenate([jnp.concatenate([k_lo, k_hi], axis=0), kaug_ref[g]], axis=1)
            col0 = g * ATT_GROUP * ATT_HEAD_DIM
            qp = jnp.concatenate(
                [q_ref[:, col0 + p * LANES:col0 + (p + 1) * LANES] for p in range(pairs)], axis=0)
            qfull = jnp.concatenate([qp, qaug], axis=1)
            s = lax.dot_general(qfull, kfull, (((1,), (1,)), ((), ())),
                                preferred_element_type=F32)
            for p in range(pairs):
                for e in range(2):
                    sj = s[p * bq:(p + 1) * bq, e * nk:(e + 1) * nk] + bias
                    m = jnp.max(sj, axis=-1, keepdims=True)
                    p_refs[slot][g, p * bq:(p + 1) * bq, e * nk:(e + 1) * nk] = (
                        jnp.exp2(sj - m).astype(BF16))

    def values(slot):
        vp, vc = vp_ref[...], vc_ref[...]
        vones = vones_ref[...]
        for g in range(ATT_KV_HEADS):
            v_lo, v_hi = _pair_blocks(vp, vc, g)
            vfull = jnp.concatenate([jnp.concatenate([v_lo, v_hi], axis=0), vones], axis=1)
            o = jnp.dot(p_refs[slot][g], vfull, preferred_element_type=F32)
            col0 = g * ATT_GROUP * ATT_HEAD_DIM
            for p in range(pairs):
                num = o[p * bq:(p + 1) * bq, :LANES]
                den = o[p * bq:(p + 1) * bq, LANES:]
                c1 = col0 + p * LANES
                gate = gb_ref[:, c1:c1 + LANES].astype(F32)
                ret = ret_ref[:, c1:c1 + LANES].astype(F32)
                o_ref[:, c1:c1 + LANES] = (gate * num / den + ret).astype(o_ref.dtype)

    @pl.when(i == 0)
    def _first():
        scores(0)

    for par in (0, 1):
        @pl.when((i > 0) & (i < last) & (lax.rem(i, 2) == par))
        def _steady():
            values(1 - par)
            scores(par)

    @pl.when(i == last)
    def _last():
        values((last - 1) % 2)


def _attention(aq, akv, gates, ret, sinks, bsz, seq):
    t = bsz * seq
    bq = ATT_BLOCK
    nb = seq // bq
    rows = (ATT_GROUP // 2) * bq

    def spec(width, col, lag, back):
        def index(b, i):
            blk = jnp.clip(i - lag, 0, nb - 1)
            return (b * nb + jnp.maximum(blk - back, 0), col)
        return pl.BlockSpec((bq, width), index)

    return pl.pallas_call(
        functools.partial(_attention_kernel, nb),
        grid=(bsz, nb + 1),
        in_specs=[pl.BlockSpec(memory_space=pltpu.SMEM), spec(D_MODEL, 0, 0, 0),
                  spec(KV_W, 0, 0, 1), spec(KV_W, 0, 0, 0),
                  spec(KV_W, 1, 1, 1), spec(KV_W, 1, 1, 0), spec(D_MODEL, 1, 1, 0),
                  spec(D_MODEL, 0, 1, 0)],
        out_specs=spec(D_MODEL, 0, 1, 0),
        out_shape=jax.ShapeDtypeStruct((t, D_MODEL), BF16),
        scratch_shapes=[pltpu.VMEM((rows, LANES), BF16),
                        pltpu.VMEM((ATT_KV_HEADS, 2 * 2 * bq, LANES), BF16),
                        pltpu.VMEM((2 * 2 * bq, LANES), BF16),
                        pltpu.VMEM((ATT_KV_HEADS, rows, 2 * 2 * bq), BF16),
                        pltpu.VMEM((ATT_KV_HEADS, rows, 2 * 2 * bq), BF16)],
        compiler_params=_params(("arbitrary", "arbitrary"), 32),
        name="attention",
    )(sinks, aq, akv, akv, akv, akv, gates, ret)


OUT_PROJ_ROW_CHUNKS = 2


def _out_proj_kernel(alpha, m_ref, x_ref, w_ref, g_ref, b_ref, h_ref, hb_ref):
    rows = h_ref.shape[0] // OUT_PROJ_ROW_CHUNKS
    for c in range(OUT_PROJ_ROW_CHUNKS):
        rs = slice(c * rows, (c + 1) * rows)
        y = jnp.dot(m_ref[rs, :], w_ref[...], preferred_element_type=F32)
        h = _layer_norm(alpha * x_ref[rs, :] + y, g_ref[...], b_ref[...])
        h_ref[rs, :] = h
        hb_ref[rs, :] = h.astype(BF16)


def _out_proj(merged, x2, wb, ln_g, ln_b, alpha):
    t, d = x2.shape
    tm = 512
    row = pl.BlockSpec((tm, d), lambda i: (i, 0))
    vec = pl.BlockSpec((1, d), lambda i: (0, 0))
    return pl.pallas_call(
        functools.partial(_out_proj_kernel, alpha),
        grid=(t // tm,),
        in_specs=[row, row, pl.BlockSpec((d, d), lambda i: (0, 0)), vec, vec],
        out_specs=[row, row],
        out_shape=[jax.ShapeDtypeStruct((t, d), F32), jax.ShapeDtypeStruct((t, d), BF16)],
        compiler_params=_params(("parallel",), 48),
        name="out_proj_ln",
    )(merged, x2, wb, ln_g.reshape(1, d), ln_b.reshape(1, d))


def _ffn_kernel(alpha, hb_ref, h_hbm, wg_ref, wu_ref, wd_ref, g_ref, b_ref, o_ref, hres_ref, sem):
    i = pl.program_id(0)
    f = pl.program_id(1)
    tm = o_ref.shape[0]
    res_copy = pltpu.make_async_copy(h_hbm.at[pl.ds(i * tm, tm), :], hres_ref, sem)

    @pl.when(f == 0)
    def _init():
        res_copy.start()
        o_ref[...] = jnp.zeros_like(o_ref)

    hb = hb_ref[...]
    gt = jnp.dot(hb, wg_ref[...], preferred_element_type=F32)
    ut = jnp.dot(hb, wu_ref[...], preferred_element_type=F32)
    a = (gt * _sigmoid(gt) * ut).astype(BF16)
    o_ref[...] += jnp.dot(a, wd_ref[...], preferred_element_type=F32)

    @pl.when(f == pl.num_programs(1) - 1)
    def _finish():
        res_copy.wait()
        o_ref[...] = _layer_norm(alpha * hres_ref[...] + o_ref[...], g_ref[...], b_ref[...])


def _ffn(h1, h1b, wg, wu, wd, ln_g, ln_b, alpha):
    t, d = h1.shape
    dff = wg.shape[1]
    tm, tf = 1024, 512
    row = pl.BlockSpec((tm, d), lambda i, f: (i, 0))
    vec = pl.BlockSpec((1, d), lambda i, f: (0, 0))
    return pl.pallas_call(
        functools.partial(_ffn_kernel, alpha),
        grid=(t // tm, dff // tf),
        in_specs=[row,
                  pl.BlockSpec(memory_space=pl.ANY),
                  pl.BlockSpec((d, tf), lambda i, f: (0, f)),
                  pl.BlockSpec((d, tf), lambda i, f: (0, f)),
                  pl.BlockSpec((tf, d), lambda i, f: (f, 0)),
                  vec, vec],
        out_specs=row,
        out_shape=jax.ShapeDtypeStruct((t, d), F32),
        scratch_shapes=[pltpu.VMEM((tm, d), F32), pltpu.SemaphoreType.DMA(())],
        compiler_params=_params(("parallel", "arbitrary"), 60),
        name="ffn_ln",
    )(h1b, h1, wg, wu, wd, ln_g.reshape(1, d), ln_b.reshape(1, d))


def kernel(x, positions, w_in, ret_gn_g, ret_gn_b, att_sinks, w_out, ln1_g, ln1_b,
           w_gate, w_up, w_down, ln2_g, ln2_b):
    bsz, seq, d = x.shape
    depth = w_in.shape[0]
    alpha = (2.0 * depth) ** 0.25
    h = x.reshape(bsz * seq, d)
    cos_r, sin_r, ca, s1, s2, hb = _prep(positions, h)
    for l in range(depth):
        if l > 0:
            hb = h.astype(BF16)
        w = w_in[l]
        rqk = _proj_segment(hb, w, OFF_RQ, OFF_RV - OFF_RQ, 1024, _ep_ret_rotary,
                            (cos_r, sin_r), "in_proj_ret_qk")
        rv, wgb = _proj_segment(hb, w, OFF_RV, D_MODEL, 1024, _ep_plain, (), "in_proj_ret_v",
                                ride=w_gate[l])
        rgs, wub = _proj_segment(hb, w, OFF_RG, D_MODEL, 1024, _ep_swish, (), "in_proj_ret_g",
                                 ride=w_up[l])
        aq = _proj_segment(hb, w, OFF_AQ, D_MODEL, 1024, _ep_att_q, (ca, s1, s2), "in_proj_att_q")
        akv = _proj_segment(hb, w, OFF_AK, 2 * KV_W, 2 * KV_W, _ep_att_kv, (ca, s1, s2),
                            "in_proj_att_kv")
        gates, wdb = _proj_segment(hb, w, OFF_GA, 2 * D_MODEL, 1024, _ep_sigmoid, (),
                                   "in_proj_gates", ride=w_down[l])
        ret = _retention(rqk, rv, rgs, gates, ret_gn_g[l], ret_gn_b[l], bsz, seq)
        merged = _attention(aq, akv, gates, ret, att_sinks[l], bsz, seq)
        h, hb = _out_proj(merged, h, w_out[l].astype(BF16), ln1_g[l], ln1_b[l], alpha)
        h = _ffn(h, hb, wgb, wub, wdb, ln2_g[l], ln2_b[l], alpha)
    return h.reshape(bsz, seq, d)
```

```python
import functools
import math

import jax
import jax.numpy as jnp
from jax import lax
from jax.experimental import pallas as pl
from jax.experimental.pallas import tpu as pltpu

F32 = jnp.float32
BF16 = jnp.bfloat16

D_MODEL = 2048
RET_HEADS = 4
RET_DV = D_MODEL // RET_HEADS
RET_DK = RET_DV // 2
RET_CHUNK = 256
RET_THETA = 10000.0
ATT_HEAD_DIM = 64
ATT_HEADS = D_MODEL // ATT_HEAD_DIM
ATT_KV_HEADS = ATT_HEADS // 8
ATT_GROUP = ATT_HEADS // ATT_KV_HEADS
WINDOW = 128
ATT_BLOCK = WINDOW
ROPE_THETA = 500000.0
ROPE_DIM = ATT_HEAD_DIM // 4
LN_EPS = 1e-5
GN_EPS = 1e-5

LANES = 128
LOG2E = math.log2(math.e)

OFF_RQ = 0
OFF_RK = OFF_RQ + RET_HEADS * RET_DK
OFF_RV = OFF_RK + RET_HEADS * RET_DK
OFF_RG = OFF_RV + RET_HEADS * RET_DV
OFF_AQ = OFF_RG + RET_HEADS * RET_DV
OFF_AK = OFF_AQ + ATT_HEADS * ATT_HEAD_DIM
OFF_AV = OFF_AK + ATT_KV_HEADS * ATT_HEAD_DIM
OFF_GA = OFF_AV + ATT_KV_HEADS * ATT_HEAD_DIM
OFF_GB = OFF_GA + D_MODEL
KV_W = ATT_KV_HEADS * ATT_HEAD_DIM

MIB = 1024 * 1024


def _params(sem, vmem_mib, flags=None):
    return pltpu.CompilerParams(dimension_semantics=sem, vmem_limit_bytes=vmem_mib * MIB,
                                flags=flags)


def _sigmoid(v):
    return 0.5 * jnp.tanh(0.5 * v) + 0.5


def _layer_norm(z, g, b):
    mu = jnp.mean(z, axis=-1, keepdims=True)
    d = z - mu
    var = jnp.mean(d * d, axis=-1, keepdims=True)
    return d * lax.rsqrt(var + LN_EPS) * g + b


ROPE_PACK = ATT_HEAD_DIM // ROPE_DIM
PREP_ROWS = 64


def _prep_kernel(pos_ref, posp_ref, fr_ref, fa_ref, x_ref,
                 cr_ref, sr_ref, ca_ref, s1_ref, s2_ref, xb_ref):
    fr = fr_ref[...]

    def ret_rows(k, carry):
        rs = pl.ds(pl.multiple_of(k * PREP_ROWS, PREP_ROWS), PREP_ROWS)
        xb_ref[rs, :] = x_ref[rs, :].astype(BF16)
        ang_r = pos_ref[rs, :].astype(F32) * fr
        cr_ref[rs, :] = jnp.cos(ang_r)
        sr_ref[rs, :] = jnp.sin(ang_r)
        return carry

    lax.fori_loop(0, pos_ref.shape[0] // PREP_ROWS, ret_rows, 0)
    ang_p = posp_ref[...].astype(F32) * fa_ref[...]
    cp = jnp.cos(ang_p)
    sp = jnp.sin(ang_p)
    dp = lax.broadcasted_iota(jnp.int32, ang_p.shape, 1) & (ROPE_DIM - 1)
    half = ROPE_DIM // 2
    s1p = jnp.where(dp < half, -sp, 0.0)
    s2p = jnp.where(dp >= half, sp, 0.0)
    lane = lax.broadcasted_iota(jnp.int32, ang_p.shape, 1)
    rotated = (lane & (ATT_HEAD_DIM - 1)) < ROPE_DIM
    rg = ang_p.shape[0]
    for g in range(ROPE_PACK):
        def spread(p, fill):
            a = pltpu.roll(p, LANES - ROPE_DIM * g, 1) if g else p
            return jnp.where(rotated, a, fill)

        rs = slice(g * rg, (g + 1) * rg)
        ca_ref[rs, :] = spread(cp, 1.0)
        s1_ref[rs, :] = spread(s1p, 0.0)
        s2_ref[rs, :] = spread(s2p, 0.0)


def _prep(positions, x2):
    t, dm = x2.shape
    tm = 1024
    rg = tm // ROPE_PACK
    pos_b = jnp.broadcast_to(positions.reshape(t, 1), (t, LANES))
    pos_p = jnp.repeat(positions.reshape(t // tm, ROPE_PACK, rg).transpose(0, 2, 1), ROPE_DIM, axis=-1)
    pos_p = jnp.tile(pos_p, (1, 1, LANES // ATT_HEAD_DIM)).reshape(t // ROPE_PACK, LANES)
    inv_r = 1.0 / (RET_THETA ** jnp.linspace(0.0, 1.0, RET_DK // 2, dtype=F32))
    inv_a = 1.0 / (ROPE_THETA ** (jnp.arange(0, ROPE_DIM, 2, dtype=F32) / ROPE_DIM))
    fa = inv_a[jnp.arange(LANES) % (ROPE_DIM // 2)]
    row = pl.BlockSpec((tm, LANES), lambda i: (i, 0))
    prow = pl.BlockSpec((rg, LANES), lambda i: (i, 0))
    xrow = pl.BlockSpec((tm, dm), lambda i: (i, 0))
    vec = pl.BlockSpec((1, LANES), lambda i: (0, 0))
    tab = jax.ShapeDtypeStruct((t, LANES), F32)
    return pl.pallas_call(
        _prep_kernel,
        grid=(t // tm,),
        in_specs=[row, prow, vec, vec, xrow],
        out_specs=[row] * 5 + [xrow],
        out_shape=[tab] * 5 + [jax.ShapeDtypeStruct((t, dm), BF16)],
        compiler_params=_params(("parallel",), 40),
        name="rotary_tables_xcast",
    )(pos_b, pos_p, inv_r.reshape(1, LANES), fa.reshape(1, LANES), x2)


def _ep_plain(acc, j, *tabs):
    return acc


def _ep_swish(acc, j, *tabs):
    return acc * _sigmoid(acc)


def _ep_sigmoid(acc, j, *tabs):
    return _sigmoid(acc)


def _ep_ret_rotary(acc, j, cos_ref, sin_ref, *att_tabs):
    scale = jnp.where(j == 0, RET_DK ** -0.5, 1.0)
    cos = cos_ref[...] * scale
    sin = sin_ref[...] * scale
    half = RET_DK // 2
    outs = []
    for c0 in range(0, acc.shape[1], RET_DK):
        a1 = acc[:, c0:c0 + half]
        a2 = acc[:, c0 + half:c0 + RET_DK]
        outs += [a1 * cos - a2 * sin, a2 * cos + a1 * sin]
    return jnp.concatenate(outs, axis=-1)


def _rope_lanes(a, ca, s1, s2):
    outs = []
    for c0 in range(0, a.shape[1], LANES):
        ac = a[:, c0:c0 + LANES]
        up = pltpu.roll(ac, LANES - ROPE_DIM // 2, 1)
        dn = pltpu.roll(ac, ROPE_DIM // 2, 1)
        outs.append(ac * ca + up * s1 + dn * s2)
    return jnp.concatenate(outs, axis=-1)


def _ep_att_q(acc, j, cos_ref, sin_ref, ca_ref, s1_ref, s2_ref):
    c = (ATT_HEAD_DIM ** -0.5) * LOG2E
    return _rope_lanes(acc, ca_ref[...] * c, s1_ref[...] * c, s2_ref[...] * c)


def _ep_att_kv(acc, j, ca_ref, s1_ref, s2_ref):
    k = _rope_lanes(acc[:, :KV_W], ca_ref[...], s1_ref[...], s2_ref[...])
    return jnp.concatenate([k, acc[:, KV_W:]], axis=-1)


def _proj_kernel(segments, n_tab, n_ride, x_ref, w_ref, *rest):
    tab_refs = rest[:n_tab]
    ride_refs = rest[n_tab:n_tab + n_ride]
    o_ref = rest[n_tab + n_ride]
    ride_out_refs = rest[n_tab + n_ride + 1:n_tab + 2 * n_ride + 1]
    wb_ref = rest[-1]
    j = pl.program_id(0)

    @pl.when(pl.program_id(1) == 0)
    def _cast_weights():
        wb_ref[...] = w_ref[...].astype(BF16)

    for j0, j1, epilogue, rides in segments:
        @pl.when((j >= j0) & (j < j1))
        def _segment():
            for r in rides:
                ride_out_refs[r][...] = ride_refs[r][...].astype(BF16)
            acc = jnp.dot(x_ref[...], wb_ref[...], preferred_element_type=F32)
            o_ref[...] = epilogue(acc, j - j0, *tab_refs).astype(o_ref.dtype)


def _proj_multi(xb, w, segs, tn, tables=(), rides=(), name="in_proj"):
    t, k = xb.shape
    tm = 1024
    nt = t // tm
    j0s, bounds = [], 0
    for col0, width, _ in segs:
        assert col0 % LANES == 0 and width % tn == 0
        j0s.append(bounds)
        bounds += width // tn
    n_tiles = bounds

    def w_col(j):
        col = segs[0][0] + j * tn
        for (col0, _, _), j0 in zip(segs[1:], j0s[1:]):
            col = jnp.where(j >= j0, col0 + (j - j0) * tn, col)
        return pl.multiple_of(col, LANES)

    tab = pl.BlockSpec((tm, LANES), lambda j, i: (i, 0))
    in_specs = [pl.BlockSpec((tm, k), lambda j, i: (i, 0)),
                pl.BlockSpec((pl.Element(k), pl.Element(tn)), lambda j, i: (0, w_col(j)))]
    in_specs += [tab] * len(tables)
    out_specs = [pl.BlockSpec((tm, tn), lambda j, i: (i, j))]
    out_shape = [jax.ShapeDtypeStruct((t, n_tiles * tn), BF16)]
    ride_steps = []
    for arr, first, n_slabs in rides:
        slab = arr.shape[0] // n_slabs
        assert slab * n_slabs == arr.shape[0] and slab % 16 == 0 and first + n_slabs <= n_tiles * nt
        spec = pl.BlockSpec(
            (slab, arr.shape[1]),
            lambda j, i, first=first, n=n_slabs: (jnp.clip(j * nt + i - first, 0, n - 1), 0))
        in_specs.append(spec)
        out_specs.append(spec)
        out_shape.append(jax.ShapeDtypeStruct(arr.shape, BF16))
        ride_steps.append((first, first + n_slabs))
    segments = []
    for (_, width, epilogue), j0 in zip(segs, j0s):
        j1 = j0 + width // tn
        active = tuple(r for r, (s0, s1) in enumerate(ride_steps) if s0 < j1 * nt and s1 > j0 * nt)
        segments.append((j0, j1, epilogue, active))
    outs = pl.pallas_call(
        functools.partial(_proj_kernel, tuple(segments), len(tables), len(rides)),
        grid=(n_tiles, nt),
        in_specs=in_specs,
        out_specs=out_specs,
        out_shape=out_shape,
        scratch_shapes=[pltpu.VMEM((k, tn), BF16)],
        compiler_params=_params(("arbitrary", "arbitrary"), 56),
        name=name,
    )(xb, w, *tables, *[arr for arr, _, _ in rides])
    return outs[0] if not rides else outs


def _retention_kernel(q_ref, k_ref, v_ref, gs_ref, ga_ref, gng_ref, gnb_ref,
                      o_ref, state_ref, dmask_ref, qdec_ref, kdec_ref, cdec_ref):
    c = RET_CHUNK

    @pl.when(pl.program_id(1) == 0)
    def _init():
        state_ref[...] = jnp.zeros_like(state_ref)
        ii = lax.broadcasted_iota(jnp.int32, (c, c), 0)
        jj = lax.broadcasted_iota(jnp.int32, (c, c), 1)
        diff = (ii - jj).astype(F32)
        idx = lax.broadcasted_iota(jnp.int32, (c, RET_DK), 0).astype(F32)
        for h in range(RET_HEADS):
            def log_gamma(shape):
                return jnp.log(1.0 - jnp.exp2(-5.0 - jnp.full(shape, float(h), F32)))

            dmask_ref[h] = jnp.where(
                diff >= 0.0, jnp.exp(log_gamma((c, c)) * jnp.maximum(diff, 0.0)), 0.0)
            lgl = log_gamma((c, RET_DK))
            qdec_ref[h] = jnp.exp(lgl * (idx + 1.0))
            kdec_ref[h] = jnp.exp(lgl * (c - 1.0 - idx))
            cdec_ref[h] = jnp.exp(log_gamma((8, LANES)) * float(c))

    for r0, h in ((r0, h) for r0 in range(0, q_ref.shape[0], c) for h in range(RET_HEADS)):
        rs = slice(r0, r0 + c)
        ksl = slice(h * RET_DK, (h + 1) * RET_DK)
        vsl = slice(h * RET_DV, (h + 1) * RET_DV)
        qb = q_ref[rs, ksl]
        kb = k_ref[rs, ksl]
        v = v_ref[rs, vsl]
        qd = (qb.astype(F32) * qdec_ref[h]).astype(BF16)
        kd = (kb.astype(F32) * kdec_ref[h]).astype(BF16)

        s = lax.dot_general(qb, kb, (((1,), (1,)), ((), ())), preferred_element_type=F32)
        s = s * dmask_ref[h]
        inner = jnp.dot(s.astype(BF16), v, preferred_element_type=F32)
        state = state_ref[h]
        cross = jnp.dot(qd, state.astype(BF16), preferred_element_type=F32)
        kv = lax.dot_general(kd, v, (((0,), (0,)), ((), ())), preferred_element_type=F32)
        state_ref[h] = state * cdec_ref[h, 0:1, 0:1] + kv

        y = inner + cross
        mu = jnp.mean(y, axis=-1, keepdims=True)
        d = y - mu
        var = jnp.mean(d * d, axis=-1, keepdims=True)
        yn = d * lax.rsqrt(var + GN_EPS) * gng_ref[:, vsl] + gnb_ref[:, vsl]
        gate = ga_ref[rs, vsl].astype(F32) * gs_ref[rs, vsl].astype(F32)
        o_ref[rs, vsl] = (gate * yn).astype(o_ref.dtype)


RET_STEP_CHUNKS = 2


def _retention(qa, vg, gn_g, gn_b, bsz, seq):
    t = bsz * seq
    c = RET_CHUNK
    rows_per_step = RET_STEP_CHUNKS * c
    nc = seq // rows_per_step
    nh = RET_HEADS

    def rows(width, col):
        return pl.BlockSpec((rows_per_step, width), lambda b, n: (b * nc + n, col))

    gn = pl.BlockSpec((1, D_MODEL), lambda b, n: (0, 0))
    return pl.pallas_call(
        _retention_kernel,
        grid=(bsz, nc),
        in_specs=[rows(nh * RET_DK, 0), rows(nh * RET_DK, 1), rows(D_MODEL, 0), rows(D_MODEL, 1),
                  rows(D_MODEL, 2), gn, gn],
        out_specs=rows(D_MODEL, 0),
        out_shape=jax.ShapeDtypeStruct((t, D_MODEL), BF16),
        scratch_shapes=[pltpu.VMEM((nh, RET_DK, RET_DV), F32), pltpu.VMEM((nh, c, c), F32),
                        pltpu.VMEM((nh, c, RET_DK), F32), pltpu.VMEM((nh, c, RET_DK), F32),
                        pltpu.VMEM((nh, 8, LANES), F32)],
        compiler_params=_params(("parallel", "arbitrary"), 32),
        name="retention",
    )(qa, qa, vg, vg, vg, gn_g.reshape(1, -1), gn_b.reshape(1, -1))


SINK_LANES = 8


def _pair_blocks(prev, cur, g):
    c0 = (g // 2) * LANES
    kk = jnp.concatenate([prev[:, c0:c0 + LANES], cur[:, c0:c0 + LANES]], axis=0).astype(F32)
    lane = lax.broadcasted_iota(jnp.int32, kk.shape, 1)
    row = lax.broadcasted_iota(jnp.int32, kk.shape, 0)
    hd = ATT_HEAD_DIM
    own = jnp.where(((lane >= hd) if g % 2 else (lane < hd)) & (row != 0), kk, 0.0)
    other = pltpu.roll(own, hd, 1)
    lo, hi = (other, own) if g % 2 else (own, other)
    return lo.astype(BF16), hi.astype(BF16)


def _attention_consts(sink_ref, qaug_ref, kaug_ref, vones_ref):
    nk = 2 * ATT_BLOCK
    row_q = lax.broadcasted_iota(jnp.int32, qaug_ref.shape, 0)
    lane_q = lax.broadcasted_iota(jnp.int32, qaug_ref.shape, 1)
    pair_of_lane = (lane_q & (SINK_LANES - 1)) >> 1
    onehot = (lane_q < 2 * SINK_LANES) & (pair_of_lane == row_q // ATT_BLOCK)
    qaug_ref[...] = jnp.where(onehot, 1.0, 0.0).astype(BF16)
    row_k = lax.broadcasted_iota(jnp.int32, vones_ref.shape, 0)
    lane_k = lax.broadcasted_iota(jnp.int32, vones_ref.shape, 1)
    vones_ref[...] = jnp.where((row_k < nk) == (lane_k < ATT_HEAD_DIM), 1.0, 0.0).astype(BF16)
    for g in range(ATT_KV_HEADS):
        vals = jnp.zeros(vones_ref.shape, F32)
        for j in range(ATT_GROUP):
            sk = sink_ref[g * ATT_GROUP + j] * LOG2E
            vals = jnp.where((lane_k == j) & (row_k == (j % 2) * nk), sk, vals)
        hi = vals.astype(BF16).astype(F32)
        kaug_ref[g] = (hi + pltpu.roll(vals - hi, SINK_LANES, 1)).astype(BF16)


def _attention_kernel(last, sink_ref, q_ref, kp_ref, kc_ref, vp_ref, vc_ref, gb_ref, ret_ref, o_ref,
                      qaug_ref, kaug_ref, vones_ref, *p_refs):
    i = pl.program_id(1)
    bq = ATT_BLOCK
    nk = 2 * bq
    pairs = ATT_GROUP // 2

    @pl.when((pl.program_id(0) == 0) & (i == 0))
    def _init():
        _attention_consts(sink_ref, qaug_ref, kaug_ref, vones_ref)

    def scores(slot):
        kp, kc = kp_ref[...], kc_ref[...]
        qi = lax.broadcasted_iota(jnp.int32, (bq, nk), 0)
        kj = lax.broadcasted_iota(jnp.int32, (bq, nk), 1)
        dist = qi + bq - kj
        kmin = jnp.where(i > 0, 0, bq)
        valid = ((dist >= 0) & (dist < WINDOW) & (kj >= kmin)) | (kj == 0)
        bias = jnp.where(valid, 0.0, -jnp.inf)
        qaug = qaug_ref[...]
        for g in range(ATT_KV_HEADS):
            k_lo, k_hi = _pair_blocks(kp, kc, g)
            kfull = jnp.concatenate([jnp.concatenate([k_lo, k_hi], axis=0), kaug_ref[g]], axis=1)
            col0 = g * ATT_GROUP * ATT_HEAD_DIM
            qp = jnp.concatenate(
                [q_ref[:, col0 + p * LANES:col0 + (p + 1) * LANES] for p in range(pairs)], axis=0)
            qfull = jnp.concatenate([qp, qaug], axis=1)
            s = lax.dot_general(qfull, kfull, (((1,), (1,)), ((), ())),
                                preferred_element_type=F32)
            for p in range(pairs):
                for e in range(2):
                    sj = s[p * bq:(p + 1) * bq, e * nk:(e + 1) * nk] + bias
                    m = jnp.max(sj, axis=-1, keepdims=True)
                    p_refs[slot][g, p * bq:(p + 1) * bq, e * nk:(e + 1) * nk] = (
                        jnp.exp2(sj - m).astype(BF16))

    def values(slot):
        vp, vc = vp_ref[...], vc_ref[...]
        vones = vones_ref[...]
        for g in range(ATT_KV_HEADS):
            v_lo, v_hi = _pair_blocks(vp, vc, g)
            vfull = jnp.concatenate([jnp.concatenate([v_lo, v_hi], axis=0), vones], axis=1)
            o = jnp.dot(p_refs[slot][g], vfull, preferred_element_type=F32)
            col0 = g * ATT_GROUP * ATT_HEAD_DIM
            for p in range(pairs):
                num = o[p * bq:(p + 1) * bq, :LANES]
                den = o[p * bq:(p + 1) * bq, LANES:]
                c1 = col0 + p * LANES
                gate = gb_ref[:, c1:c1 + LANES].astype(F32)
                ret = ret_ref[:, c1:c1 + LANES].astype(F32)
                o_ref[:, c1:c1 + LANES] = (gate * num / den + ret).astype(o_ref.dtype)

    @pl.when(i == 0)
    def _first():
        scores(0)

    for par in (0, 1):
        @pl.when((i > 0) & (i < last) & (lax.rem(i, 2) == par))
        def _steady():
            values(1 - par)
            scores(par)

    @pl.when(i == last)
    def _last():
        values((last - 1) % 2)


def _attention(qa, akv, vg, ret, sinks, bsz, seq):
    t = bsz * seq
    bq = ATT_BLOCK
    nb = seq // bq
    rows = (ATT_GROUP // 2) * bq

    def spec(width, col, lag, back):
        def index(b, i):
            blk = jnp.clip(i - lag, 0, nb - 1)
            return (b * nb + jnp.maximum(blk - back, 0), col)
        return pl.BlockSpec((bq, width), index)

    return pl.pallas_call(
        functools.partial(_attention_kernel, nb),
        grid=(bsz, nb + 1),
        in_specs=[pl.BlockSpec(memory_space=pltpu.SMEM), spec(D_MODEL, 1, 0, 0),
                  spec(KV_W, 0, 0, 1), spec(KV_W, 0, 0, 0),
                  spec(KV_W, 1, 1, 1), spec(KV_W, 1, 1, 0), spec(D_MODEL, 3, 1, 0),
                  spec(D_MODEL, 0, 1, 0)],
        out_specs=spec(D_MODEL, 0, 1, 0),
        out_shape=jax.ShapeDtypeStruct((t, D_MODEL), BF16),
        scratch_shapes=[pltpu.VMEM((rows, LANES), BF16),
                        pltpu.VMEM((ATT_KV_HEADS, 2 * 2 * bq, LANES), BF16),
                        pltpu.VMEM((2 * 2 * bq, LANES), BF16),
                        pltpu.VMEM((ATT_KV_HEADS, rows, 2 * 2 * bq), BF16),
                        pltpu.VMEM((ATT_KV_HEADS, rows, 2 * 2 * bq), BF16)],
        compiler_params=_params(("arbitrary", "arbitrary"), 32),
        name="attention",
    )(sinks, qa, akv, akv, akv, akv, vg, ret)


OUT_PROJ_ROW_CHUNKS = 2


def _out_proj_kernel(alpha, m_ref, x_ref, w_ref, g_ref, b_ref, h_ref, hb_ref):
    rows = h_ref.shape[0] // OUT_PROJ_ROW_CHUNKS
    for c in range(OUT_PROJ_ROW_CHUNKS):
        rs = slice(c * rows, (c + 1) * rows)
        y = jnp.dot(m_ref[rs, :], w_ref[...], preferred_element_type=F32)
        h = _layer_norm(alpha * x_ref[rs, :] + y, g_ref[...], b_ref[...])
        h_ref[rs, :] = h
        hb_ref[rs, :] = h.astype(BF16)


def _out_proj(merged, x2, wb, ln_g, ln_b, alpha):
    t, d = x2.shape
    tm = 512
    row = pl.BlockSpec((tm, d), lambda i: (i, 0))
    vec = pl.BlockSpec((1, d), lambda i: (0, 0))
    return pl.pallas_call(
        functools.partial(_out_proj_kernel, alpha),
        grid=(t // tm,),
        in_specs=[row, row, pl.BlockSpec((d, d), lambda i: (0, 0)), vec, vec],
        out_specs=[row, row],
        out_shape=[jax.ShapeDtypeStruct((t, d), F32), jax.ShapeDtypeStruct((t, d), BF16)],
        compiler_params=_params(("parallel",), 48),
        name="out_proj_ln",
    )(merged, x2, wb, ln_g.reshape(1, d), ln_b.reshape(1, d))


def _ffn_kernel(alpha, hb_ref, h_hbm, wg_ref, wu_ref, wd_ref, g_ref, b_ref, o_ref, hres_ref, sem):
    i = pl.program_id(0)
    f = pl.program_id(1)
    tm = o_ref.shape[0]
    res_copy = pltpu.make_async_copy(h_hbm.at[pl.ds(i * tm, tm), :], hres_ref, sem)

    @pl.when(f == 0)
    def _init():
        res_copy.start()
        o_ref[...] = jnp.zeros_like(o_ref)

    hb = hb_ref[...]
    gt = jnp.dot(hb, wg_ref[...], preferred_element_type=F32)
    ut = jnp.dot(hb, wu_ref[...], preferred_element_type=F32)
    a = (gt * _sigmoid(gt) * ut).astype(BF16)
    o_ref[...] += jnp.dot(a, wd_ref[...], preferred_element_type=F32)

    @pl.when(f == pl.num_programs(1) - 1)
    def _finish():
        res_copy.wait()
        o_ref[...] = _layer_norm(alpha * hres_ref[...] + o_ref[...], g_ref[...], b_ref[...])


def _ffn(h1, h1b, wg, wu, wd, ln_g, ln_b, alpha):
    t, d = h1.shape
    dff = wg.shape[1]
    tm, tf = 1024, 512
    row = pl.BlockSpec((tm, d), lambda i, f: (i, 0))
    vec = pl.BlockSpec((1, d), lambda i, f: (0, 0))
    return pl.pallas_call(
        functools.partial(_ffn_kernel, alpha),
        grid=(t // tm, dff // tf),
        in_specs=[row,
                  pl.BlockSpec(memory_space=pl.ANY),
                  pl.BlockSpec((d, tf), lambda i, f: (0, f)),
                  pl.BlockSpec((d, tf), lambda i, f: (0, f)),
                  pl.BlockSpec((tf, d), lambda i, f: (f, 0)),
                  vec, vec],
        out_specs=row,
        out_shape=jax.ShapeDtypeStruct((t, d), F32),
        scratch_shapes=[pltpu.VMEM((tm, d), F32), pltpu.SemaphoreType.DMA(())],
        compiler_params=_params(("parallel", "arbitrary"), 60),
        name="ffn_ln",
    )(h1b, h1, wg, wu, wd, ln_g.reshape(1, d), ln_b.reshape(1, d))


def kernel(x, positions, w_in, ret_gn_g, ret_gn_b, att_sinks, w_out, ln1_g, ln1_b,
           w_gate, w_up, w_down, ln2_g, ln2_b):
    bsz, seq, d = x.shape
    depth = w_in.shape[0]
    alpha = (2.0 * depth) ** 0.25
    h = x.reshape(bsz * seq, d)
    cos_r, sin_r, ca, s1, s2, hb = _prep(positions, h)
    for l in range(depth):
        if l > 0:
            hb = h.astype(BF16)
        w = w_in[l]
        steps = 4 * (bsz * seq // 1024)
        qa, wdb = _proj_multi(
            hb, w, [(OFF_RQ, OFF_RV - OFF_RQ, _ep_ret_rotary), (OFF_AQ, D_MODEL, _ep_att_q)], 1024,
            (cos_r, sin_r, ca, s1, s2), [(w_down[l], 0, steps)], "in_proj_qk")
        vg, wgb, wub = _proj_multi(
            hb, w, [(OFF_RV, D_MODEL, _ep_plain), (OFF_RG, D_MODEL, _ep_swish),
                    (OFF_GA, 2 * D_MODEL, _ep_sigmoid)], 1024,
            (), [(w_gate[l], 0, steps), (w_up[l], steps, steps)], "in_proj_vg")
        akv = _proj_multi(hb, w, [(OFF_AK, 2 * KV_W, _ep_att_kv)], 2 * KV_W, (ca, s1, s2), (),
                          "in_proj_att_kv")
        ret = _retention(qa, vg, ret_gn_g[l], ret_gn_b[l], bsz, seq)
        merged = _attention(qa, akv, vg, ret, att_sinks[l], bsz, seq)
        h, hb = _out_proj(merged, h, w_out[l].astype(BF16), ln1_g[l], ln1_b[l], alpha)
        h = _ffn(h, hb, wgb, wub, wdb, ln2_g[l], ln2_b[l], alpha)
    return h.reshape(bsz, seq, d)
```

```python
import functools
import math

import jax
import jax.numpy as jnp
from jax import lax
from jax.experimental import pallas as pl
from jax.experimental.pallas import tpu as pltpu

F32 = jnp.float32
BF16 = jnp.bfloat16

D_MODEL = 2048
RET_HEADS = 4
RET_DV = D_MODEL // RET_HEADS
RET_DK = RET_DV // 2
RET_CHUNK = 256
RET_THETA = 10000.0
ATT_HEAD_DIM = 64
ATT_HEADS = D_MODEL // ATT_HEAD_DIM
ATT_KV_HEADS = ATT_HEADS // 8
ATT_GROUP = ATT_HEADS // ATT_KV_HEADS
WINDOW = 128
ATT_BLOCK = WINDOW
ROPE_THETA = 500000.0
ROPE_DIM = ATT_HEAD_DIM // 4
LN_EPS = 1e-5
GN_EPS = 1e-5

LANES = 128
LOG2E = math.log2(math.e)

OFF_RQ = 0
OFF_RK = OFF_RQ + RET_HEADS * RET_DK
OFF_RV = OFF_RK + RET_HEADS * RET_DK
OFF_RG = OFF_RV + RET_HEADS * RET_DV
OFF_AQ = OFF_RG + RET_HEADS * RET_DV
OFF_AK = OFF_AQ + ATT_HEADS * ATT_HEAD_DIM
OFF_AV = OFF_AK + ATT_KV_HEADS * ATT_HEAD_DIM
OFF_GA = OFF_AV + ATT_KV_HEADS * ATT_HEAD_DIM
OFF_GB = OFF_GA + D_MODEL
KV_W = ATT_KV_HEADS * ATT_HEAD_DIM

MIB = 1024 * 1024


def _params(sem, vmem_mib, flags=None):
    return pltpu.CompilerParams(dimension_semantics=sem, vmem_limit_bytes=vmem_mib * MIB,
                                flags=flags)


def _sigmoid(v):
    return 0.5 * jnp.tanh(0.5 * v) + 0.5


def _layer_norm(z, g, b):
    mu = jnp.mean(z, axis=-1, keepdims=True)
    d = z - mu
    var = jnp.mean(d * d, axis=-1, keepdims=True)
    return d * lax.rsqrt(var + LN_EPS) * g + b


def _prep_kernel(pos_ref, fr_ref, fa_ref, x_ref, cr_ref, sr_ref, ca_ref, s1_ref, s2_ref, xb_ref):
    xb_ref[...] = x_ref[...].astype(BF16)
    pos = pos_ref[...].astype(F32)
    ang_r = pos * fr_ref[...]
    cr_ref[...] = jnp.cos(ang_r)
    sr_ref[...] = jnp.sin(ang_r)
    ang_a = pos * fa_ref[...]
    sa = jnp.sin(ang_a)
    d = lax.broadcasted_iota(jnp.int32, ang_a.shape, 1) & (ATT_HEAD_DIM - 1)
    half = ROPE_DIM // 2
    ca_ref[...] = jnp.cos(ang_a)
    s1_ref[...] = jnp.where(d < half, -sa, 0.0)
    s2_ref[...] = jnp.where((d >= half) & (d < ROPE_DIM), sa, 0.0)


def _prep(positions, x2):
    t, dm = x2.shape
    tm = 1024
    pos_b = jnp.broadcast_to(positions.reshape(t, 1), (t, LANES))
    inv_r = 1.0 / (RET_THETA ** jnp.linspace(0.0, 1.0, RET_DK // 2, dtype=F32))
    inv_a = 1.0 / (ROPE_THETA ** (jnp.arange(0, ROPE_DIM, 2, dtype=F32) / ROPE_DIM))
    d = jnp.arange(LANES) % ATT_HEAD_DIM
    fa = jnp.where(d < ROPE_DIM, inv_a[d % (ROPE_DIM // 2)], 0.0).astype(F32)
    row = pl.BlockSpec((tm, LANES), lambda i: (i, 0))
    xrow = pl.BlockSpec((tm, dm), lambda i: (i, 0))
    vec = pl.BlockSpec((1, LANES), lambda i: (0, 0))
    tab = jax.ShapeDtypeStruct((t, LANES), F32)
    return pl.pallas_call(
        _prep_kernel,
        grid=(t // tm,),
        in_specs=[row, vec, vec, xrow],
        out_specs=[row] * 5 + [xrow],
        out_shape=[tab] * 5 + [jax.ShapeDtypeStruct((t, dm), BF16)],
        compiler_params=_params(("parallel",), 40),
        name="rotary_tables_xcast",
    )(pos_b, inv_r.reshape(1, LANES), fa.reshape(1, LANES), x2)


def _ep_plain(acc, j, *tabs):
    return acc


def _ep_swish(acc, j, *tabs):
    return acc * _sigmoid(acc)


def _ep_sigmoid(acc, j, *tabs):
    return _sigmoid(acc)


def _ep_ret_rotary(acc, j, cos_ref, sin_ref, *att_tabs):
    scale = jnp.where(j == 0, RET_DK ** -0.5, 1.0)
    cos = cos_ref[...] * scale
    sin = sin_ref[...] * scale
    half = RET_DK // 2
    outs = []
    for c0 in range(0, acc.shape[1], RET_DK):
        a1 = acc[:, c0:c0 + half]
        a2 = acc[:, c0 + half:c0 + RET_DK]
        outs += [a1 * cos - a2 * sin, a2 * cos + a1 * sin]
    return jnp.concatenate(outs, axis=-1)


def _rope_lanes(a, ca, s1, s2):
    outs = []
    for c0 in range(0, a.shape[1], LANES):
        ac = a[:, c0:c0 + LANES]
        up = pltpu.roll(ac, LANES - ROPE_DIM // 2, 1)
        dn = pltpu.roll(ac, ROPE_DIM // 2, 1)
        outs.append(ac * ca + up * s1 + dn * s2)
    return jnp.concatenate(outs, axis=-1)


def _ep_att_q(acc, j, cos_ref, sin_ref, ca_ref, s1_ref, s2_ref):
    c = (ATT_HEAD_DIM ** -0.5) * LOG2E
    return _rope_lanes(acc, ca_ref[...] * c, s1_ref[...] * c, s2_ref[...] * c)


def _ep_att_kv(acc, j, ca_ref, s1_ref, s2_ref):
    k = _rope_lanes(acc[:, :KV_W], ca_ref[...], s1_ref[...], s2_ref[...])
    return jnp.concatenate([k, acc[:, KV_W:]], axis=-1)


def _proj_kernel(segments, n_tab, n_ride, x_ref, w_ref, *rest):
    tab_refs = rest[:n_tab]
    ride_refs = rest[n_tab:n_tab + n_ride]
    o_ref = rest[n_tab + n_ride]
    ride_out_refs = rest[n_tab + n_ride + 1:n_tab + 2 * n_ride + 1]
    wb_ref = rest[-1]
    j = pl.program_id(0)

    @pl.when(pl.program_id(1) == 0)
    def _cast_weights():
        wb_ref[...] = w_ref[...].astype(BF16)

    for j0, j1, epilogue, rides in segments:
        @pl.when((j >= j0) & (j < j1))
        def _segment():
            for r in rides:
                ride_out_refs[r][...] = ride_refs[r][...].astype(BF16)
            acc = jnp.dot(x_ref[...], wb_ref[...], preferred_element_type=F32)
            o_ref[...] = epilogue(acc, j - j0, *tab_refs).astype(o_ref.dtype)


def _proj_multi(xb, w, segs, tn, tables=(), rides=(), name="in_proj"):
    t, k = xb.shape
    tm = 1024
    nt = t // tm
    j0s, bounds = [], 0
    for col0, width, _ in segs:
        assert col0 % LANES == 0 and width % tn == 0
        j0s.append(bounds)
        bounds += width // tn
    n_tiles = bounds

    def w_col(j):
        col = segs[0][0] + j * tn
        for (col0, _, _), j0 in zip(segs[1:], j0s[1:]):
            col = jnp.where(j >= j0, col0 + (j - j0) * tn, col)
        return pl.multiple_of(col, LANES)

    tab = pl.BlockSpec((tm, LANES), lambda j, i: (i, 0))
    in_specs = [pl.BlockSpec((tm, k), lambda j, i: (i, 0)),
                pl.BlockSpec((pl.Element(k), pl.Element(tn)), lambda j, i: (0, w_col(j)))]
    in_specs += [tab] * len(tables)
    out_specs = [pl.BlockSpec((tm, tn), lambda j, i: (i, j))]
    out_shape = [jax.ShapeDtypeStruct((t, n_tiles * tn), BF16)]
    ride_steps = []
    for arr, first, n_slabs in rides:
        slab = arr.shape[0] // n_slabs
        assert slab * n_slabs == arr.shape[0] and slab % 16 == 0 and first + n_slabs <= n_tiles * nt
        spec = pl.BlockSpec(
            (slab, arr.shape[1]),
            lambda j, i, first=first, n=n_slabs: (jnp.clip(j * nt + i - first, 0, n - 1), 0))
        in_specs.append(spec)
        out_specs.append(spec)
        out_shape.append(jax.ShapeDtypeStruct(arr.shape, BF16))
        ride_steps.append((first, first + n_slabs))
    segments = []
    for (_, width, epilogue), j0 in zip(segs, j0s):
        j1 = j0 + width // tn
        active = tuple(r for r, (s0, s1) in enumerate(ride_steps) if s0 < j1 * nt and s1 > j0 * nt)
        segments.append((j0, j1, epilogue, active))
    outs = pl.pallas_call(
        functools.partial(_proj_kernel, tuple(segments), len(tables), len(rides)),
        grid=(n_tiles, nt),
        in_specs=in_specs,
        out_specs=out_specs,
        out_shape=out_shape,
        scratch_shapes=[pltpu.VMEM((k, tn), BF16)],
        compiler_params=_params(("arbitrary", "arbitrary"), 56),
        name=name,
    )(xb, w, *tables, *[arr for arr, _, _ in rides])
    return outs[0] if not rides else outs


def _retention_kernel(q_ref, k_ref, v_ref, gs_ref, ga_ref, gng_ref, gnb_ref,
                      o_ref, state_ref, dmask_ref, qdec_ref, kdec_ref, cdec_ref):
    c = RET_CHUNK

    @pl.when(pl.program_id(1) == 0)
    def _init():
        state_ref[...] = jnp.zeros_like(state_ref)
        ii = lax.broadcasted_iota(jnp.int32, (c, c), 0)
        jj = lax.broadcasted_iota(jnp.int32, (c, c), 1)
        diff = (ii - jj).astype(F32)
        idx = lax.broadcasted_iota(jnp.int32, (c, RET_DK), 0).astype(F32)
        for h in range(RET_HEADS):
            def log_gamma(shape):
                return jnp.log(1.0 - jnp.exp2(-5.0 - jnp.full(shape, float(h), F32)))

            dmask_ref[h] = jnp.where(
                diff >= 0.0, jnp.exp(log_gamma((c, c)) * jnp.maximum(diff, 0.0)), 0.0)
            lgl = log_gamma((c, RET_DK))
            qdec_ref[h] = jnp.exp(lgl * (idx + 1.0))
            kdec_ref[h] = jnp.exp(lgl * (c - 1.0 - idx))
            cdec_ref[h] = jnp.exp(log_gamma((8, LANES)) * float(c))

    for r0, h in ((r0, h) for r0 in range(0, q_ref.shape[0], c) for h in range(RET_HEADS)):
        rs = slice(r0, r0 + c)
        ksl = slice(h * RET_DK, (h + 1) * RET_DK)
        vsl = slice(h * RET_DV, (h + 1) * RET_DV)
        qb = q_ref[rs, ksl]
        kb = k_ref[rs, ksl]
        v = v_ref[rs, vsl]
        qd = (qb.astype(F32) * qdec_ref[h]).astype(BF16)
        kd = (kb.astype(F32) * kdec_ref[h]).astype(BF16)

        s = lax.dot_general(qb, kb, (((1,), (1,)), ((), ())), preferred_element_type=F32)
        s = s * dmask_ref[h]
        inner = jnp.dot(s.astype(BF16), v, preferred_element_type=F32)
        state = state_ref[h]
        cross = jnp.dot(qd, state.astype(BF16), preferred_element_type=F32)
        kv = lax.dot_general(kd, v, (((0,), (0,)), ((), ())), preferred_element_type=F32)
        state_ref[h] = state * cdec_ref[h, 0:1, 0:1] + kv

        y = inner + cross
        mu = jnp.mean(y, axis=-1, keepdims=True)
        d = y - mu
        var = jnp.mean(d * d, axis=-1, keepdims=True)
        yn = d * lax.rsqrt(var + GN_EPS) * gng_ref[:, vsl] + gnb_ref[:, vsl]
        gate = ga_ref[rs, vsl].astype(F32) * gs_ref[rs, vsl].astype(F32)
        o_ref[rs, vsl] = (gate * yn).astype(o_ref.dtype)


RET_STEP_CHUNKS = 2


def _retention(qa, vg, gn_g, gn_b, bsz, seq):
    t = bsz * seq
    c = RET_CHUNK
    rows_per_step = RET_STEP_CHUNKS * c
    nc = seq // rows_per_step
    nh = RET_HEADS

    def rows(width, col):
        return pl.BlockSpec((rows_per_step, width), lambda b, n: (b * nc + n, col))

    gn = pl.BlockSpec((1, D_MODEL), lambda b, n: (0, 0))
    return pl.pallas_call(
        _retention_kernel,
        grid=(bsz, nc),
        in_specs=[rows(nh * RET_DK, 0), rows(nh * RET_DK, 1), rows(D_MODEL, 0), rows(D_MODEL, 1),
                  rows(D_MODEL, 2), gn, gn],
        out_specs=rows(D_MODEL, 0),
        out_shape=jax.ShapeDtypeStruct((t, D_MODEL), BF16),
        scratch_shapes=[pltpu.VMEM((nh, RET_DK, RET_DV), F32), pltpu.VMEM((nh, c, c), F32),
                        pltpu.VMEM((nh, c, RET_DK), F32), pltpu.VMEM((nh, c, RET_DK), F32),
                        pltpu.VMEM((nh, 8, LANES), F32)],
        compiler_params=_params(("parallel", "arbitrary"), 32),
        name="retention",
    )(qa, qa, vg, vg, vg, gn_g.reshape(1, -1), gn_b.reshape(1, -1))


SINK_LANES = 8


def _pair_blocks(prev, cur, g):
    c0 = (g // 2) * LANES
    kk = jnp.concatenate([prev[:, c0:c0 + LANES], cur[:, c0:c0 + LANES]], axis=0).astype(F32)
    lane = lax.broadcasted_iota(jnp.int32, kk.shape, 1)
    row = lax.broadcasted_iota(jnp.int32, kk.shape, 0)
    hd = ATT_HEAD_DIM
    own = jnp.where(((lane >= hd) if g % 2 else (lane < hd)) & (row != 0), kk, 0.0)
    other = pltpu.roll(own, hd, 1)
    lo, hi = (other, own) if g % 2 else (own, other)
    return lo.astype(BF16), hi.astype(BF16)


def _attention_consts(sink_ref, qaug_ref, kaug_ref, vones_ref):
    nk = 2 * ATT_BLOCK
    row_q = lax.broadcasted_iota(jnp.int32, qaug_ref.shape, 0)
    lane_q = lax.broadcasted_iota(jnp.int32, qaug_ref.shape, 1)
    pair_of_lane = (lane_q & (SINK_LANES - 1)) >> 1
    onehot = (lane_q < 2 * SINK_LANES) & (pair_of_lane == row_q // ATT_BLOCK)
    qaug_ref[...] = jnp.where(onehot, 1.0, 0.0).astype(BF16)
    row_k = lax.broadcasted_iota(jnp.int32, vones_ref.shape, 0)
    lane_k = lax.broadcasted_iota(jnp.int32, vones_ref.shape, 1)
    vones_ref[...] = jnp.where((row_k < nk) == (lane_k < ATT_HEAD_DIM), 1.0, 0.0).astype(BF16)
    for g in range(ATT_KV_HEADS):
        vals = jnp.zeros(vones_ref.shape, F32)
        for j in range(ATT_GROUP):
            sk = sink_ref[g * ATT_GROUP + j] * LOG2E
            vals = jnp.where((lane_k == j) & (row_k == (j % 2) * nk), sk, vals)
        hi = vals.astype(BF16).astype(F32)
        kaug_ref[g] = (hi + pltpu.roll(vals - hi, SINK_LANES, 1)).astype(BF16)


def _attention_kernel(last, sink_ref, q_ref, kp_ref, kc_ref, vp_ref, vc_ref, gb_ref, ret_ref, o_ref,
                      qaug_ref, kaug_ref, vones_ref, *p_refs):
    i = pl.program_id(1)
    bq = ATT_BLOCK
    nk = 2 * bq
    pairs = ATT_GROUP // 2

    @pl.when((pl.program_id(0) == 0) & (i == 0))
    def _init():
        _attention_consts(sink_ref, qaug_ref, kaug_ref, vones_ref)

    def scores(slot):
        kp, kc = kp_ref[...], kc_ref[...]
        qi = lax.broadcasted_iota(jnp.int32, (bq, nk), 0)
        kj = lax.broadcasted_iota(jnp.int32, (bq, nk), 1)
        dist = qi + bq - kj
        kmin = jnp.where(i > 0, 0, bq)
        valid = ((dist >= 0) & (dist < WINDOW) & (kj >= kmin)) | (kj == 0)
        bias = jnp.where(valid, 0.0, -jnp.inf)
        qaug = qaug_ref[...]
        for g in range(ATT_KV_HEADS):
            k_lo, k_hi = _pair_blocks(kp, kc, g)
            kfull = jnp.concatenate([jnp.concatenate([k_lo, k_hi], axis=0), kaug_ref[g]], axis=1)
            col0 = g * ATT_GROUP * ATT_HEAD_DIM
            qp = jnp.concatenate(
                [q_ref[:, col0 + p * LANES:col0 + (p + 1) * LANES] for p in range(pairs)], axis=0)
            qfull = jnp.concatenate([qp, qaug], axis=1)
            s = lax.dot_general(qfull, kfull, (((1,), (1,)), ((), ())),
                                preferred_element_type=F32)
            for p in range(pairs):
                for e in range(2):
                    sj = s[p * bq:(p + 1) * bq, e * nk:(e + 1) * nk] + bias
                    m = jnp.max(sj, axis=-1, keepdims=True)
                    p_refs[slot][g, p * bq:(p + 1) * bq, e * nk:(e + 1) * nk] = (
                        jnp.exp2(sj - m).astype(BF16))

    def values(slot):
        vp, vc = vp_ref[...], vc_ref[...]
        vones = vones_ref[...]
        for g in range(ATT_KV_HEADS):
            v_lo, v_hi = _pair_blocks(vp, vc, g)
            vfull = jnp.concatenate([jnp.concatenate([v_lo, v_hi], axis=0), vones], axis=1)
            o = jnp.dot(p_refs[slot][g], vfull, preferred_element_type=F32)
            col0 = g * ATT_GROUP * ATT_HEAD_DIM
            for p in range(pairs):
                num = o[p * bq:(p + 1) * bq, :LANES]
                den = o[p * bq:(p + 1) * bq, LANES:]
                c1 = col0 + p * LANES
                gate = gb_ref[:, c1:c1 + LANES].astype(F32)
                ret = ret_ref[:, c1:c1 + LANES].astype(F32)
                o_ref[:, c1:c1 + LANES] = (gate * num / den + ret).astype(o_ref.dtype)

    @pl.when(i == 0)
    def _first():
        scores(0)

    for par in (0, 1):
        @pl.when((i > 0) & (i < last) & (lax.rem(i, 2) == par))
        def _steady():
            values(1 - par)
            scores(par)

    @pl.when(i == last)
    def _last():
        values((last - 1) % 2)


def _attention(qa, akv, vg, ret, sinks, bsz, seq):
    t = bsz * seq
    bq = ATT_BLOCK
    nb = seq // bq
    rows = (ATT_GROUP // 2) * bq

    def spec(width, col, lag, back):
        def index(b, i):
            blk = jnp.clip(i - lag, 0, nb - 1)
            return (b * nb + jnp.maximum(blk - back, 0), col)
        return pl.BlockSpec((bq, width), index)

    return pl.pallas_call(
        functools.partial(_attention_kernel, nb),
        grid=(bsz, nb + 1),
        in_specs=[pl.BlockSpec(memory_space=pltpu.SMEM), spec(D_MODEL, 1, 0, 0),
                  spec(KV_W, 0, 0, 1), spec(KV_W, 0, 0, 0),
                  spec(KV_W, 1, 1, 1), spec(KV_W, 1, 1, 0), spec(D_MODEL, 3, 1, 0),
                  spec(D_MODEL, 0, 1, 0)],
        out_specs=spec(D_MODEL, 0, 1, 0),
        out_shape=jax.ShapeDtypeStruct((t, D_MODEL), BF16),
        scratch_shapes=[pltpu.VMEM((rows, LANES), BF16),
                        pltpu.VMEM((ATT_KV_HEADS, 2 * 2 * bq, LANES), BF16),
                        pltpu.VMEM((2 * 2 * bq, LANES), BF16),
                        pltpu.VMEM((ATT_KV_HEADS, rows, 2 * 2 * bq), BF16),
                        pltpu.VMEM((ATT_KV_HEADS, rows, 2 * 2 * bq), BF16)],
        compiler_params=_params(("arbitrary", "arbitrary"), 32),
        name="attention",
    )(sinks, qa, akv, akv, akv, akv, vg, ret)


OUT_PROJ_ROW_CHUNKS = 2


def _out_proj_kernel(alpha, m_ref, x_ref, w_ref, g_ref, b_ref, h_ref, hb_ref):
    rows = h_ref.shape[0] // OUT_PROJ_ROW_CHUNKS
    for c in range(OUT_PROJ_ROW_CHUNKS):
        rs = slice(c * rows, (c + 1) * rows)
        y = jnp.dot(m_ref[rs, :], w_ref[...], preferred_element_type=F32)
        h = _layer_norm(alpha * x_ref[rs, :] + y, g_ref[...], b_ref[...])
        h_ref[rs, :] = h
        hb_ref[rs, :] = h.astype(BF16)


def _out_proj(merged, x2, wb, ln_g, ln_b, alpha):
    t, d = x2.shape
    tm = 512
    row = pl.BlockSpec((tm, d), lambda i: (i, 0))
    vec = pl.BlockSpec((1, d), lambda i: (0, 0))
    return pl.pallas_call(
        functools.partial(_out_proj_kernel, alpha),
        grid=(t // tm,),
        in_specs=[row, row, pl.BlockSpec((d, d), lambda i: (0, 0)), vec, vec],
        out_specs=[row, row],
        out_shape=[jax.ShapeDtypeStruct((t, d), F32), jax.ShapeDtypeStruct((t, d), BF16)],
        compiler_params=_params(("parallel",), 48),
        name="out_proj_ln",
    )(merged, x2, wb, ln_g.reshape(1, d), ln_b.reshape(1, d))


def _ffn_kernel(alpha, hb_ref, h_hbm, wg_ref, wu_ref, wd_ref, g_ref, b_ref, o_ref, hres_ref, sem):
    i = pl.program_id(0)
    f = pl.program_id(1)
    last = pl.num_programs(1) - 1
    tm = o_ref.shape[0]
    res_copy = pltpu.make_async_copy(h_hbm.at[pl.ds(i * tm, tm), :], hres_ref, sem)

    def tile():
        hb = hb_ref[...]
        gt = jnp.dot(hb, wg_ref[...], preferred_element_type=F32)
        ut = jnp.dot(hb, wu_ref[...], preferred_element_type=F32)
        a = (gt * _sigmoid(gt) * ut).astype(BF16)
        return jnp.dot(a, wd_ref[...], preferred_element_type=F32)

    @pl.when(f == 0)
    def _first():
        res_copy.start()
        o_ref[...] = tile()

    @pl.when((f > 0) & (f < last))
    def _middle():
        o_ref[...] += tile()

    @pl.when(f == last)
    def _last():
        res_copy.wait()
        z = alpha * hres_ref[...] + (o_ref[...] + tile())
        o_ref[...] = _layer_norm(z, g_ref[...], b_ref[...])


def _ffn(h1, h1b, wg, wu, wd, ln_g, ln_b, alpha):
    t, d = h1.shape
    dff = wg.shape[1]
    tm, tf = 1024, 512
    row = pl.BlockSpec((tm, d), lambda i, f: (i, 0))
    vec = pl.BlockSpec((1, d), lambda i, f: (0, 0))
    return pl.pallas_call(
        functools.partial(_ffn_kernel, alpha),
        grid=(t // tm, dff // tf),
        in_specs=[row,
                  pl.BlockSpec(memory_space=pl.ANY),
                  pl.BlockSpec((d, tf), lambda i, f: (0, f)),
                  pl.BlockSpec((d, tf), lambda i, f: (0, f)),
                  pl.BlockSpec((tf, d), lambda i, f: (f, 0)),
                  vec, vec],
        out_specs=row,
        out_shape=jax.ShapeDtypeStruct((t, d), F32),
        scratch_shapes=[pltpu.VMEM((tm, d), F32), pltpu.SemaphoreType.DMA(())],
        compiler_params=_params(("parallel", "arbitrary"), 60),
        name="ffn_ln",
    )(h1b, h1, wg, wu, wd, ln_g.reshape(1, d), ln_b.reshape(1, d))


def kernel(x, positions, w_in, ret_gn_g, ret_gn_b, att_sinks, w_out, ln1_g, ln1_b,
           w_gate, w_up, w_down, ln2_g, ln2_b):
    bsz, seq, d = x.shape
    depth = w_in.shape[0]
    alpha = (2.0 * depth) ** 0.25
    h = x.reshape(bsz * seq, d)
    cos_r, sin_r, ca, s1, s2, hb = _prep(positions, h)
    for l in range(depth):
        if l > 0:
            hb = h.astype(BF16)
        w = w_in[l]
        steps = 4 * (bsz * seq // 1024)
        qa, wdb = _proj_multi(
            hb, w, [(OFF_RQ, OFF_RV - OFF_RQ, _ep_ret_rotary), (OFF_AQ, D_MODEL, _ep_att_q)], 1024,
            (cos_r, sin_r, ca, s1, s2), [(w_down[l], 0, steps)], "in_proj_qk")
        vg, wgb, wub = _proj_multi(
            hb, w, [(OFF_RV, D_MODEL, _ep_plain), (OFF_RG, D_MODEL, _ep_swish),
                    (OFF_GA, 2 * D_MODEL, _ep_sigmoid)], 1024,
            (), [(w_gate[l], 0, steps), (w_up[l], steps, steps)], "in_proj_vg")
        akv, wob = _proj_multi(hb, w, [(OFF_AK, 2 * KV_W, _ep_att_kv)], 2 * KV_W, (ca, s1, s2),
                               [(w_out[l], 0, steps // 4)], "in_proj_att_kv")
        ret = _retention(qa, vg, ret_gn_g[l], ret_gn_b[l], bsz, seq)
        merged = _attention(qa, akv, vg, ret, att_sinks[l], bsz, seq)
        h, hb = _out_proj(merged, h, wob, ln1_g[l], ln1_b[l], alpha)
        h = _ffn(h, hb, wgb, wub, wdb, ln2_g[l], ln2_b[l], alpha)
    return h.reshape(bsz, seq, d)
```

```python
import functools
import math

import jax
import jax.numpy as jnp
from jax import lax
from jax.experimental import pallas as pl
from jax.experimental.pallas import tpu as pltpu

F32 = jnp.float32
BF16 = jnp.bfloat16

D_MODEL = 2048
RET_HEADS = 4
RET_DV = D_MODEL // RET_HEADS
RET_DK = RET_DV // 2
RET_CHUNK = 256
RET_THETA = 10000.0
ATT_HEAD_DIM = 64
ATT_HEADS = D_MODEL // ATT_HEAD_DIM
ATT_KV_HEADS = ATT_HEADS // 8
ATT_GROUP = ATT_HEADS // ATT_KV_HEADS
WINDOW = 128
ATT_BLOCK = WINDOW
ROPE_THETA = 500000.0
ROPE_DIM = ATT_HEAD_DIM // 4
LN_EPS = 1e-5
GN_EPS = 1e-5

LANES = 128
LOG2E = math.log2(math.e)

OFF_RQ = 0
OFF_RK = OFF_RQ + RET_HEADS * RET_DK
OFF_RV = OFF_RK + RET_HEADS * RET_DK
OFF_RG = OFF_RV + RET_HEADS * RET_DV
OFF_AQ = OFF_RG + RET_HEADS * RET_DV
OFF_AK = OFF_AQ + ATT_HEADS * ATT_HEAD_DIM
OFF_AV = OFF_AK + ATT_KV_HEADS * ATT_HEAD_DIM
OFF_GA = OFF_AV + ATT_KV_HEADS * ATT_HEAD_DIM
OFF_GB = OFF_GA + D_MODEL
KV_W = ATT_KV_HEADS * ATT_HEAD_DIM

MIB = 1024 * 1024


def _params(sem, vmem_mib, flags=None):
    return pltpu.CompilerParams(dimension_semantics=sem, vmem_limit_bytes=vmem_mib * MIB,
                                flags=flags)


def _sigmoid(v):
    return 0.5 * jnp.tanh(0.5 * v) + 0.5


def _layer_norm(z, g, b):
    mu = jnp.mean(z, axis=-1, keepdims=True)
    d = z - mu
    var = jnp.mean(d * d, axis=-1, keepdims=True)
    return d * lax.rsqrt(var + LN_EPS) * g + b


def _prep_kernel(pos_ref, fr_ref, fa_ref, x_ref, cr_ref, sr_ref, ca_ref, s1_ref, s2_ref, xb_ref):
    xb_ref[...] = x_ref[...].astype(BF16)
    pos = pos_ref[...].astype(F32)
    ang_r = pos * fr_ref[...]
    cr_ref[...] = jnp.cos(ang_r)
    sr_ref[...] = jnp.sin(ang_r)
    ang_a = pos * fa_ref[...]
    sa = jnp.sin(ang_a)
    d = lax.broadcasted_iota(jnp.int32, ang_a.shape, 1) & (ATT_HEAD_DIM - 1)
    half = ROPE_DIM // 2
    ca_ref[...] = jnp.cos(ang_a)
    s1_ref[...] = jnp.where(d < half, -sa, 0.0)
    s2_ref[...] = jnp.where((d >= half) & (d < ROPE_DIM), sa, 0.0)


def _prep(positions, x2):
    t, dm = x2.shape
    tm = 1024
    pos_b = jnp.broadcast_to(positions.reshape(t, 1), (t, LANES))
    inv_r = 1.0 / (RET_THETA ** jnp.linspace(0.0, 1.0, RET_DK // 2, dtype=F32))
    inv_a = 1.0 / (ROPE_THETA ** (jnp.arange(0, ROPE_DIM, 2, dtype=F32) / ROPE_DIM))
    d = jnp.arange(LANES) % ATT_HEAD_DIM
    fa = jnp.where(d < ROPE_DIM, jnp.tile(inv_a, LANES // inv_a.size), 0.0).astype(F32)
    row = pl.BlockSpec((tm, LANES), lambda i: (i, 0))
    xrow = pl.BlockSpec((tm, dm), lambda i: (i, 0))
    vec = pl.BlockSpec((1, LANES), lambda i: (0, 0))
    tab = jax.ShapeDtypeStruct((t, LANES), F32)
    return pl.pallas_call(
        _prep_kernel,
        grid=(t // tm,),
        in_specs=[row, vec, vec, xrow],
        out_specs=[row] * 5 + [xrow],
        out_shape=[tab] * 5 + [jax.ShapeDtypeStruct((t, dm), BF16)],
        compiler_params=_params(("parallel",), 40),
        name="rotary_tables_xcast",
    )(pos_b, inv_r.reshape(1, LANES), fa.reshape(1, LANES), x2)


def _ep_plain(acc, j, *tabs):
    return acc


def _ep_swish(acc, j, *tabs):
    return acc * _sigmoid(acc)


def _ep_sigmoid(acc, j, *tabs):
    return _sigmoid(acc)


def _ep_ret_rotary(acc, j, cos_ref, sin_ref, *att_tabs):
    scale = jnp.where(j == 0, RET_DK ** -0.5, 1.0)
    cos = cos_ref[...] * scale
    sin = sin_ref[...] * scale
    half = RET_DK // 2
    outs = []
    for c0 in range(0, acc.shape[1], RET_DK):
        a1 = acc[:, c0:c0 + half]
        a2 = acc[:, c0 + half:c0 + RET_DK]
        outs += [a1 * cos - a2 * sin, a2 * cos + a1 * sin]
    return jnp.concatenate(outs, axis=-1)


def _rope_lanes(a, ca, s1, s2):
    outs = []
    for c0 in range(0, a.shape[1], LANES):
        ac = a[:, c0:c0 + LANES]
        up = pltpu.roll(ac, LANES - ROPE_DIM // 2, 1)
        dn = pltpu.roll(ac, ROPE_DIM // 2, 1)
        outs.append(ac * ca + up * s1 + dn * s2)
    return jnp.concatenate(outs, axis=-1)


def _ep_att_q(acc, j, cos_ref, sin_ref, ca_ref, s1_ref, s2_ref):
    c = (ATT_HEAD_DIM ** -0.5) * LOG2E
    return _rope_lanes(acc, ca_ref[...] * c, s1_ref[...] * c, s2_ref[...] * c)


def _ep_att_kv(acc, j, ca_ref, s1_ref, s2_ref):
    k = _rope_lanes(acc[:, :KV_W], ca_ref[...], s1_ref[...], s2_ref[...])
    return jnp.concatenate([k, acc[:, KV_W:]], axis=-1)


def _proj_kernel(segments, n_tab, n_ride, x_ref, w_ref, *rest):
    tab_refs = rest[:n_tab]
    ride_refs = rest[n_tab:n_tab + n_ride]
    o_ref = rest[n_tab + n_ride]
    ride_out_refs = rest[n_tab + n_ride + 1:n_tab + 2 * n_ride + 1]
    wb_ref = rest[-1]
    j = pl.program_id(0)

    @pl.when(pl.program_id(1) == 0)
    def _cast_weights():
        wb_ref[...] = w_ref[...].astype(BF16)

    for j0, j1, epilogue, rides in segments:
        @pl.when((j >= j0) & (j < j1))
        def _segment():
            for r in rides:
                ride_out_refs[r][...] = ride_refs[r][...].astype(BF16)
            acc = jnp.dot(x_ref[...], wb_ref[...], preferred_element_type=F32)
            o_ref[...] = epilogue(acc, j - j0, *tab_refs).astype(o_ref.dtype)


def _proj_multi(xb, w, segs, tn, tables=(), rides=(), name="in_proj"):
    t, k = xb.shape
    tm = 1024
    nt = t // tm
    j0s, bounds = [], 0
    for col0, width, _ in segs:
        assert col0 % LANES == 0 and width % tn == 0
        j0s.append(bounds)
        bounds += width // tn
    n_tiles = bounds

    def w_col(j):
        col = segs[0][0] + j * tn
        for (col0, _, _), j0 in zip(segs[1:], j0s[1:]):
            col = jnp.where(j >= j0, col0 + (j - j0) * tn, col)
        return pl.multiple_of(col, LANES)

    tab = pl.BlockSpec((tm, LANES), lambda j, i: (i, 0))
    in_specs = [pl.BlockSpec((tm, k), lambda j, i: (i, 0)),
                pl.BlockSpec((pl.Element(k), pl.Element(tn)), lambda j, i: (0, w_col(j)))]
    in_specs += [tab] * len(tables)
    out_specs = [pl.BlockSpec((tm, tn), lambda j, i: (i, j))]
    out_shape = [jax.ShapeDtypeStruct((t, n_tiles * tn), BF16)]
    ride_steps = []
    for arr, first, n_slabs in rides:
        slab = arr.shape[0] // n_slabs
        assert slab * n_slabs == arr.shape[0] and slab % 16 == 0 and first + n_slabs <= n_tiles * nt
        spec = pl.BlockSpec(
            (slab, arr.shape[1]),
            lambda j, i, first=first, n=n_slabs: (jnp.clip(j * nt + i - first, 0, n - 1), 0))
        in_specs.append(spec)
        out_specs.append(spec)
        out_shape.append(jax.ShapeDtypeStruct(arr.shape, BF16))
        ride_steps.append((first, first + n_slabs))
    segments = []
    for (_, width, epilogue), j0 in zip(segs, j0s):
        j1 = j0 + width // tn
        active = tuple(r for r, (s0, s1) in enumerate(ride_steps) if s0 < j1 * nt and s1 > j0 * nt)
        segments.append((j0, j1, epilogue, active))
    outs = pl.pallas_call(
        functools.partial(_proj_kernel, tuple(segments), len(tables), len(rides)),
        grid=(n_tiles, nt),
        in_specs=in_specs,
        out_specs=out_specs,
        out_shape=out_shape,
        scratch_shapes=[pltpu.VMEM((k, tn), BF16)],
        compiler_params=_params(("arbitrary", "arbitrary"), 56),
        name=name,
    )(xb, w, *tables, *[arr for arr, _, _ in rides])
    return outs[0] if not rides else outs


def _retention_kernel(q_ref, k_ref, v_ref, gs_ref, ga_ref, gng_ref, gnb_ref,
                      o_ref, state_ref, dmask_ref, qdec_ref, kdec_ref, cdec_ref):
    c = RET_CHUNK

    @pl.when(pl.program_id(1) == 0)
    def _init():
        state_ref[...] = jnp.zeros_like(state_ref)
        ii = lax.broadcasted_iota(jnp.int32, (c, c), 0)
        jj = lax.broadcasted_iota(jnp.int32, (c, c), 1)
        diff = (ii - jj).astype(F32)
        idx = lax.broadcasted_iota(jnp.int32, (c, RET_DK), 0).astype(F32)
        for h in range(RET_HEADS):
            def log_gamma(shape):
                return jnp.log(1.0 - jnp.exp2(-5.0 - jnp.full(shape, float(h), F32)))

            dmask_ref[h] = jnp.where(
                diff >= 0.0, jnp.exp(log_gamma((c, c)) * jnp.maximum(diff, 0.0)), 0.0)
            lgl = log_gamma((c, RET_DK))
            qdec_ref[h] = jnp.exp(lgl * (idx + 1.0))
            kdec_ref[h] = jnp.exp(lgl * (c - 1.0 - idx))
            cdec_ref[h] = jnp.exp(log_gamma((8, LANES)) * float(c))

    for r0, h in ((r0, h) for r0 in range(0, q_ref.shape[0], c) for h in range(RET_HEADS)):
        rs = slice(r0, r0 + c)
        ksl = slice(h * RET_DK, (h + 1) * RET_DK)
        vsl = slice(h * RET_DV, (h + 1) * RET_DV)
        qb = q_ref[rs, ksl]
        kb = k_ref[rs, ksl]
        v = v_ref[rs, vsl]
        qd = (qb.astype(F32) * qdec_ref[h]).astype(BF16)
        kd = (kb.astype(F32) * kdec_ref[h]).astype(BF16)

        s = lax.dot_general(qb, kb, (((1,), (1,)), ((), ())), preferred_element_type=F32)
        s = s * dmask_ref[h]
        inner = jnp.dot(s.astype(BF16), v, preferred_element_type=F32)
        state = state_ref[h]
        cross = jnp.dot(qd, state.astype(BF16), preferred_element_type=F32)
        kv = lax.dot_general(kd, v, (((0,), (0,)), ((), ())), preferred_element_type=F32)
        state_ref[h] = state * cdec_ref[h, 0:1, 0:1] + kv

        y = inner + cross
        mu = jnp.mean(y, axis=-1, keepdims=True)
        d = y - mu
        var = jnp.mean(d * d, axis=-1, keepdims=True)
        yn = d * lax.rsqrt(var + GN_EPS) * gng_ref[:, vsl] + gnb_ref[:, vsl]
        gate = ga_ref[rs, vsl].astype(F32) * gs_ref[rs, vsl].astype(F32)
        o_ref[rs, vsl] = (gate * yn).astype(o_ref.dtype)


RET_STEP_CHUNKS = 4


def _retention(qa, vg, gn_g, gn_b, bsz, seq):
    t = bsz * seq
    c = RET_CHUNK
    rows_per_step = RET_STEP_CHUNKS * c
    nc = seq // rows_per_step
    nh = RET_HEADS

    def rows(width, col):
        return pl.BlockSpec((rows_per_step, width), lambda b, n: (b * nc + n, col))

    gn = pl.BlockSpec((1, D_MODEL), lambda b, n: (0, 0))
    return pl.pallas_call(
        _retention_kernel,
        grid=(bsz, nc),
        in_specs=[rows(nh * RET_DK, 0), rows(nh * RET_DK, 1), rows(D_MODEL, 0), rows(D_MODEL, 1),
                  rows(D_MODEL, 2), gn, gn],
        out_specs=rows(D_MODEL, 0),
        out_shape=jax.ShapeDtypeStruct((t, D_MODEL), BF16),
        scratch_shapes=[pltpu.VMEM((nh, RET_DK, RET_DV), F32), pltpu.VMEM((nh, c, c), F32),
                        pltpu.VMEM((nh, c, RET_DK), F32), pltpu.VMEM((nh, c, RET_DK), F32),
                        pltpu.VMEM((nh, 8, LANES), F32)],
        compiler_params=_params(("parallel", "arbitrary"), 56),
        name="retention",
    )(qa, qa, vg, vg, vg, gn_g.reshape(1, -1), gn_b.reshape(1, -1))


SINK_LANES = 8


def _pair_blocks(prev, cur, g):
    c0 = (g // 2) * LANES
    kk = jnp.concatenate([prev[:, c0:c0 + LANES], cur[:, c0:c0 + LANES]], axis=0).astype(F32)
    lane = lax.broadcasted_iota(jnp.int32, kk.shape, 1)
    row = lax.broadcasted_iota(jnp.int32, kk.shape, 0)
    hd = ATT_HEAD_DIM
    own = jnp.where(((lane >= hd) if g % 2 else (lane < hd)) & (row != 0), kk, 0.0)
    other = pltpu.roll(own, hd, 1)
    lo, hi = (other, own) if g % 2 else (own, other)
    return lo.astype(BF16), hi.astype(BF16)


def _attention_consts(sink_ref, qaug_ref, kaug_ref, vones_ref):
    nk = 2 * ATT_BLOCK
    row_q = lax.broadcasted_iota(jnp.int32, qaug_ref.shape, 0)
    lane_q = lax.broadcasted_iota(jnp.int32, qaug_ref.shape, 1)
    pair_of_lane = (lane_q & (SINK_LANES - 1)) >> 1
    onehot = (lane_q < 2 * SINK_LANES) & (pair_of_lane == row_q // ATT_BLOCK)
    qaug_ref[...] = jnp.where(onehot, 1.0, 0.0).astype(BF16)
    row_k = lax.broadcasted_iota(jnp.int32, vones_ref.shape, 0)
    lane_k = lax.broadcasted_iota(jnp.int32, vones_ref.shape, 1)
    vones_ref[...] = jnp.where((row_k < nk) == (lane_k < ATT_HEAD_DIM), 1.0, 0.0).astype(BF16)
    for g in range(ATT_KV_HEADS):
        vals = jnp.zeros(vones_ref.shape, F32)
        for j in range(ATT_GROUP):
            sk = sink_ref[g * ATT_GROUP + j] * LOG2E
            vals = jnp.where((lane_k == j) & (row_k == (j % 2) * nk), sk, vals)
        hi = vals.astype(BF16).astype(F32)
        kaug_ref[g] = (hi + pltpu.roll(vals - hi, SINK_LANES, 1)).astype(BF16)


def _attention_kernel(last, sink_ref, q_ref, kp_ref, kc_ref, vp_ref, vc_ref, gb_ref, ret_ref, o_ref,
                      qaug_ref, kaug_ref, vones_ref, *p_refs):
    i = pl.program_id(1)
    bq = ATT_BLOCK
    nk = 2 * bq
    pairs = ATT_GROUP // 2

    @pl.when((pl.program_id(0) == 0) & (i == 0))
    def _init():
        _attention_consts(sink_ref, qaug_ref, kaug_ref, vones_ref)

    def scores(slot):
        kp, kc = kp_ref[...], kc_ref[...]
        qi = lax.broadcasted_iota(jnp.int32, (bq, nk), 0)
        kj = lax.broadcasted_iota(jnp.int32, (bq, nk), 1)
        dist = qi + bq - kj
        kmin = jnp.where(i > 0, 0, bq)
        valid = ((dist >= 0) & (dist < WINDOW) & (kj >= kmin)) | (kj == 0)
        bias = jnp.where(valid, 0.0, -jnp.inf)
        qaug = qaug_ref[...]
        for g in range(ATT_KV_HEADS):
            k_lo, k_hi = _pair_blocks(kp, kc, g)
            kfull = jnp.concatenate([jnp.concatenate([k_lo, k_hi], axis=0), kaug_ref[g]], axis=1)
            col0 = g * ATT_GROUP * ATT_HEAD_DIM
            qp = jnp.concatenate(
                [q_ref[:, col0 + p * LANES:col0 + (p + 1) * LANES] for p in range(pairs)], axis=0)
            qfull = jnp.concatenate([qp, qaug], axis=1)
            s = lax.dot_general(qfull, kfull, (((1,), (1,)), ((), ())),
                                preferred_element_type=F32)
            for p in range(pairs):
                for e in range(2):
                    sj = s[p * bq:(p + 1) * bq, e * nk:(e + 1) * nk] + bias
                    m = jnp.max(sj, axis=-1, keepdims=True)
                    p_refs[slot][g, p * bq:(p + 1) * bq, e * nk:(e + 1) * nk] = (
                        jnp.exp2(sj - m).astype(BF16))

    def values(slot):
        vp, vc = vp_ref[...], vc_ref[...]
        vones = vones_ref[...]
        for g in range(ATT_KV_HEADS):
            v_lo, v_hi = _pair_blocks(vp, vc, g)
            vfull = jnp.concatenate([jnp.concatenate([v_lo, v_hi], axis=0), vones], axis=1)
            o = jnp.dot(p_refs[slot][g], vfull, preferred_element_type=F32)
            col0 = g * ATT_GROUP * ATT_HEAD_DIM
            for p in range(pairs):
                num = o[p * bq:(p + 1) * bq, :LANES]
                den = o[p * bq:(p + 1) * bq, LANES:]
                c1 = col0 + p * LANES
                gate = gb_ref[:, c1:c1 + LANES].astype(F32)
                ret = ret_ref[:, c1:c1 + LANES].astype(F32)
                o_ref[:, c1:c1 + LANES] = (gate * num / den + ret).astype(o_ref.dtype)

    @pl.when(i == 0)
    def _first():
        scores(0)

    for par in (0, 1):
        @pl.when((i > 0) & (i < last) & (lax.rem(i, 2) == par))
        def _steady():
            values(1 - par)
            scores(par)

    @pl.when(i == last)
    def _last():
        values((last - 1) % 2)


def _attention(qa, akv, vg, ret, sinks, bsz, seq):
    t = bsz * seq
    bq = ATT_BLOCK
    nb = seq // bq
    rows = (ATT_GROUP // 2) * bq

    def spec(width, col, lag, back):
        def index(b, i):
            blk = jnp.clip(i - lag, 0, nb - 1)
            return (b * nb + jnp.maximum(blk - back, 0), col)
        return pl.BlockSpec((bq, width), index)

    return pl.pallas_call(
        functools.partial(_attention_kernel, nb),
        grid=(bsz, nb + 1),
        in_specs=[pl.BlockSpec(memory_space=pltpu.SMEM), spec(D_MODEL, 1, 0, 0),
                  spec(KV_W, 0, 0, 1), spec(KV_W, 0, 0, 0),
                  spec(KV_W, 1, 1, 1), spec(KV_W, 1, 1, 0), spec(D_MODEL, 3, 1, 0),
                  spec(D_MODEL, 0, 1, 0)],
        out_specs=spec(D_MODEL, 0, 1, 0),
        out_shape=jax.ShapeDtypeStruct((t, D_MODEL), BF16),
        scratch_shapes=[pltpu.VMEM((rows, LANES), BF16),
                        pltpu.VMEM((ATT_KV_HEADS, 2 * 2 * bq, LANES), BF16),
                        pltpu.VMEM((2 * 2 * bq, LANES), BF16),
                        pltpu.VMEM((ATT_KV_HEADS, rows, 2 * 2 * bq), BF16),
                        pltpu.VMEM((ATT_KV_HEADS, rows, 2 * 2 * bq), BF16)],
        compiler_params=_params(("arbitrary", "arbitrary"), 32),
        name="attention",
    )(sinks, qa, akv, akv, akv, akv, vg, ret)


OUT_PROJ_ROW_CHUNKS = 2


def _out_proj_kernel(alpha, m_ref, x_ref, w_ref, g_ref, b_ref, h_ref, hb_ref):
    rows = h_ref.shape[0] // OUT_PROJ_ROW_CHUNKS
    for c in range(OUT_PROJ_ROW_CHUNKS):
        rs = slice(c * rows, (c + 1) * rows)
        y = jnp.dot(m_ref[rs, :], w_ref[...], preferred_element_type=F32)
        h = _layer_norm(alpha * x_ref[rs, :] + y, g_ref[...], b_ref[...])
        h_ref[rs, :] = h
        hb_ref[rs, :] = h.astype(BF16)


def _out_proj(merged, x2, wb, ln_g, ln_b, alpha):
    t, d = x2.shape
    tm = 512
    row = pl.BlockSpec((tm, d), lambda i: (i, 0))
    vec = pl.BlockSpec((1, d), lambda i: (0, 0))
    return pl.pallas_call(
        functools.partial(_out_proj_kernel, alpha),
        grid=(t // tm,),
        in_specs=[row, row, pl.BlockSpec((d, d), lambda i: (0, 0)), vec, vec],
        out_specs=[row, row],
        out_shape=[jax.ShapeDtypeStruct((t, d), F32), jax.ShapeDtypeStruct((t, d), BF16)],
        compiler_params=_params(("parallel",), 48),
        name="out_proj_ln",
    )(merged, x2, wb, ln_g.reshape(1, d), ln_b.reshape(1, d))


def _ffn_kernel(alpha, hb_ref, h_hbm, wg_ref, wu_ref, wd_ref, g_ref, b_ref, o_ref, hres_ref, sem):
    i = pl.program_id(0)
    f = pl.program_id(1)
    last = pl.num_programs(1) - 1
    tm = o_ref.shape[0]
    res_copy = pltpu.make_async_copy(h_hbm.at[pl.ds(i * tm, tm), :], hres_ref, sem)

    def tile():
        hb = hb_ref[...]
        gt = jnp.dot(hb, wg_ref[...], preferred_element_type=F32)
        ut = jnp.dot(hb, wu_ref[...], preferred_element_type=F32)
        a = (gt * _sigmoid(gt) * ut).astype(BF16)
        return jnp.dot(a, wd_ref[...], preferred_element_type=F32)

    @pl.when(f == 0)
    def _first():
        res_copy.start()
        o_ref[...] = tile()

    @pl.when((f > 0) & (f < last))
    def _middle():
        o_ref[...] += tile()

    @pl.when(f == last)
    def _last():
        res_copy.wait()
        z = alpha * hres_ref[...] + (o_ref[...] + tile())
        o_ref[...] = _layer_norm(z, g_ref[...], b_ref[...])


def _ffn(h1, h1b, wg, wu, wd, ln_g, ln_b, alpha):
    t, d = h1.shape
    dff = wg.shape[1]
    tm, tf = 1024, 512
    row = pl.BlockSpec((tm, d), lambda i, f: (i, 0))
    vec = pl.BlockSpec((1, d), lambda i, f: (0, 0))
    return pl.pallas_call(
        functools.partial(_ffn_kernel, alpha),
        grid=(t // tm, dff // tf),
        in_specs=[row,
                  pl.BlockSpec(memory_space=pl.ANY),
                  pl.BlockSpec((d, tf), lambda i, f: (0, f)),
                  pl.BlockSpec((d, tf), lambda i, f: (0, f)),
                  pl.BlockSpec((tf, d), lambda i, f: (f, 0)),
                  vec, vec],
        out_specs=row,
        out_shape=jax.ShapeDtypeStruct((t, d), F32),
        scratch_shapes=[pltpu.VMEM((tm, d), F32), pltpu.SemaphoreType.DMA(())],
        compiler_params=_params(("parallel", "arbitrary"), 60),
        name="ffn_ln",
    )(h1b, h1, wg, wu, wd, ln_g.reshape(1, d), ln_b.reshape(1, d))


def kernel(x, positions, w_in, ret_gn_g, ret_gn_b, att_sinks, w_out, ln1_g, ln1_b,
           w_gate, w_up, w_down, ln2_g, ln2_b):
    bsz, seq, d = x.shape
    depth = w_in.shape[0]
    alpha = (2.0 * depth) ** 0.25
    h = x.reshape(bsz * seq, d)
    cos_r, sin_r, ca, s1, s2, hb = _prep(positions, h)
    for l in range(depth):
        if l > 0:
            hb = h.astype(BF16)
        w = w_in[l]
        steps = 4 * (bsz * seq // 1024)
        qa, wdb = _proj_multi(
            hb, w, [(OFF_RQ, OFF_RV - OFF_RQ, _ep_ret_rotary), (OFF_AQ, D_MODEL, _ep_att_q)], 1024,
            (cos_r, sin_r, ca, s1, s2), [(w_down[l], 0, steps)], "in_proj_qk")
        vg, wgb, wub = _proj_multi(
            hb, w, [(OFF_RV, D_MODEL, _ep_plain), (OFF_RG, D_MODEL, _ep_swish),
                    (OFF_GA, 2 * D_MODEL, _ep_sigmoid)], 1024,
            (), [(w_gate[l], 0, steps), (w_up[l], steps, steps)], "in_proj_vg")
        akv, wob = _proj_multi(hb, w, [(OFF_AK, 2 * KV_W, _ep_att_kv)], 2 * KV_W, (ca, s1, s2),
                               [(w_out[l], 0, steps // 4)], "in_proj_att_kv")
        ret = _retention(qa, vg, ret_gn_g[l], ret_gn_b[l], bsz, seq)
        merged = _attention(qa, akv, vg, ret, att_sinks[l], bsz, seq)
        h, hb = _out_proj(merged, h, wob, ln1_g[l], ln1_b[l], alpha)
        h = _ffn(h, hb, wgb, wub, wdb, ln2_g[l], ln2_b[l], alpha)
    return h.reshape(bsz, seq, d)
```

```python
import functools
import math

import jax
import jax.numpy as jnp
from jax import lax
from jax.experimental import pallas as pl
from jax.experimental.pallas import tpu as pltpu

F32 = jnp.float32
BF16 = jnp.bfloat16

D_MODEL = 2048
RET_HEADS = 4
RET_DV = D_MODEL // RET_HEADS
RET_DK = RET_DV // 2
RET_CHUNK = 256
RET_THETA = 10000.0
ATT_HEAD_DIM = 64
ATT_HEADS = D_MODEL // ATT_HEAD_DIM
ATT_KV_HEADS = ATT_HEADS // 8
ATT_GROUP = ATT_HEADS // ATT_KV_HEADS
WINDOW = 128
ATT_BLOCK = WINDOW
ROPE_THETA = 500000.0
ROPE_DIM = ATT_HEAD_DIM // 4
LN_EPS = 1e-5
GN_EPS = 1e-5

LANES = 128
LOG2E = math.log2(math.e)

OFF_RQ = 0
OFF_RK = OFF_RQ + RET_HEADS * RET_DK
OFF_RV = OFF_RK + RET_HEADS * RET_DK
OFF_RG = OFF_RV + RET_HEADS * RET_DV
OFF_AQ = OFF_RG + RET_HEADS * RET_DV
OFF_AK = OFF_AQ + ATT_HEADS * ATT_HEAD_DIM
OFF_AV = OFF_AK + ATT_KV_HEADS * ATT_HEAD_DIM
OFF_GA = OFF_AV + ATT_KV_HEADS * ATT_HEAD_DIM
OFF_GB = OFF_GA + D_MODEL
KV_W = ATT_KV_HEADS * ATT_HEAD_DIM

MIB = 1024 * 1024


def _params(sem, vmem_mib, flags=None):
    return pltpu.CompilerParams(dimension_semantics=sem, vmem_limit_bytes=vmem_mib * MIB,
                                flags=flags)


def _sigmoid(v):
    return 0.5 * jnp.tanh(0.5 * v) + 0.5


def _layer_norm(z, g, b):
    mu = jnp.mean(z, axis=-1, keepdims=True)
    d = z - mu
    var = jnp.mean(d * d, axis=-1, keepdims=True)
    return d * lax.rsqrt(var + LN_EPS) * g + b


def _prep_kernel(pos_ref, fr_ref, fa_ref, x_ref, cr_ref, sr_ref, ca_ref, s1_ref, s2_ref, xb_ref):
    xb_ref[...] = x_ref[...].astype(BF16)
    pos_t = pos_ref[...].astype(F32).T
    pos = jnp.concatenate([jnp.broadcast_to(pos_t[:, c:c + 1], (LANES, LANES))
                           for c in range(pos_t.shape[1])], axis=0)
    ang_r = pos * fr_ref[...]
    cr_ref[...] = jnp.cos(ang_r)
    sr_ref[...] = jnp.sin(ang_r)
    ang_a = pos * fa_ref[...]
    sa = jnp.sin(ang_a)
    d = lax.broadcasted_iota(jnp.int32, ang_a.shape, 1) & (ATT_HEAD_DIM - 1)
    half = ROPE_DIM // 2
    ca_ref[...] = jnp.cos(ang_a)
    s1_ref[...] = jnp.where(d < half, -sa, 0.0)
    s2_ref[...] = jnp.where((d >= half) & (d < ROPE_DIM), sa, 0.0)


def _prep(positions, x2):
    t, dm = x2.shape
    tm = 1024
    pos_l = positions.reshape(t // LANES, LANES)
    inv_r = 1.0 / (RET_THETA ** jnp.linspace(0.0, 1.0, RET_DK // 2, dtype=F32))
    inv_a = 1.0 / (ROPE_THETA ** (jnp.arange(0, ROPE_DIM, 2, dtype=F32) / ROPE_DIM))
    d = jnp.arange(LANES) % ATT_HEAD_DIM
    fa = jnp.where(d < ROPE_DIM, jnp.tile(inv_a, LANES // inv_a.size), 0.0).astype(F32)
    row = pl.BlockSpec((tm, LANES), lambda i: (i, 0))
    xrow = pl.BlockSpec((tm, dm), lambda i: (i, 0))
    vec = pl.BlockSpec((1, LANES), lambda i: (0, 0))
    tab = jax.ShapeDtypeStruct((t, LANES), F32)
    return pl.pallas_call(
        _prep_kernel,
        grid=(t // tm,),
        in_specs=[pl.BlockSpec((tm // LANES, LANES), lambda i: (i, 0)), vec, vec, xrow],
        out_specs=[row] * 5 + [xrow],
        out_shape=[tab] * 5 + [jax.ShapeDtypeStruct((t, dm), BF16)],
        compiler_params=_params(("parallel",), 60),
        name="rotary_tables_xcast",
    )(pos_l, inv_r.reshape(1, LANES), fa.reshape(1, LANES), x2)


def _ep_plain(acc, j, *tabs):
    return acc


def _ep_swish(acc, j, *tabs):
    return acc * _sigmoid(acc)


def _ep_sigmoid(acc, j, *tabs):
    return _sigmoid(acc)


def _ep_ret_rotary(acc, j, cos_ref, sin_ref, *att_tabs):
    scale = jnp.where(j == 0, RET_DK ** -0.5, 1.0)
    cos = cos_ref[...] * scale
    sin = sin_ref[...] * scale
    half = RET_DK // 2
    outs = []
    for c0 in range(0, acc.shape[1], RET_DK):
        a1 = acc[:, c0:c0 + half]
        a2 = acc[:, c0 + half:c0 + RET_DK]
        outs += [a1 * cos - a2 * sin, a2 * cos + a1 * sin]
    return jnp.concatenate(outs, axis=-1)


def _rope_lanes(a, ca, s1, s2):
    outs = []
    for c0 in range(0, a.shape[1], LANES):
        ac = a[:, c0:c0 + LANES]
        up = pltpu.roll(ac, LANES - ROPE_DIM // 2, 1)
        dn = pltpu.roll(ac, ROPE_DIM // 2, 1)
        outs.append(ac * ca + up * s1 + dn * s2)
    return jnp.concatenate(outs, axis=-1)


def _ep_att_q(acc, j, cos_ref, sin_ref, ca_ref, s1_ref, s2_ref):
    c = (ATT_HEAD_DIM ** -0.5) * LOG2E
    return _rope_lanes(acc, ca_ref[...] * c, s1_ref[...] * c, s2_ref[...] * c)


def _ep_att_kv(acc, j, ca_ref, s1_ref, s2_ref):
    k = _rope_lanes(acc[:, :KV_W], ca_ref[...], s1_ref[...], s2_ref[...])
    return jnp.concatenate([k, acc[:, KV_W:]], axis=-1)


def _proj_kernel(segments, n_tab, n_ride, x_ref, w_ref, *rest):
    tab_refs = rest[:n_tab]
    ride_refs = rest[n_tab:n_tab + n_ride]
    o_ref = rest[n_tab + n_ride]
    ride_out_refs = rest[n_tab + n_ride + 1:n_tab + 2 * n_ride + 1]
    wb_ref = rest[-1]
    j = pl.program_id(0)

    @pl.when(pl.program_id(1) == 0)
    def _cast_weights():
        wb_ref[...] = w_ref[...].astype(BF16)

    for j0, j1, epilogue, rides in segments:
        @pl.when((j >= j0) & (j < j1))
        def _segment():
            for r in rides:
                ride_out_refs[r][...] = ride_refs[r][...].astype(BF16)
            acc = jnp.dot(x_ref[...], wb_ref[...], preferred_element_type=F32)
            o_ref[...] = epilogue(acc, j - j0, *tab_refs).astype(o_ref.dtype)


def _proj_multi(xb, w, segs, tn, tables=(), rides=(), name="in_proj"):
    t, k = xb.shape
    tm = 1024
    nt = t // tm
    j0s, bounds = [], 0
    for col0, width, _ in segs:
        assert col0 % LANES == 0 and width % tn == 0
        j0s.append(bounds)
        bounds += width // tn
    n_tiles = bounds

    def w_col(j):
        col = segs[0][0] + j * tn
        for (col0, _, _), j0 in zip(segs[1:], j0s[1:]):
            col = jnp.where(j >= j0, col0 + (j - j0) * tn, col)
        return pl.multiple_of(col, LANES)

    tab = pl.BlockSpec((tm, LANES), lambda j, i: (i, 0))
    in_specs = [pl.BlockSpec((tm, k), lambda j, i: (i, 0)),
                pl.BlockSpec((pl.Element(k), pl.Element(tn)), lambda j, i: (0, w_col(j)))]
    in_specs += [tab] * len(tables)
    out_specs = [pl.BlockSpec((tm, tn), lambda j, i: (i, j))]
    out_shape = [jax.ShapeDtypeStruct((t, n_tiles * tn), BF16)]
    ride_steps = []
    for arr, first, n_slabs in rides:
        slab = arr.shape[0] // n_slabs
        assert slab * n_slabs == arr.shape[0] and slab % 16 == 0 and first + n_slabs <= n_tiles * nt
        spec = pl.BlockSpec(
            (slab, arr.shape[1]),
            lambda j, i, first=first, n=n_slabs: (jnp.clip(j * nt + i - first, 0, n - 1), 0))
        in_specs.append(spec)
        out_specs.append(spec)
        out_shape.append(jax.ShapeDtypeStruct(arr.shape, BF16))
        ride_steps.append((first, first + n_slabs))
    segments = []
    for (_, width, epilogue), j0 in zip(segs, j0s):
        j1 = j0 + width // tn
        active = tuple(r for r, (s0, s1) in enumerate(ride_steps) if s0 < j1 * nt and s1 > j0 * nt)
        segments.append((j0, j1, epilogue, active))
    outs = pl.pallas_call(
        functools.partial(_proj_kernel, tuple(segments), len(tables), len(rides)),
        grid=(n_tiles, nt),
        in_specs=in_specs,
        out_specs=out_specs,
        out_shape=out_shape,
        scratch_shapes=[pltpu.VMEM((k, tn), BF16)],
        compiler_params=_params(("arbitrary", "arbitrary"), 56),
        name=name,
    )(xb, w, *tables, *[arr for arr, _, _ in rides])
    return outs[0] if not rides else outs


def _retention_kernel(q_ref, k_ref, v_ref, gs_ref, ga_ref, gng_ref, gnb_ref,
                      o_ref, state_ref, dmask_ref, qdec_ref, kdec_ref, cdec_ref):
    c = RET_CHUNK

    @pl.when(pl.program_id(1) == 0)
    def _init():
        state_ref[...] = jnp.zeros_like(state_ref)
        ii = lax.broadcasted_iota(jnp.int32, (c, c), 0)
        jj = lax.broadcasted_iota(jnp.int32, (c, c), 1)
        diff = (ii - jj).astype(F32)
        idx = lax.broadcasted_iota(jnp.int32, (c, RET_DK), 0).astype(F32)
        for h in range(RET_HEADS):
            def log_gamma(shape):
                return jnp.log(1.0 - jnp.exp2(-5.0 - jnp.full(shape, float(h), F32)))

            dmask_ref[h] = jnp.where(
                diff >= 0.0, jnp.exp(log_gamma((c, c)) * jnp.maximum(diff, 0.0)), 0.0)
            lgl = log_gamma((c, RET_DK))
            qdec_ref[h] = jnp.exp(lgl * (idx + 1.0))
            kdec_ref[h] = jnp.exp(lgl * (c - 1.0 - idx))
            cdec_ref[h] = jnp.exp(log_gamma((8, LANES)) * float(c))

    for r0, h in ((r0, h) for r0 in range(0, q_ref.shape[0], c) for h in range(RET_HEADS)):
        rs = slice(r0, r0 + c)
        ksl = slice(h * RET_DK, (h + 1) * RET_DK)
        vsl = slice(h * RET_DV, (h + 1) * RET_DV)
        qb = q_ref[rs, ksl]
        kb = k_ref[rs, ksl]
        v = v_ref[rs, vsl]
        qd = (qb.astype(F32) * qdec_ref[h]).astype(BF16)
        kd = (kb.astype(F32) * kdec_ref[h]).astype(BF16)

        s = lax.dot_general(qb, kb, (((1,), (1,)), ((), ())), preferred_element_type=F32)
        s = s * dmask_ref[h]
        inner = jnp.dot(s.astype(BF16), v, preferred_element_type=F32)
        state = state_ref[h]
        cross = jnp.dot(qd, state.astype(BF16), preferred_element_type=F32)
        kv = lax.dot_general(kd, v, (((0,), (0,)), ((), ())), preferred_element_type=F32)
        state_ref[h] = state * cdec_ref[h, 0:1, 0:1] + kv

        y = inner + cross
        mu = jnp.mean(y, axis=-1, keepdims=True)
        d = y - mu
        var = jnp.mean(d * d, axis=-1, keepdims=True)
        yn = d * lax.rsqrt(var + GN_EPS) * gng_ref[:, vsl] + gnb_ref[:, vsl]
        gate = ga_ref[rs, vsl].astype(F32) * gs_ref[rs, vsl].astype(F32)
        o_ref[rs, vsl] = (gate * yn).astype(o_ref.dtype)


RET_STEP_CHUNKS = 2


def _retention(qa, vg, gn_g, gn_b, bsz, seq):
    t = bsz * seq
    c = RET_CHUNK
    rows_per_step = RET_STEP_CHUNKS * c
    nc = seq // rows_per_step
    nh = RET_HEADS

    def rows(width, col):
        return pl.BlockSpec((rows_per_step, width), lambda b, n: (b * nc + n, col))

    gn = pl.BlockSpec((1, D_MODEL), lambda b, n: (0, 0))
    return pl.pallas_call(
        _retention_kernel,
        grid=(bsz, nc),
        in_specs=[rows(nh * RET_DK, 0), rows(nh * RET_DK, 1), rows(D_MODEL, 0), rows(D_MODEL, 1),
                  rows(D_MODEL, 2), gn, gn],
        out_specs=rows(D_MODEL, 0),
        out_shape=jax.ShapeDtypeStruct((t, D_MODEL), BF16),
        scratch_shapes=[pltpu.VMEM((nh, RET_DK, RET_DV), F32), pltpu.VMEM((nh, c, c), F32),
                        pltpu.VMEM((nh, c, RET_DK), F32), pltpu.VMEM((nh, c, RET_DK), F32),
                        pltpu.VMEM((nh, 8, LANES), F32)],
        compiler_params=_params(("parallel", "arbitrary"), 32),
        name="retention",
    )(qa, qa, vg, vg, vg, gn_g.reshape(1, -1), gn_b.reshape(1, -1))


SINK_LANES = 8


def _pair_blocks(prev, cur, g):
    c0 = (g // 2) * LANES
    kk = jnp.concatenate([prev[:, c0:c0 + LANES], cur[:, c0:c0 + LANES]], axis=0).astype(F32)
    lane = lax.broadcasted_iota(jnp.int32, kk.shape, 1)
    row = lax.broadcasted_iota(jnp.int32, kk.shape, 0)
    hd = ATT_HEAD_DIM
    own = jnp.where(((lane >= hd) if g % 2 else (lane < hd)) & (row != 0), kk, 0.0)
    other = pltpu.roll(own, hd, 1)
    lo, hi = (other, own) if g % 2 else (own, other)
    return lo.astype(BF16), hi.astype(BF16)


def _attention_consts(sink_ref, qaug_ref, kaug_ref, vones_ref):
    nk = 2 * ATT_BLOCK
    row_q = lax.broadcasted_iota(jnp.int32, qaug_ref.shape, 0)
    lane_q = lax.broadcasted_iota(jnp.int32, qaug_ref.shape, 1)
    pair_of_lane = (lane_q & (SINK_LANES - 1)) >> 1
    onehot = (lane_q < 2 * SINK_LANES) & (pair_of_lane == row_q // ATT_BLOCK)
    qaug_ref[...] = jnp.where(onehot, 1.0, 0.0).astype(BF16)
    row_k = lax.broadcasted_iota(jnp.int32, vones_ref.shape, 0)
    lane_k = lax.broadcasted_iota(jnp.int32, vones_ref.shape, 1)
    vones_ref[...] = jnp.where((row_k < nk) == (lane_k < ATT_HEAD_DIM), 1.0, 0.0).astype(BF16)
    for g in range(ATT_KV_HEADS):
        vals = jnp.zeros(vones_ref.shape, F32)
        for j in range(ATT_GROUP):
            sk = sink_ref[g * ATT_GROUP + j] * LOG2E
            vals = jnp.where((lane_k == j) & (row_k == (j % 2) * nk), sk, vals)
        hi = vals.astype(BF16).astype(F32)
        kaug_ref[g] = (hi + pltpu.roll(vals - hi, SINK_LANES, 1)).astype(BF16)


def _attention_kernel(last, sink_ref, q_ref, kp_ref, kc_ref, vp_ref, vc_ref, gb_ref, ret_ref, o_ref,
                      qaug_ref, kaug_ref, vones_ref, *p_refs):
    i = pl.program_id(1)
    bq = ATT_BLOCK
    nk = 2 * bq
    pairs = ATT_GROUP // 2

    @pl.when((pl.program_id(0) == 0) & (i == 0))
    def _init():
        _attention_consts(sink_ref, qaug_ref, kaug_ref, vones_ref)

    def scores(slot):
        kp, kc = kp_ref[...], kc_ref[...]
        qi = lax.broadcasted_iota(jnp.int32, (bq, nk), 0)
        kj = lax.broadcasted_iota(jnp.int32, (bq, nk), 1)
        dist = qi + bq - kj
        kmin = jnp.where(i > 0, 0, bq)
        valid = ((dist >= 0) & (dist < WINDOW) & (kj >= kmin)) | (kj == 0)
        bias = jnp.where(valid, 0.0, -jnp.inf)
        qaug = qaug_ref[...]
        for g in range(ATT_KV_HEADS):
            k_lo, k_hi = _pair_blocks(kp, kc, g)
            kfull = jnp.concatenate([jnp.concatenate([k_lo, k_hi], axis=0), kaug_ref[g]], axis=1)
            col0 = g * ATT_GROUP * ATT_HEAD_DIM
            qp = jnp.concatenate(
                [q_ref[:, col0 + p * LANES:col0 + (p + 1) * LANES] for p in range(pairs)], axis=0)
            qfull = jnp.concatenate([qp, qaug], axis=1)
            s = lax.dot_general(qfull, kfull, (((1,), (1,)), ((), ())),
                                preferred_element_type=F32)
            for p in range(pairs):
                for e in range(2):
                    sj = s[p * bq:(p + 1) * bq, e * nk:(e + 1) * nk] + bias
                    m = jnp.max(sj, axis=-1, keepdims=True)
                    p_refs[slot][g, p * bq:(p + 1) * bq, e * nk:(e + 1) * nk] = (
                        jnp.exp2(sj - m).astype(BF16))

    def values(slot):
        vp, vc = vp_ref[...], vc_ref[...]
        vones = vones_ref[...]
        for g in range(ATT_KV_HEADS):
            v_lo, v_hi = _pair_blocks(vp, vc, g)
            vfull = jnp.concatenate([jnp.concatenate([v_lo, v_hi], axis=0), vones], axis=1)
            o = jnp.dot(p_refs[slot][g], vfull, preferred_element_type=F32)
            col0 = g * ATT_GROUP * ATT_HEAD_DIM
            for p in range(pairs):
                num = o[p * bq:(p + 1) * bq, :LANES]
                den = o[p * bq:(p + 1) * bq, LANES:]
                c1 = col0 + p * LANES
                gate = gb_ref[:, c1:c1 + LANES].astype(F32)
                ret = ret_ref[:, c1:c1 + LANES].astype(F32)
                o_ref[:, c1:c1 + LANES] = (gate * num / den + ret).astype(o_ref.dtype)

    @pl.when(i == 0)
    def _first():
        scores(0)

    for par in (0, 1):
        @pl.when((i > 0) & (i < last) & (lax.rem(i, 2) == par))
        def _steady():
            values(1 - par)
            scores(par)

    @pl.when(i == last)
    def _last():
        values((last - 1) % 2)


def _attention(qa, akv, vg, ret, sinks, bsz, seq):
    t = bsz * seq
    bq = ATT_BLOCK
    nb = seq // bq
    rows = (ATT_GROUP // 2) * bq

    def spec(width, col, lag, back):
        def index(b, i):
            blk = jnp.clip(i - lag, 0, nb - 1)
            return (b * nb + jnp.maximum(blk - back, 0), col)
        return pl.BlockSpec((bq, width), index)

    return pl.pallas_call(
        functools.partial(_attention_kernel, nb),
        grid=(bsz, nb + 1),
        in_specs=[pl.BlockSpec(memory_space=pltpu.SMEM), spec(D_MODEL, 1, 0, 0),
                  spec(KV_W, 0, 0, 1), spec(KV_W, 0, 0, 0),
                  spec(KV_W, 1, 1, 1), spec(KV_W, 1, 1, 0), spec(D_MODEL, 3, 1, 0),
                  spec(D_MODEL, 0, 1, 0)],
        out_specs=spec(D_MODEL, 0, 1, 0),
        out_shape=jax.ShapeDtypeStruct((t, D_MODEL), BF16),
        scratch_shapes=[pltpu.VMEM((rows, LANES), BF16),
                        pltpu.VMEM((ATT_KV_HEADS, 2 * 2 * bq, LANES), BF16),
                        pltpu.VMEM((2 * 2 * bq, LANES), BF16),
                        pltpu.VMEM((ATT_KV_HEADS, rows, 2 * 2 * bq), BF16),
                        pltpu.VMEM((ATT_KV_HEADS, rows, 2 * 2 * bq), BF16)],
        compiler_params=_params(("arbitrary", "arbitrary"), 32),
        name="attention",
    )(sinks, qa, akv, akv, akv, akv, vg, ret)


OUT_PROJ_ROW_CHUNKS = 2


def _out_proj_kernel(alpha, m_ref, x_ref, w_ref, g_ref, b_ref, h_ref, hb_ref):
    rows = h_ref.shape[0] // OUT_PROJ_ROW_CHUNKS
    for c in range(OUT_PROJ_ROW_CHUNKS):
        rs = slice(c * rows, (c + 1) * rows)
        y = jnp.dot(m_ref[rs, :], w_ref[...], preferred_element_type=F32)
        h = _layer_norm(alpha * x_ref[rs, :] + y, g_ref[...], b_ref[...])
        h_ref[rs, :] = h
        hb_ref[rs, :] = h.astype(BF16)


def _out_proj(merged, x2, wb, ln_g, ln_b, alpha):
    t, d = x2.shape
    tm = 512
    row = pl.BlockSpec((tm, d), lambda i: (i, 0))
    vec = pl.BlockSpec((1, d), lambda i: (0, 0))
    return pl.pallas_call(
        functools.partial(_out_proj_kernel, alpha),
        grid=(t // tm,),
        in_specs=[row, row, pl.BlockSpec((d, d), lambda i: (0, 0)), vec, vec],
        out_specs=[row, row],
        out_shape=[jax.ShapeDtypeStruct((t, d), F32), jax.ShapeDtypeStruct((t, d), BF16)],
        compiler_params=_params(("parallel",), 48),
        name="out_proj_ln",
    )(merged, x2, wb, ln_g.reshape(1, d), ln_b.reshape(1, d))


def _ffn_kernel(alpha, hb_ref, h_hbm, wg_ref, wu_ref, wd_ref, g_ref, b_ref, o_ref, hres_ref, sem):
    i = pl.program_id(0)
    f = pl.program_id(1)
    last = pl.num_programs(1) - 1
    tm = o_ref.shape[0]
    res_copy = pltpu.make_async_copy(h_hbm.at[pl.ds(i * tm, tm), :], hres_ref, sem)

    def tile():
        hb = hb_ref[...]
        gt = jnp.dot(hb, wg_ref[...], preferred_element_type=F32)
        ut = jnp.dot(hb, wu_ref[...], preferred_element_type=F32)
        a = (gt * _sigmoid(gt) * ut).astype(BF16)
        return jnp.dot(a, wd_ref[...], preferred_element_type=F32)

    @pl.when(f == 0)
    def _first():
        res_copy.start()
        o_ref[...] = tile()

    @pl.when((f > 0) & (f < last))
    def _middle():
        o_ref[...] += tile()

    @pl.when(f == last)
    def _last():
        res_copy.wait()
        z = alpha * hres_ref[...] + (o_ref[...] + tile())
        o_ref[...] = _layer_norm(z, g_ref[...], b_ref[...])


def _ffn(h1, h1b, wg, wu, wd, ln_g, ln_b, alpha):
    t, d = h1.shape
    dff = wg.shape[1]
    tm, tf = 1024, 512
    row = pl.BlockSpec((tm, d), lambda i, f: (i, 0))
    vec = pl.BlockSpec((1, d), lambda i, f: (0, 0))
    return pl.pallas_call(
        functools.partial(_ffn_kernel, alpha),
        grid=(t // tm, dff // tf),
        in_specs=[row,
                  pl.BlockSpec(memory_space=pl.ANY),
                  pl.BlockSpec((d, tf), lambda i, f: (0, f)),
                  pl.BlockSpec((d, tf), lambda i, f: (0, f)),
                  pl.BlockSpec((tf, d), lambda i, f: (f, 0)),
                  vec, vec],
        out_specs=row,
        out_shape=jax.ShapeDtypeStruct((t, d), F32),
        scratch_shapes=[pltpu.VMEM((tm, d), F32), pltpu.SemaphoreType.DMA(())],
        compiler_params=_params(("parallel", "arbitrary"), 60),
        name="ffn_ln",
    )(h1b, h1, wg, wu, wd, ln_g.reshape(1, d), ln_b.reshape(1, d))


def kernel(x, positions, w_in, ret_gn_g, ret_gn_b, att_sinks, w_out, ln1_g, ln1_b,
           w_gate, w_up, w_down, ln2_g, ln2_b):
    bsz, seq, d = x.shape
    depth = w_in.shape[0]
    alpha = (2.0 * depth) ** 0.25
    h = x.reshape(bsz * seq, d)
    cos_r, sin_r, ca, s1, s2, hb = _prep(positions, h)
    for l in range(depth):
        if l > 0:
            hb = h.astype(BF16)
        w = w_in[l]
        steps = 4 * (bsz * seq // 1024)
        qa, wdb = _proj_multi(
            hb, w, [(OFF_RQ, OFF_RV - OFF_RQ, _ep_ret_rotary), (OFF_AQ, D_MODEL, _ep_att_q)], 1024,
            (cos_r, sin_r, ca, s1, s2), [(w_down[l], 0, steps)], "in_proj_qk")
        vg, wgb, wub = _proj_multi(
            hb, w, [(OFF_RV, D_MODEL, _ep_plain), (OFF_RG, D_MODEL, _ep_swish),
                    (OFF_GA, 2 * D_MODEL, _ep_sigmoid)], 1024,
            (), [(w_gate[l], 0, steps), (w_up[l], steps, steps)], "in_proj_vg")
        akv, wob = _proj_multi(hb, w, [(OFF_AK, 2 * KV_W, _ep_att_kv)], 2 * KV_W, (ca, s1, s2),
                               [(w_out[l], 0, steps // 4)], "in_proj_att_kv")
        ret = _retention(qa, vg, ret_gn_g[l], ret_gn_b[l], bsz, seq)
        merged = _attention(qa, akv, vg, ret, att_sinks[l], bsz, seq)
        h, hb = _out_proj(merged, h, wob, ln1_g[l], ln1_b[l], alpha)
        h = _ffn(h, hb, wgb, wub, wdb, ln2_g[l], ln2_b[l], alpha)
    return h.reshape(bsz, seq, d)
```

```python
import functools
import math

import jax
import jax.numpy as jnp
from jax import lax
from jax.experimental import pallas as pl
from jax.experimental.pallas import tpu as pltpu

F32 = jnp.float32
BF16 = jnp.bfloat16

D_MODEL = 2048
RET_HEADS = 4
RET_DV = D_MODEL // RET_HEADS
RET_DK = RET_DV // 2
RET_CHUNK = 256
RET_THETA = 10000.0
ATT_HEAD_DIM = 64
ATT_HEADS = D_MODEL // ATT_HEAD_DIM
ATT_KV_HEADS = ATT_HEADS // 8
ATT_GROUP = ATT_HEADS // ATT_KV_HEADS
WINDOW = 128
ATT_BLOCK = WINDOW
ROPE_THETA = 500000.0
ROPE_DIM = ATT_HEAD_DIM // 4
LN_EPS = 1e-5
GN_EPS = 1e-5

LANES = 128
LOG2E = math.log2(math.e)

OFF_RQ = 0
OFF_RK = OFF_RQ + RET_HEADS * RET_DK
OFF_RV = OFF_RK + RET_HEADS * RET_DK
OFF_RG = OFF_RV + RET_HEADS * RET_DV
OFF_AQ = OFF_RG + RET_HEADS * RET_DV
OFF_AK = OFF_AQ + ATT_HEADS * ATT_HEAD_DIM
OFF_AV = OFF_AK + ATT_KV_HEADS * ATT_HEAD_DIM
OFF_GA = OFF_AV + ATT_KV_HEADS * ATT_HEAD_DIM
OFF_GB = OFF_GA + D_MODEL
KV_W = ATT_KV_HEADS * ATT_HEAD_DIM

MIB = 1024 * 1024


def _params(sem, vmem_mib, flags=None):
    return pltpu.CompilerParams(dimension_semantics=sem, vmem_limit_bytes=vmem_mib * MIB,
                                flags=flags)


def _sigmoid(v):
    return 0.5 * jnp.tanh(0.5 * v) + 0.5


def _layer_norm(z, g, b):
    mu = jnp.mean(z, axis=-1, keepdims=True)
    d = z - mu
    var = jnp.mean(d * d, axis=-1, keepdims=True)
    return d * lax.rsqrt(var + LN_EPS) * g + b


ROPE_PACK = ATT_HEAD_DIM // ROPE_DIM
PREP_ROWS = 64


def _prep_kernel(pos_ref, fr_ref, fa_ref, cr_ref, sr_ref, ca_ref, s1_ref, s2_ref, pos_scr):
    tm = pos_scr.shape[0]
    pos_t = pos_ref[...].astype(F32).T
    cols = [jnp.broadcast_to(pos_t[:, c:c + 1], (LANES, LANES)) for c in range(tm // LANES)]
    pos_scr[...] = jnp.concatenate(cols, axis=0)
    fr = fr_ref[...]

    def ret_rows(k, carry):
        rs = pl.ds(pl.multiple_of(k * PREP_ROWS, PREP_ROWS), PREP_ROWS)
        ang_r = pos_scr[rs, :] * fr
        cr_ref[rs, :] = jnp.cos(ang_r)
        sr_ref[rs, :] = jnp.sin(ang_r)
        return carry

    lax.fori_loop(0, tm // PREP_ROWS, ret_rows, 0)
    rg = tm // ROPE_PACK
    per = rg // LANES
    lane = lax.broadcasted_iota(jnp.int32, (rg, LANES), 1)
    group = (lane & (ATT_HEAD_DIM - 1)) // ROPE_DIM
    pos_p = jnp.concatenate(cols[:per], axis=0)
    for g in range(1, ROPE_PACK):
        pos_p = jnp.where(group == g, jnp.concatenate(cols[g * per:(g + 1) * per], axis=0), pos_p)
    ang_p = pos_p * fa_ref[...]
    cp = jnp.cos(ang_p)
    sp = jnp.sin(ang_p)
    dp = lane & (ROPE_DIM - 1)
    half = ROPE_DIM // 2
    s1p = jnp.where(dp < half, -sp, 0.0)
    s2p = jnp.where(dp >= half, sp, 0.0)
    rotated = group == 0
    for g in range(ROPE_PACK):
        def spread(p, fill):
            a = pltpu.roll(p, LANES - ROPE_DIM * g, 1) if g else p
            return jnp.where(rotated, a, fill)

        rs = slice(g * rg, (g + 1) * rg)
        ca_ref[rs, :] = spread(cp, 1.0)
        s1_ref[rs, :] = spread(s1p, 0.0)
        s2_ref[rs, :] = spread(s2p, 0.0)


def _prep(positions):
    t = positions.size
    tm = 1024
    pos_l = positions.reshape(t // LANES, LANES)
    inv_r = 1.0 / (RET_THETA ** jnp.linspace(0.0, 1.0, RET_DK // 2, dtype=F32))
    inv_a = 1.0 / (ROPE_THETA ** (jnp.arange(0, ROPE_DIM, 2, dtype=F32) / ROPE_DIM))
    fa = jnp.tile(inv_a, LANES // inv_a.size)
    row = pl.BlockSpec((tm, LANES), lambda i: (i, 0))
    vec = pl.BlockSpec((1, LANES), lambda i: (0, 0))
    tab = jax.ShapeDtypeStruct((t, LANES), F32)
    return pl.pallas_call(
        _prep_kernel,
        grid=(t // tm,),
        in_specs=[pl.BlockSpec((tm // LANES, LANES), lambda i: (i, 0)), vec, vec],
        out_specs=[row] * 5,
        out_shape=[tab] * 5,
        scratch_shapes=[pltpu.VMEM((tm, LANES), F32)],
        compiler_params=_params(("parallel",), 60),
        name="rotary_tables",
    )(pos_l, inv_r.reshape(1, LANES), fa.reshape(1, LANES))


def _ep_plain(acc, j, *tabs):
    return acc


def _ep_swish(acc, j, *tabs):
    return acc * _sigmoid(acc)


def _ep_sigmoid(acc, j, *tabs):
    return _sigmoid(acc)


def _ep_ret_rotary(acc, j, cos_ref, sin_ref, *att_tabs):
    scale = jnp.where(j == 0, RET_DK ** -0.5, 1.0)
    cos = cos_ref[...] * scale
    sin = sin_ref[...] * scale
    half = RET_DK // 2
    outs = []
    for c0 in range(0, acc.shape[1], RET_DK):
        a1 = acc[:, c0:c0 + half]
        a2 = acc[:, c0 + half:c0 + RET_DK]
        outs += [a1 * cos - a2 * sin, a2 * cos + a1 * sin]
    return jnp.concatenate(outs, axis=-1)


def _rope_lanes(a, ca, s1, s2):
    outs = []
    for c0 in range(0, a.shape[1], LANES):
        ac = a[:, c0:c0 + LANES]
        up = pltpu.roll(ac, LANES - ROPE_DIM // 2, 1)
        dn = pltpu.roll(ac, ROPE_DIM // 2, 1)
        outs.append(ac * ca + up * s1 + dn * s2)
    return jnp.concatenate(outs, axis=-1)


def _ep_att_q(acc, j, cos_ref, sin_ref, ca_ref, s1_ref, s2_ref):
    c = (ATT_HEAD_DIM ** -0.5) * LOG2E
    return _rope_lanes(acc, ca_ref[...] * c, s1_ref[...] * c, s2_ref[...] * c)


def _ep_att_kv(acc, j, ca_ref, s1_ref, s2_ref):
    k = _rope_lanes(acc[:, :KV_W], ca_ref[...], s1_ref[...], s2_ref[...])
    return jnp.concatenate([k, acc[:, KV_W:]], axis=-1)


def _proj_kernel(segments, n_tab, n_ride, cast_x, x_ref, w_ref, *rest):
    tab_refs = rest[:n_tab]
    ride_refs = rest[n_tab:n_tab + n_ride]
    o_ref = rest[n_tab + n_ride]
    xb_ref = rest[n_tab + n_ride + 1] if cast_x else None
    ride_out_refs = rest[-1 - n_ride:-1]
    wb_ref = rest[-1]
    j = pl.program_id(0)

    @pl.when(pl.program_id(1) == 0)
    def _cast_weights():
        wb_ref[...] = w_ref[...].astype(BF16)

    for j0, j1, epilogue, rides in segments:
        @pl.when((j >= j0) & (j < j1))
        def _segment():
            for r in rides:
                ride_out_refs[r][...] = ride_refs[r][...].astype(BF16)
            x = x_ref[...]
            if cast_x:
                x = x.astype(BF16)
                xb_ref[...] = x
            acc = jnp.dot(x, wb_ref[...], preferred_element_type=F32)
            o_ref[...] = epilogue(acc, j - j0, *tab_refs).astype(o_ref.dtype)


def _proj_multi(xb, w, segs, tn, tables=(), rides=(), name="in_proj", cast_x=False):
    t, k = xb.shape
    tm = 1024
    nt = t // tm
    j0s, bounds = [], 0
    for col0, width, _ in segs:
        assert col0 % LANES == 0 and width % tn == 0
        j0s.append(bounds)
        bounds += width // tn
    n_tiles = bounds

    def w_col(j):
        col = segs[0][0] + j * tn
        for (col0, _, _), j0 in zip(segs[1:], j0s[1:]):
            col = jnp.where(j >= j0, col0 + (j - j0) * tn, col)
        return pl.multiple_of(col, LANES)

    tab = pl.BlockSpec((tm, LANES), lambda j, i: (i, 0))
    in_specs = [pl.BlockSpec((tm, k), lambda j, i: (i, 0)),
                pl.BlockSpec((pl.Element(k), pl.Element(tn)), lambda j, i: (0, w_col(j)))]
    in_specs += [tab] * len(tables)
    out_specs = [pl.BlockSpec((tm, tn), lambda j, i: (i, j))]
    out_shape = [jax.ShapeDtypeStruct((t, n_tiles * tn), BF16)]
    if cast_x:
        assert n_tiles == 1
        out_specs.append(pl.BlockSpec((tm, k), lambda j, i: (i, 0)))
        out_shape.append(jax.ShapeDtypeStruct((t, k), BF16))
    ride_steps = []
    for arr, first, n_slabs in rides:
        slab = arr.shape[0] // n_slabs
        assert slab * n_slabs == arr.shape[0] and slab % 16 == 0 and first + n_slabs <= n_tiles * nt
        spec = pl.BlockSpec(
            (slab, arr.shape[1]),
            lambda j, i, first=first, n=n_slabs: (jnp.clip(j * nt + i - first, 0, n - 1), 0))
        in_specs.append(spec)
        out_specs.append(spec)
        out_shape.append(jax.ShapeDtypeStruct(arr.shape, BF16))
        ride_steps.append((first, first + n_slabs))
    segments = []
    for (_, width, epilogue), j0 in zip(segs, j0s):
        j1 = j0 + width // tn
        active = tuple(r for r, (s0, s1) in enumerate(ride_steps) if s0 < j1 * nt and s1 > j0 * nt)
        segments.append((j0, j1, epilogue, active))
    outs = pl.pallas_call(
        functools.partial(_proj_kernel, tuple(segments), len(tables), len(rides), cast_x),
        grid=(n_tiles, nt),
        in_specs=in_specs,
        out_specs=out_specs,
        out_shape=out_shape,
        scratch_shapes=[pltpu.VMEM((k, tn), BF16)],
        compiler_params=_params(("arbitrary", "arbitrary"), 56),
        name=name,
    )(xb, w, *tables, *[arr for arr, _, _ in rides])
    return outs[0] if len(outs) == 1 else outs


def _retention_kernel(q_ref, k_ref, v_ref, gs_ref, ga_ref, gng_ref, gnb_ref,
                      o_ref, state_ref, dmask_ref, qdec_ref, kdec_ref, cdec_ref):
    c = RET_CHUNK

    @pl.when(pl.program_id(1) == 0)
    def _init():
        state_ref[...] = jnp.zeros_like(state_ref)
        ii = lax.broadcasted_iota(jnp.int32, (c, c), 0)
        jj = lax.broadcasted_iota(jnp.int32, (c, c), 1)
        diff = (ii - jj).astype(F32)
        idx = lax.broadcasted_iota(jnp.int32, (c, RET_DK), 0).astype(F32)
        for h in range(RET_HEADS):
            def log_gamma(shape):
                return jnp.log(1.0 - jnp.exp2(-5.0 - jnp.full(shape, float(h), F32)))

            dmask_ref[h] = jnp.where(
                diff >= 0.0, jnp.exp(log_gamma((c, c)) * jnp.maximum(diff, 0.0)), 0.0)
            lgl = log_gamma((c, RET_DK))
            qdec_ref[h] = jnp.exp(lgl * (idx + 1.0))
            kdec_ref[h] = jnp.exp(lgl * (c - 1.0 - idx))
            cdec_ref[h] = jnp.exp(log_gamma((8, LANES)) * float(c))

    for r0, h in ((r0, h) for r0 in range(0, q_ref.shape[0], c) for h in range(RET_HEADS)):
        rs = slice(r0, r0 + c)
        ksl = slice(h * RET_DK, (h + 1) * RET_DK)
        vsl = slice(h * RET_DV, (h + 1) * RET_DV)
        qb = q_ref[rs, ksl]
        kb = k_ref[rs, ksl]
        v = v_ref[rs, vsl]
        qd = (qb.astype(F32) * qdec_ref[h]).astype(BF16)
        kd = (kb.astype(F32) * kdec_ref[h]).astype(BF16)

        s = lax.dot_general(qb, kb, (((1,), (1,)), ((), ())), preferred_element_type=F32)
        s = s * dmask_ref[h]
        inner = jnp.dot(s.astype(BF16), v, preferred_element_type=F32)
        state = state_ref[h]
        cross = jnp.dot(qd, state.astype(BF16), preferred_element_type=F32)
        kv = lax.dot_general(kd, v, (((0,), (0,)), ((), ())), preferred_element_type=F32)
        state_ref[h] = state * cdec_ref[h, 0:1, 0:1] + kv

        y = inner + cross
        mu = jnp.mean(y, axis=-1, keepdims=True)
        d = y - mu
        var = jnp.mean(d * d, axis=-1, keepdims=True)
        yn = d * lax.rsqrt(var + GN_EPS) * gng_ref[:, vsl] + gnb_ref[:, vsl]
        gate = ga_ref[rs, vsl].astype(F32) * gs_ref[rs, vsl].astype(F32)
        o_ref[rs, vsl] = (gate * yn).astype(o_ref.dtype)


RET_STEP_CHUNKS = 2


def _retention(qa, vg, gn_g, gn_b, bsz, seq):
    t = bsz * seq
    c = RET_CHUNK
    rows_per_step = RET_STEP_CHUNKS * c
    nc = seq // rows_per_step
    nh = RET_HEADS

    def rows(width, col):
        return pl.BlockSpec((rows_per_step, width), lambda b, n: (b * nc + n, col))

    gn = pl.BlockSpec((1, D_MODEL), lambda b, n: (0, 0))
    return pl.pallas_call(
        _retention_kernel,
        grid=(bsz, nc),
        in_specs=[rows(nh * RET_DK, 0), rows(nh * RET_DK, 1), rows(D_MODEL, 0), rows(D_MODEL, 1),
                  rows(D_MODEL, 2), gn, gn],
        out_specs=rows(D_MODEL, 0),
        out_shape=jax.ShapeDtypeStruct((t, D_MODEL), BF16),
        scratch_shapes=[pltpu.VMEM((nh, RET_DK, RET_DV), F32), pltpu.VMEM((nh, c, c), F32),
                        pltpu.VMEM((nh, c, RET_DK), F32), pltpu.VMEM((nh, c, RET_DK), F32),
                        pltpu.VMEM((nh, 8, LANES), F32)],
        compiler_params=_params(("parallel", "arbitrary"), 32),
        name="retention",
    )(qa, qa, vg, vg, vg, gn_g.reshape(1, -1), gn_b.reshape(1, -1))


SINK_LANES = 8


def _pair_blocks(prev, cur, g):
    c0 = (g // 2) * LANES
    kk = jnp.concatenate([prev[:, c0:c0 + LANES], cur[:, c0:c0 + LANES]], axis=0).astype(F32)
    lane = lax.broadcasted_iota(jnp.int32, kk.shape, 1)
    row = lax.broadcasted_iota(jnp.int32, kk.shape, 0)
    hd = ATT_HEAD_DIM
    own = jnp.where(((lane >= hd) if g % 2 else (lane < hd)) & (row != 0), kk, 0.0)
    other = pltpu.roll(own, hd, 1)
    lo, hi = (other, own) if g % 2 else (own, other)
    return lo.astype(BF16), hi.astype(BF16)


def _attention_consts(sink_ref, qaug_ref, kaug_ref, vones_ref):
    nk = 2 * ATT_BLOCK
    row_q = lax.broadcasted_iota(jnp.int32, qaug_ref.shape, 0)
    lane_q = lax.broadcasted_iota(jnp.int32, qaug_ref.shape, 1)
    pair_of_lane = (lane_q & (SINK_LANES - 1)) >> 1
    onehot = (lane_q < 2 * SINK_LANES) & (pair_of_lane == row_q // ATT_BLOCK)
    qaug_ref[...] = jnp.where(onehot, 1.0, 0.0).astype(BF16)
    row_k = lax.broadcasted_iota(jnp.int32, vones_ref.shape, 0)
    lane_k = lax.broadcasted_iota(jnp.int32, vones_ref.shape, 1)
    vones_ref[...] = jnp.where((row_k < nk) == (lane_k < ATT_HEAD_DIM), 1.0, 0.0).astype(BF16)
    for g in range(ATT_KV_HEADS):
        vals = jnp.zeros(vones_ref.shape, F32)
        for j in range(ATT_GROUP):
            sk = sink_ref[g * ATT_GROUP + j] * LOG2E
            vals = jnp.where((lane_k == j) & (row_k == (j % 2) * nk), sk, vals)
        hi = vals.astype(BF16).astype(F32)
        kaug_ref[g] = (hi + pltpu.roll(vals - hi, SINK_LANES, 1)).astype(BF16)


def _attention_kernel(last, sink_ref, q_ref, kp_ref, kc_ref, vp_ref, vc_ref, gb_ref, ret_ref, o_ref,
                      qaug_ref, kaug_ref, vones_ref, *p_refs):
    i = pl.program_id(1)
    bq = ATT_BLOCK
    nk = 2 * bq
    pairs = ATT_GROUP // 2

    @pl.when((pl.program_id(0) == 0) & (i == 0))
    def _init():
        _attention_consts(sink_ref, qaug_ref, kaug_ref, vones_ref)

    def scores(slot):
        kp, kc = kp_ref[...], kc_ref[...]
        qi = lax.broadcasted_iota(jnp.int32, (bq, nk), 0)
        kj = lax.broadcasted_iota(jnp.int32, (bq, nk), 1)
        dist = qi + bq - kj
        kmin = jnp.where(i > 0, 0, bq)
        valid = ((dist >= 0) & (dist < WINDOW) & (kj >= kmin)) | (kj == 0)
        bias = jnp.where(valid, 0.0, -jnp.inf)
        qaug = qaug_ref[...]
        for g in range(ATT_KV_HEADS):
            k_lo, k_hi = _pair_blocks(kp, kc, g)
            kfull = jnp.concatenate([jnp.concatenate([k_lo, k_hi], axis=0), kaug_ref[g]], axis=1)
            col0 = g * ATT_GROUP * ATT_HEAD_DIM
            qp = jnp.concatenate(
                [q_ref[:, col0 + p * LANES:col0 + (p + 1) * LANES] for p in range(pairs)], axis=0)
            qfull = jnp.concatenate([qp, qaug], axis=1)
            s = lax.dot_general(qfull, kfull, (((1,), (1,)), ((), ())),
                                preferred_element_type=F32)
            for p in range(pairs):
                for e in range(2):
                    sj = s[p * bq:(p + 1) * bq, e * nk:(e + 1) * nk] + bias
                    m = jnp.max(sj, axis=-1, keepdims=True)
                    p_refs[slot][g, p * bq:(p + 1) * bq, e * nk:(e + 1) * nk] = (
                        jnp.exp2(sj - m).astype(BF16))

    def values(slot):
        vp, vc = vp_ref[...], vc_ref[...]
        vones = vones_ref[...]
        for g in range(ATT_KV_HEADS):
            v_lo, v_hi = _pair_blocks(vp, vc, g)
            vfull = jnp.concatenate([jnp.concatenate([v_lo, v_hi], axis=0), vones], axis=1)
            o = jnp.dot(p_refs[slot][g], vfull, preferred_element_type=F32)
            col0 = g * ATT_GROUP * ATT_HEAD_DIM
            for p in range(pairs):
                num = o[p * bq:(p + 1) * bq, :LANES]
                den = o[p * bq:(p + 1) * bq, LANES:]
                c1 = col0 + p * LANES
                gate = gb_ref[:, c1:c1 + LANES].astype(F32)
                ret = ret_ref[:, c1:c1 + LANES].astype(F32)
                o_ref[:, c1:c1 + LANES] = (gate * num / den + ret).astype(o_ref.dtype)

    @pl.when(i == 0)
    def _first():
        scores(0)

    for par in (0, 1):
        @pl.when((i > 0) & (i < last) & (lax.rem(i, 2) == par))
        def _steady():
            values(1 - par)
            scores(par)

    @pl.when(i == last)
    def _last():
        values((last - 1) % 2)


def _attention(qa, akv, vg, ret, sinks, bsz, seq):
    t = bsz * seq
    bq = ATT_BLOCK
    nb = seq // bq
    rows = (ATT_GROUP // 2) * bq

    def spec(width, col, lag, back):
        def index(b, i):
            blk = jnp.clip(i - lag, 0, nb - 1)
            return (b * nb + jnp.maximum(blk - back, 0), col)
        return pl.BlockSpec((bq, width), index)

    return pl.pallas_call(
        functools.partial(_attention_kernel, nb),
        grid=(bsz, nb + 1),
        in_specs=[pl.BlockSpec(memory_space=pltpu.SMEM), spec(D_MODEL, 1, 0, 0),
                  spec(KV_W, 0, 0, 1), spec(KV_W, 0, 0, 0),
                  spec(KV_W, 1, 1, 1), spec(KV_W, 1, 1, 0), spec(D_MODEL, 3, 1, 0),
                  spec(D_MODEL, 0, 1, 0)],
        out_specs=spec(D_MODEL, 0, 1, 0),
        out_shape=jax.ShapeDtypeStruct((t, D_MODEL), BF16),
        scratch_shapes=[pltpu.VMEM((rows, LANES), BF16),
                        pltpu.VMEM((ATT_KV_HEADS, 2 * 2 * bq, LANES), BF16),
                        pltpu.VMEM((2 * 2 * bq, LANES), BF16),
                        pltpu.VMEM((ATT_KV_HEADS, rows, 2 * 2 * bq), BF16),
                        pltpu.VMEM((ATT_KV_HEADS, rows, 2 * 2 * bq), BF16)],
        compiler_params=_params(("arbitrary", "arbitrary"), 32),
        name="attention",
    )(sinks, qa, akv, akv, akv, akv, vg, ret)


OUT_PROJ_ROW_CHUNKS = 2


def _out_proj_kernel(alpha, m_ref, x_ref, w_ref, g_ref, b_ref, h_ref, hb_ref):
    rows = h_ref.shape[0] // OUT_PROJ_ROW_CHUNKS
    for c in range(OUT_PROJ_ROW_CHUNKS):
        rs = slice(c * rows, (c + 1) * rows)
        y = jnp.dot(m_ref[rs, :], w_ref[...], preferred_element_type=F32)
        h = _layer_norm(alpha * x_ref[rs, :] + y, g_ref[...], b_ref[...])
        h_ref[rs, :] = h
        hb_ref[rs, :] = h.astype(BF16)


def _out_proj(merged, x2, wb, ln_g, ln_b, alpha):
    t, d = x2.shape
    tm = 512
    row = pl.BlockSpec((tm, d), lambda i: (i, 0))
    vec = pl.BlockSpec((1, d), lambda i: (0, 0))
    return pl.pallas_call(
        functools.partial(_out_proj_kernel, alpha),
        grid=(t // tm,),
        in_specs=[row, row, pl.BlockSpec((d, d), lambda i: (0, 0)), vec, vec],
        out_specs=[row, row],
        out_shape=[jax.ShapeDtypeStruct((t, d), F32), jax.ShapeDtypeStruct((t, d), BF16)],
        compiler_params=_params(("parallel",), 48),
        name="out_proj_ln",
    )(merged, x2, wb, ln_g.reshape(1, d), ln_b.reshape(1, d))


def _ffn_kernel(alpha, hb_ref, h_hbm, wg_ref, wu_ref, wd_ref, g_ref, b_ref, o_ref, hres_ref, sem):
    i = pl.program_id(0)
    f = pl.program_id(1)
    last = pl.num_programs(1) - 1
    tm = o_ref.shape[0]
    res_copy = pltpu.make_async_copy(h_hbm.at[pl.ds(i * tm, tm), :], hres_ref, sem)

    def tile():
        hb = hb_ref[...]
        gt = jnp.dot(hb, wg_ref[...], preferred_element_type=F32)
        ut = jnp.dot(hb, wu_ref[...], preferred_element_type=F32)
        a = (gt * _sigmoid(gt) * ut).astype(BF16)
        return jnp.dot(a, wd_ref[...], preferred_element_type=F32)

    @pl.when(f == 0)
    def _first():
        res_copy.start()
        o_ref[...] = tile()

    @pl.when((f > 0) & (f < last))
    def _middle():
        o_ref[...] += tile()

    @pl.when(f == last)
    def _last():
        res_copy.wait()
        z = alpha * hres_ref[...] + (o_ref[...] + tile())
        o_ref[...] = _layer_norm(z, g_ref[...], b_ref[...])


def _ffn(h1, h1b, wg, wu, wd, ln_g, ln_b, alpha):
    t, d = h1.shape
    dff = wg.shape[1]
    tm, tf = 1024, 512
    row = pl.BlockSpec((tm, d), lambda i, f: (i, 0))
    vec = pl.BlockSpec((1, d), lambda i, f: (0, 0))
    return pl.pallas_call(
        functools.partial(_ffn_kernel, alpha),
        grid=(t // tm, dff // tf),
        in_specs=[row,
                  pl.BlockSpec(memory_space=pl.ANY),
                  pl.BlockSpec((d, tf), lambda i, f: (0, f)),
                  pl.BlockSpec((d, tf), lambda i, f: (0, f)),
                  pl.BlockSpec((tf, d), lambda i, f: (f, 0)),
                  vec, vec],
        out_specs=row,
        out_shape=jax.ShapeDtypeStruct((t, d), F32),
        scratch_shapes=[pltpu.VMEM((tm, d), F32), pltpu.SemaphoreType.DMA(())],
        compiler_params=_params(("parallel", "arbitrary"), 60),
        name="ffn_ln",
    )(h1b, h1, wg, wu, wd, ln_g.reshape(1, d), ln_b.reshape(1, d))


def kernel(x, positions, w_in, ret_gn_g, ret_gn_b, att_sinks, w_out, ln1_g, ln1_b,
           w_gate, w_up, w_down, ln2_g, ln2_b):
    bsz, seq, d = x.shape
    depth = w_in.shape[0]
    alpha = (2.0 * depth) ** 0.25
    h = x.reshape(bsz * seq, d)
    cos_r, sin_r, ca, s1, s2 = _prep(positions)
    for l in range(depth):
        w = w_in[l]
        steps = 4 * (bsz * seq // 1024)
        akv, hb, wob = _proj_multi(h, w, [(OFF_AK, 2 * KV_W, _ep_att_kv)], 2 * KV_W, (ca, s1, s2),
                                   [(w_out[l], 0, steps // 4)], "in_proj_att_kv", cast_x=True)
        qa, wdb = _proj_multi(
            hb, w, [(OFF_RQ, OFF_RV - OFF_RQ, _ep_ret_rotary), (OFF_AQ, D_MODEL, _ep_att_q)], 1024,
            (cos_r, sin_r, ca, s1, s2), [(w_down[l], 0, steps)], "in_proj_qk")
        vg, wgb, wub = _proj_multi(
            hb, w, [(OFF_RV, D_MODEL, _ep_plain), (OFF_RG, D_MODEL, _ep_swish),
                    (OFF_GA, 2 * D_MODEL, _ep_sigmoid)], 1024,
            (), [(w_gate[l], 0, steps), (w_up[l], steps, steps)], "in_proj_vg")
        ret = _retention(qa, vg, ret_gn_g[l], ret_gn_b[l], bsz, seq)
        merged = _attention(qa, akv, vg, ret, att_sinks[l], bsz, seq)
        h, hb = _out_proj(merged, h, wob, ln1_g[l], ln1_b[l], alpha)
        h = _ffn(h, hb, wgb, wub, wdb, ln2_g[l], ln2_b[l], alpha)
    return h.reshape(bsz, seq, d)
```

```python
import functools
import math

import jax
import jax.numpy as jnp
from jax import lax
from jax.experimental import pallas as pl
from jax.experimental.pallas import tpu as pltpu

F32 = jnp.float32
BF16 = jnp.bfloat16

D_MODEL = 2048
RET_HEADS = 4
RET_DV = D_MODEL // RET_HEADS
RET_DK = RET_DV // 2
RET_CHUNK = 256
RET_THETA = 10000.0
ATT_HEAD_DIM = 64
ATT_HEADS = D_MODEL // ATT_HEAD_DIM
ATT_KV_HEADS = ATT_HEADS // 8
ATT_GROUP = ATT_HEADS // ATT_KV_HEADS
WINDOW = 128
ATT_BLOCK = WINDOW
ROPE_THETA = 500000.0
ROPE_DIM = ATT_HEAD_DIM // 4
LN_EPS = 1e-5
GN_EPS = 1e-5

LANES = 128
LOG2E = math.log2(math.e)

OFF_RQ = 0
OFF_RK = OFF_RQ + RET_HEADS * RET_DK
OFF_RV = OFF_RK + RET_HEADS * RET_DK
OFF_RG = OFF_RV + RET_HEADS * RET_DV
OFF_AQ = OFF_RG + RET_HEADS * RET_DV
OFF_AK = OFF_AQ + ATT_HEADS * ATT_HEAD_DIM
OFF_AV = OFF_AK + ATT_KV_HEADS * ATT_HEAD_DIM
OFF_GA = OFF_AV + ATT_KV_HEADS * ATT_HEAD_DIM
OFF_GB = OFF_GA + D_MODEL
KV_W = ATT_KV_HEADS * ATT_HEAD_DIM

MIB = 1024 * 1024


def _params(sem, vmem_mib, flags=None):
    return pltpu.CompilerParams(dimension_semantics=sem, vmem_limit_bytes=vmem_mib * MIB,
                                flags=flags)


def _sigmoid(v):
    return 0.5 * jnp.tanh(0.5 * v) + 0.5


def _layer_norm(z, g, b):
    mu = jnp.mean(z, axis=-1, keepdims=True)
    d = z - mu
    var = jnp.mean(d * d, axis=-1, keepdims=True)
    return d * lax.rsqrt(var + LN_EPS) * g + b


ROPE_PACK = ATT_HEAD_DIM // ROPE_DIM
PREP_ROWS = 64


def _prep_kernel(pos_ref, fr_ref, fa_ref, x_ref,
                 cr_ref, sr_ref, ca_ref, s1_ref, s2_ref, xb_ref, pos_scr):
    tm = pos_scr.shape[0]
    pos_t = pos_ref[...].astype(F32).T
    cols = [jnp.broadcast_to(pos_t[:, c:c + 1], (LANES, LANES)) for c in range(tm // LANES)]
    pos_scr[...] = jnp.concatenate(cols, axis=0)
    fr = fr_ref[...]

    def ret_rows(k, carry):
        rs = pl.ds(pl.multiple_of(k * PREP_ROWS, PREP_ROWS), PREP_ROWS)
        xb_ref[rs, :] = x_ref[rs, :].astype(BF16)
        ang_r = pos_scr[rs, :] * fr
        cr_ref[rs, :] = jnp.cos(ang_r)
        sr_ref[rs, :] = jnp.sin(ang_r)
        return carry

    lax.fori_loop(0, tm // PREP_ROWS, ret_rows, 0)
    rg = tm // ROPE_PACK
    per = rg // LANES
    lane = lax.broadcasted_iota(jnp.int32, (rg, LANES), 1)
    group = (lane & (ATT_HEAD_DIM - 1)) // ROPE_DIM
    pos_p = jnp.concatenate(cols[:per], axis=0)
    for g in range(1, ROPE_PACK):
        pos_p = jnp.where(group == g, jnp.concatenate(cols[g * per:(g + 1) * per], axis=0), pos_p)
    ang_p = pos_p * fa_ref[...]
    cp = jnp.cos(ang_p)
    sp = jnp.sin(ang_p)
    dp = lane & (ROPE_DIM - 1)
    half = ROPE_DIM // 2
    s1p = jnp.where(dp < half, -sp, 0.0)
    s2p = jnp.where(dp >= half, sp, 0.0)
    rotated = group == 0
    for g in range(ROPE_PACK):
        def spread(p, fill):
            a = pltpu.roll(p, LANES - ROPE_DIM * g, 1) if g else p
            return jnp.where(rotated, a, fill)

        rs = slice(g * rg, (g + 1) * rg)
        ca_ref[rs, :] = spread(cp, 1.0)
        s1_ref[rs, :] = spread(s1p, 0.0)
        s2_ref[rs, :] = spread(s2p, 0.0)


def _prep(positions, x2):
    t, dm = x2.shape
    tm = 1024
    pos_l = positions.reshape(t // LANES, LANES)
    inv_r = 1.0 / (RET_THETA ** jnp.linspace(0.0, 1.0, RET_DK // 2, dtype=F32))
    inv_a = 1.0 / (ROPE_THETA ** (jnp.arange(0, ROPE_DIM, 2, dtype=F32) / ROPE_DIM))
    fa = jnp.tile(inv_a, LANES // inv_a.size)
    row = pl.BlockSpec((tm, LANES), lambda i: (i, 0))
    xrow = pl.BlockSpec((tm, dm), lambda i: (i, 0))
    vec = pl.BlockSpec((1, LANES), lambda i: (0, 0))
    tab = jax.ShapeDtypeStruct((t, LANES), F32)
    return pl.pallas_call(
        _prep_kernel,
        grid=(t // tm,),
        in_specs=[pl.BlockSpec((tm // LANES, LANES), lambda i: (i, 0)), vec, vec, xrow],
        out_specs=[row] * 5 + [xrow],
        out_shape=[tab] * 5 + [jax.ShapeDtypeStruct((t, dm), BF16)],
        scratch_shapes=[pltpu.VMEM((tm, LANES), F32)],
        compiler_params=_params(("parallel",), 60),
        name="rotary_tables_xcast",
    )(pos_l, inv_r.reshape(1, LANES), fa.reshape(1, LANES), x2)


def _ep_plain(acc, j, *tabs):
    return acc


def _ep_swish(acc, j, *tabs):
    return acc * _sigmoid(acc)


def _ep_sigmoid(acc, j, *tabs):
    return _sigmoid(acc)


def _ep_ret_rotary(acc, j, cos_ref, sin_ref, *att_tabs):
    scale = jnp.where(j == 0, RET_DK ** -0.5, 1.0)
    cos = cos_ref[...] * scale
    sin = sin_ref[...] * scale
    half = RET_DK // 2
    outs = []
    for c0 in range(0, acc.shape[1], RET_DK):
        a1 = acc[:, c0:c0 + half]
        a2 = acc[:, c0 + half:c0 + RET_DK]
        outs += [a1 * cos - a2 * sin, a2 * cos + a1 * sin]
    return jnp.concatenate(outs, axis=-1)


def _rope_lanes(a, ca, s1, s2):
    outs = []
    for c0 in range(0, a.shape[1], LANES):
        ac = a[:, c0:c0 + LANES]
        up = pltpu.roll(ac, LANES - ROPE_DIM // 2, 1)
        dn = pltpu.roll(ac, ROPE_DIM // 2, 1)
        outs.append(ac * ca + up * s1 + dn * s2)
    return jnp.concatenate(outs, axis=-1)


def _ep_att_q(acc, j, cos_ref, sin_ref, ca_ref, s1_ref, s2_ref):
    c = (ATT_HEAD_DIM ** -0.5) * LOG2E
    return _rope_lanes(acc, ca_ref[...] * c, s1_ref[...] * c, s2_ref[...] * c)


def _ep_att_kv(acc, j, ca_ref, s1_ref, s2_ref):
    k = _rope_lanes(acc[:, :KV_W], ca_ref[...], s1_ref[...], s2_ref[...])
    return jnp.concatenate([k, acc[:, KV_W:]], axis=-1)


def _proj_kernel(segments, n_tab, n_ride, x_ref, w_ref, *rest):
    tab_refs = rest[:n_tab]
    ride_refs = rest[n_tab:n_tab + n_ride]
    o_ref = rest[n_tab + n_ride]
    ride_out_refs = rest[n_tab + n_ride + 1:n_tab + 2 * n_ride + 1]
    wb_ref = rest[-1]
    j = pl.program_id(0)

    @pl.when(pl.program_id(1) == 0)
    def _cast_weights():
        wb_ref[...] = w_ref[...].astype(BF16)

    for j0, j1, epilogue, rides in segments:
        @pl.when((j >= j0) & (j < j1))
        def _segment():
            for r in rides:
                ride_out_refs[r][...] = ride_refs[r][...].astype(BF16)
            acc = jnp.dot(x_ref[...], wb_ref[...], preferred_element_type=F32)
            o_ref[...] = epilogue(acc, j - j0, *tab_refs).astype(o_ref.dtype)


def _proj_multi(xb, w, segs, tn, tables=(), rides=(), name="in_proj"):
    t, k = xb.shape
    tm = 1024
    nt = t // tm
    j0s, bounds = [], 0
    for col0, width, _ in segs:
        assert col0 % LANES == 0 and width % tn == 0
        j0s.append(bounds)
        bounds += width // tn
    n_tiles = bounds

    def w_col(j):
        col = segs[0][0] + j * tn
        for (col0, _, _), j0 in zip(segs[1:], j0s[1:]):
            col = jnp.where(j >= j0, col0 + (j - j0) * tn, col)
        return pl.multiple_of(col, LANES)

    tab = pl.BlockSpec((tm, LANES), lambda j, i: (i, 0))
    in_specs = [pl.BlockSpec((tm, k), lambda j, i: (i, 0)),
                pl.BlockSpec((pl.Element(k), pl.Element(tn)), lambda j, i: (0, w_col(j)))]
    in_specs += [tab] * len(tables)
    out_specs = [pl.BlockSpec((tm, tn), lambda j, i: (i, j))]
    out_shape = [jax.ShapeDtypeStruct((t, n_tiles * tn), BF16)]
    ride_steps = []
    for arr, first, n_slabs in rides:
        slab = arr.shape[0] // n_slabs
        assert slab * n_slabs == arr.shape[0] and slab % 16 == 0 and first + n_slabs <= n_tiles * nt
        spec = pl.BlockSpec(
            (slab, arr.shape[1]),
            lambda j, i, first=first, n=n_slabs: (jnp.clip(j * nt + i - first, 0, n - 1), 0))
        in_specs.append(spec)
        out_specs.append(spec)
        out_shape.append(jax.ShapeDtypeStruct(arr.shape, BF16))
        ride_steps.append((first, first + n_slabs))
    segments = []
    for (_, width, epilogue), j0 in zip(segs, j0s):
        j1 = j0 + width // tn
        active = tuple(r for r, (s0, s1) in enumerate(ride_steps) if s0 < j1 * nt and s1 > j0 * nt)
        segments.append((j0, j1, epilogue, active))
    outs = pl.pallas_call(
        functools.partial(_proj_kernel, tuple(segments), len(tables), len(rides)),
        grid=(n_tiles, nt),
        in_specs=in_specs,
        out_specs=out_specs,
        out_shape=out_shape,
        scratch_shapes=[pltpu.VMEM((k, tn), BF16)],
        compiler_params=_params(("arbitrary", "arbitrary"), 56),
        name=name,
    )(xb, w, *tables, *[arr for arr, _, _ in rides])
    return outs[0] if not rides else outs


def _retention_kernel(q_ref, k_ref, v_ref, gs_ref, ga_ref, gng_ref, gnb_ref,
                      o_ref, state_ref, dmask_ref, qdec_ref, kdec_ref, cdec_ref):
    c = RET_CHUNK

    @pl.when(pl.program_id(1) == 0)
    def _init():
        state_ref[...] = jnp.zeros_like(state_ref)
        ii = lax.broadcasted_iota(jnp.int32, (c, c), 0)
        jj = lax.broadcasted_iota(jnp.int32, (c, c), 1)
        diff = (ii - jj).astype(F32)
        idx = lax.broadcasted_iota(jnp.int32, (c, RET_DK), 0).astype(F32)
        for h in range(RET_HEADS):
            def log_gamma(shape):
                return jnp.log(1.0 - jnp.exp2(-5.0 - jnp.full(shape, float(h), F32)))

            dmask_ref[h] = jnp.where(
                diff >= 0.0, jnp.exp(log_gamma((c, c)) * jnp.maximum(diff, 0.0)), 0.0)
            lgl = log_gamma((c, RET_DK))
            qdec_ref[h] = jnp.exp(lgl * (idx + 1.0))
            kdec_ref[h] = jnp.exp(lgl * (c - 1.0 - idx))
            cdec_ref[h] = jnp.exp(log_gamma((8, LANES)) * float(c))

    for r0, h in ((r0, h) for r0 in range(0, q_ref.shape[0], c) for h in range(RET_HEADS)):
        rs = slice(r0, r0 + c)
        ksl = slice(h * RET_DK, (h + 1) * RET_DK)
        vsl = slice(h * RET_DV, (h + 1) * RET_DV)
        qb = q_ref[rs, ksl]
        kb = k_ref[rs, ksl]
        v = v_ref[rs, vsl]
        qd = (qb.astype(F32) * qdec_ref[h]).astype(BF16)
        kd = (kb.astype(F32) * kdec_ref[h]).astype(BF16)

        s = lax.dot_general(qb, kb, (((1,), (1,)), ((), ())), preferred_element_type=F32)
        s = s * dmask_ref[h]
        inner = jnp.dot(s.astype(BF16), v, preferred_element_type=F32)
        state = state_ref[h]
        cross = jnp.dot(qd, state.astype(BF16), preferred_element_type=F32)
        kv = lax.dot_general(kd, v, (((0,), (0,)), ((), ())), preferred_element_type=F32)
        state_ref[h] = state * cdec_ref[h, 0:1, 0:1] + kv

        y = inner + cross
        mu = jnp.mean(y, axis=-1, keepdims=True)
        d = y - mu
        var = jnp.mean(d * d, axis=-1, keepdims=True)
        yn = d * lax.rsqrt(var + GN_EPS) * gng_ref[:, vsl] + gnb_ref[:, vsl]
        gate = ga_ref[rs, vsl].astype(F32) * gs_ref[rs, vsl].astype(F32)
        o_ref[rs, vsl] = (gate * yn).astype(o_ref.dtype)


RET_STEP_CHUNKS = 2


def _retention(qa, vg, gn_g, gn_b, bsz, seq):
    t = bsz * seq
    c = RET_CHUNK
    rows_per_step = RET_STEP_CHUNKS * c
    nc = seq // rows_per_step
    nh = RET_HEADS

    def rows(width, col):
        return pl.BlockSpec((rows_per_step, width), lambda b, n: (b * nc + n, col))

    gn = pl.BlockSpec((1, D_MODEL), lambda b, n: (0, 0))
    return pl.pallas_call(
        _retention_kernel,
        grid=(bsz, nc),
        in_specs=[rows(nh * RET_DK, 0), rows(nh * RET_DK, 1), rows(D_MODEL, 0), rows(D_MODEL, 1),
                  rows(D_MODEL, 2), gn, gn],
        out_specs=rows(D_MODEL, 0),
        out_shape=jax.ShapeDtypeStruct((t, D_MODEL), BF16),
        scratch_shapes=[pltpu.VMEM((nh, RET_DK, RET_DV), F32), pltpu.VMEM((nh, c, c), F32),
                        pltpu.VMEM((nh, c, RET_DK), F32), pltpu.VMEM((nh, c, RET_DK), F32),
                        pltpu.VMEM((nh, 8, LANES), F32)],
        compiler_params=_params(("parallel", "arbitrary"), 32),
        name="retention",
    )(qa, qa, vg, vg, vg, gn_g.reshape(1, -1), gn_b.reshape(1, -1))


SINK_LANES = 8


def _pair_blocks(prev, cur, g):
    c0 = (g // 2) * LANES
    kk = jnp.concatenate([prev[:, c0:c0 + LANES], cur[:, c0:c0 + LANES]], axis=0).astype(F32)
    lane = lax.broadcasted_iota(jnp.int32, kk.shape, 1)
    row = lax.broadcasted_iota(jnp.int32, kk.shape, 0)
    hd = ATT_HEAD_DIM
    own = jnp.where(((lane >= hd) if g % 2 else (lane < hd)) & (row != 0), kk, 0.0)
    other = pltpu.roll(own, hd, 1)
    lo, hi = (other, own) if g % 2 else (own, other)
    return lo.astype(BF16), hi.astype(BF16)


def _attention_consts(sink_ref, qaug_ref, kaug_ref, vones_ref):
    nk = 2 * ATT_BLOCK
    row_q = lax.broadcasted_iota(jnp.int32, qaug_ref.shape, 0)
    lane_q = lax.broadcasted_iota(jnp.int32, qaug_ref.shape, 1)
    pair_of_lane = (lane_q & (SINK_LANES - 1)) >> 1
    onehot = (lane_q < 2 * SINK_LANES) & (pair_of_lane == row_q // ATT_BLOCK)
    qaug_ref[...] = jnp.where(onehot, 1.0, 0.0).astype(BF16)
    row_k = lax.broadcasted_iota(jnp.int32, vones_ref.shape, 0)
    lane_k = lax.broadcasted_iota(jnp.int32, vones_ref.shape, 1)
    vones_ref[...] = jnp.where((row_k < nk) == (lane_k < ATT_HEAD_DIM), 1.0, 0.0).astype(BF16)
    for g in range(ATT_KV_HEADS):
        vals = jnp.zeros(vones_ref.shape, F32)
        for j in range(ATT_GROUP):
            sk = sink_ref[g * ATT_GROUP + j] * LOG2E
            vals = jnp.where((lane_k == j) & (row_k == (j % 2) * nk), sk, vals)
        hi = vals.astype(BF16).astype(F32)
        kaug_ref[g] = (hi + pltpu.roll(vals - hi, SINK_LANES, 1)).astype(BF16)


def _attention_kernel(last, sink_ref, q_ref, kp_ref, kc_ref, vp_ref, vc_ref, gb_ref, ret_ref, o_ref,
                      qaug_ref, kaug_ref, vones_ref, *p_refs):
    i = pl.program_id(1)
    bq = ATT_BLOCK
    nk = 2 * bq
    pairs = ATT_GROUP // 2

    @pl.when((pl.program_id(0) == 0) & (i == 0))
    def _init():
        _attention_consts(sink_ref, qaug_ref, kaug_ref, vones_ref)

    def scores(slot):
        kp, kc = kp_ref[...], kc_ref[...]
        qi = lax.broadcasted_iota(jnp.int32, (bq, nk), 0)
        kj = lax.broadcasted_iota(jnp.int32, (bq, nk), 1)
        dist = qi + bq - kj
        kmin = jnp.where(i > 0, 0, bq)
        valid = ((dist >= 0) & (dist < WINDOW) & (kj >= kmin)) | (kj == 0)
        bias = jnp.where(valid, 0.0, -jnp.inf)
        qaug = qaug_ref[...]
        for g in range(ATT_KV_HEADS):
            k_lo, k_hi = _pair_blocks(kp, kc, g)
            kfull = jnp.concatenate([jnp.concatenate([k_lo, k_hi], axis=0), kaug_ref[g]], axis=1)
            col0 = g * ATT_GROUP * ATT_HEAD_DIM
            qp = jnp.concatenate(
                [q_ref[:, col0 + p * LANES:col0 + (p + 1) * LANES] for p in range(pairs)], axis=0)
            qfull = jnp.concatenate([qp, qaug], axis=1)
            s = lax.dot_general(qfull, kfull, (((1,), (1,)), ((), ())),
                                preferred_element_type=F32)
            for p in range(pairs):
                for e in range(2):
                    sj = s[p * bq:(p + 1) * bq, e * nk:(e + 1) * nk] + bias
                    m = jnp.max(sj, axis=-1, keepdims=True)
                    p_refs[slot][g, p * bq:(p + 1) * bq, e * nk:(e + 1) * nk] = (
                        jnp.exp2(sj - m).astype(BF16))

    def values(slot):
        vp, vc = vp_ref[...], vc_ref[...]
        vones = vones_ref[...]
        for g in range(ATT_KV_HEADS):
            v_lo, v_hi = _pair_blocks(vp, vc, g)
            vfull = jnp.concatenate([jnp.concatenate([v_lo, v_hi], axis=0), vones], axis=1)
            o = jnp.dot(p_refs[slot][g], vfull, preferred_element_type=F32)
            col0 = g * ATT_GROUP * ATT_HEAD_DIM
            for p in range(pairs):
                num = o[p * bq:(p + 1) * bq, :LANES]
                den = o[p * bq:(p + 1) * bq, LANES:]
                c1 = col0 + p * LANES
                gate = gb_ref[:, c1:c1 + LANES].astype(F32)
                ret = ret_ref[:, c1:c1 + LANES].astype(F32)
                o_ref[:, c1:c1 + LANES] = (gate * num / den + ret).astype(o_ref.dtype)

    @pl.when(i == 0)
    def _first():
        scores(0)

    for par in (0, 1):
        @pl.when((i > 0) & (i < last) & (lax.rem(i, 2) == par))
        def _steady():
            values(1 - par)
            scores(par)

    @pl.when(i == last)
    def _last():
        values((last - 1) % 2)


def _attention(qa, akv, vg, ret, sinks, bsz, seq):
    t = bsz * seq
    bq = ATT_BLOCK
    nb = seq // bq
    rows = (ATT_GROUP // 2) * bq

    def spec(width, col, lag, back):
        def index(b, i):
            blk = jnp.clip(i - lag, 0, nb - 1)
            return (b * nb + jnp.maximum(blk - back, 0), col)
        return pl.BlockSpec((bq, width), index)

    return pl.pallas_call(
        functools.partial(_attention_kernel, nb),
        grid=(bsz, nb + 1),
        in_specs=[pl.BlockSpec(memory_space=pltpu.SMEM), spec(D_MODEL, 1, 0, 0),
                  spec(KV_W, 0, 0, 1), spec(KV_W, 0, 0, 0),
                  spec(KV_W, 1, 1, 1), spec(KV_W, 1, 1, 0), spec(D_MODEL, 3, 1, 0),
                  spec(D_MODEL, 0, 1, 0)],
        out_specs=spec(D_MODEL, 0, 1, 0),
        out_shape=jax.ShapeDtypeStruct((t, D_MODEL), BF16),
        scratch_shapes=[pltpu.VMEM((rows, LANES), BF16),
                        pltpu.VMEM((ATT_KV_HEADS, 2 * 2 * bq, LANES), BF16),
                        pltpu.VMEM((2 * 2 * bq, LANES), BF16),
                        pltpu.VMEM((ATT_KV_HEADS, rows, 2 * 2 * bq), BF16),
                        pltpu.VMEM((ATT_KV_HEADS, rows, 2 * 2 * bq), BF16)],
        compiler_params=_params(("arbitrary", "arbitrary"), 60),
        name="attention",
    )(sinks, qa, akv, akv, akv, akv, vg, ret)


OUT_PROJ_ROW_CHUNKS = 4


def _out_proj_kernel(alpha, m_ref, x_ref, w_ref, g_ref, b_ref, h_ref, hb_ref):
    rows = h_ref.shape[0] // OUT_PROJ_ROW_CHUNKS
    for c in range(OUT_PROJ_ROW_CHUNKS):
        rs = slice(c * rows, (c + 1) * rows)
        y = jnp.dot(m_ref[rs, :], w_ref[...], preferred_element_type=F32)
        h = _layer_norm(alpha * x_ref[rs, :] + y, g_ref[...], b_ref[...])
        h_ref[rs, :] = h
        hb_ref[rs, :] = h.astype(BF16)


def _out_proj(merged, x2, wb, ln_g, ln_b, alpha):
    t, d = x2.shape
    tm = 512
    row = pl.BlockSpec((tm, d), lambda i: (i, 0))
    vec = pl.BlockSpec((1, d), lambda i: (0, 0))
    return pl.pallas_call(
        functools.partial(_out_proj_kernel, alpha),
        grid=(t // tm,),
        in_specs=[row, row, pl.BlockSpec((d, d), lambda i: (0, 0)), vec, vec],
        out_specs=[row, row],
        out_shape=[jax.ShapeDtypeStruct((t, d), F32), jax.ShapeDtypeStruct((t, d), BF16)],
        compiler_params=_params(("parallel",), 48),
        name="out_proj_ln",
    )(merged, x2, wb, ln_g.reshape(1, d), ln_b.reshape(1, d))


def _ffn_kernel(alpha, hb_ref, h_hbm, wg_ref, wu_ref, wd_ref, g_ref, b_ref, o_ref, hres_ref, sem):
    i = pl.program_id(0)
    f = pl.program_id(1)
    last = pl.num_programs(1) - 1
    tm = o_ref.shape[0]
    res_copy = pltpu.make_async_copy(h_hbm.at[pl.ds(i * tm, tm), :], hres_ref, sem)

    def tile():
        hb = hb_ref[...]
        gt = jnp.dot(hb, wg_ref[...], preferred_element_type=F32)
        ut = jnp.dot(hb, wu_ref[...], preferred_element_type=F32)
        a = (gt * _sigmoid(gt) * ut).astype(BF16)
        return jnp.dot(a, wd_ref[...], preferred_element_type=F32)

    @pl.when(f == 0)
    def _first():
        res_copy.start()
        o_ref[...] = tile()

    @pl.when((f > 0) & (f < last))
    def _middle():
        o_ref[...] += tile()

    @pl.when(f == last)
    def _last():
        res_copy.wait()
        z = alpha * hres_ref[...] + (o_ref[...] + tile())
        o_ref[...] = _layer_norm(z, g_ref[...], b_ref[...])


def _ffn(h1, h1b, wg, wu, wd, ln_g, ln_b, alpha):
    t, d = h1.shape
    dff = wg.shape[1]
    tm, tf = 1024, 512
    row = pl.BlockSpec((tm, d), lambda i, f: (i, 0))
    vec = pl.BlockSpec((1, d), lambda i, f: (0, 0))
    return pl.pallas_call(
        functools.partial(_ffn_kernel, alpha),
        grid=(t // tm, dff // tf),
        in_specs=[row,
                  pl.BlockSpec(memory_space=pl.ANY),
                  pl.BlockSpec((d, tf), lambda i, f: (0, f)),
                  pl.BlockSpec((d, tf), lambda i, f: (0, f)),
                  pl.BlockSpec((tf, d), lambda i, f: (f, 0)),
                  vec, vec],
        out_specs=row,
        out_shape=jax.ShapeDtypeStruct((t, d), F32),
        scratch_shapes=[pltpu.VMEM((tm, d), F32), pltpu.SemaphoreType.DMA(())],
        compiler_params=_params(("parallel", "arbitrary"), 60),
        name="ffn_ln",
    )(h1b, h1, wg, wu, wd, ln_g.reshape(1, d), ln_b.reshape(1, d))


def kernel(x, positions, w_in, ret_gn_g, ret_gn_b, att_sinks, w_out, ln1_g, ln1_b,
           w_gate, w_up, w_down, ln2_g, ln2_b):
    bsz, seq, d = x.shape
    depth = w_in.shape[0]
    alpha = (2.0 * depth) ** 0.25
    h = x.reshape(bsz * seq, d)
    cos_r, sin_r, ca, s1, s2, hb = _prep(positions, h)
    for l in range(depth):
        if l > 0:
            hb = h.astype(BF16)
        w = w_in[l]
        steps = 4 * (bsz * seq // 1024)
        qa, wdb = _proj_multi(
            hb, w, [(OFF_RQ, OFF_RV - OFF_RQ, _ep_ret_rotary), (OFF_AQ, D_MODEL, _ep_att_q)], 1024,
            (cos_r, sin_r, ca, s1, s2), [(w_down[l], 0, steps)], "in_proj_qk")
        vg, wgb, wub = _proj_multi(
            hb, w, [(OFF_RV, D_MODEL, _ep_plain), (OFF_RG, D_MODEL, _ep_swish),
                    (OFF_GA, 2 * D_MODEL, _ep_sigmoid)], 1024,
            (), [(w_gate[l], 0, steps), (w_up[l], steps, steps)], "in_proj_vg")
        akv, wob = _proj_multi(hb, w, [(OFF_AK, 2 * KV_W, _ep_att_kv)], 2 * KV_W, (ca, s1, s2),
                               [(w_out[l], 0, steps // 4)], "in_proj_att_kv")
        ret = _retention(qa, vg, ret_gn_g[l], ret_gn_b[l], bsz, seq)
        merged = _attention(qa, akv, vg, ret, att_sinks[l], bsz, seq)
        h, hb = _out_proj(merged, h, wob, ln1_g[l], ln1_b[l], alpha)
        h = _ffn(h, hb, wgb, wub, wdb, ln2_g[l], ln2_b[l], alpha)
    return h.reshape(bsz, seq, d)
```

```python
import functools
import math

import jax
import jax.numpy as jnp
from jax import lax
from jax.experimental import pallas as pl
from jax.experimental.pallas import tpu as pltpu

F32 = jnp.float32
BF16 = jnp.bfloat16

D_MODEL = 2048
RET_HEADS = 4
RET_DV = D_MODEL // RET_HEADS
RET_DK = RET_DV // 2
RET_CHUNK = 256
RET_THETA = 10000.0
ATT_HEAD_DIM = 64
ATT_HEADS = D_MODEL // ATT_HEAD_DIM
ATT_KV_HEADS = ATT_HEADS // 8
ATT_GROUP = ATT_HEADS // ATT_KV_HEADS
WINDOW = 128
ATT_BLOCK = WINDOW
ROPE_THETA = 500000.0
ROPE_DIM = ATT_HEAD_DIM // 4
LN_EPS = 1e-5
GN_EPS = 1e-5

LANES = 128
LOG2E = math.log2(math.e)

OFF_RQ = 0
OFF_RK = OFF_RQ + RET_HEADS * RET_DK
OFF_RV = OFF_RK + RET_HEADS * RET_DK
OFF_RG = OFF_RV + RET_HEADS * RET_DV
OFF_AQ = OFF_RG + RET_HEADS * RET_DV
OFF_AK = OFF_AQ + ATT_HEADS * ATT_HEAD_DIM
OFF_AV = OFF_AK + ATT_KV_HEADS * ATT_HEAD_DIM
OFF_GA = OFF_AV + ATT_KV_HEADS * ATT_HEAD_DIM
OFF_GB = OFF_GA + D_MODEL
KV_W = ATT_KV_HEADS * ATT_HEAD_DIM

TABLES_TM = 1024
PROJ_TM, PROJ_TN = 1024, 1024
OUT_TM = 512
FFN_TM, FFN_TF = 1024, 512
RET_STEP_CHUNKS = 2
OUT_PROJ_ROW_CHUNKS = 2
VMEM_MIB = {"tables": 60, "in_proj": 56, "retention": 32, "attention": 32, "out_proj": 48,
            "ffn": 60}


def _params(sem, call):
    return pltpu.CompilerParams(dimension_semantics=sem,
                                vmem_limit_bytes=VMEM_MIB[call] * 1024 * 1024)


def _sigmoid(v):
    return 0.5 * jnp.tanh(0.5 * v) + 0.5


def _layer_norm(z, g, b):
    mu = jnp.mean(z, axis=-1, keepdims=True)
    d = z - mu
    var = jnp.mean(d * d, axis=-1, keepdims=True)
    return d * lax.rsqrt(var + LN_EPS) * g + b


def _prep_kernel(pos_ref, fr_ref, fa_ref, x_ref, cr_ref, sr_ref, ca_ref, s1_ref, s2_ref, xb_ref):
    xb_ref[...] = x_ref[...].astype(BF16)
    pos_t = pos_ref[...].astype(F32).T
    pos = jnp.concatenate([jnp.broadcast_to(pos_t[:, c:c + 1], (LANES, LANES))
                           for c in range(pos_t.shape[1])], axis=0)
    ang_r = pos * fr_ref[...]
    cr_ref[...] = jnp.cos(ang_r)
    sr_ref[...] = jnp.sin(ang_r)
    ang_a = pos * fa_ref[...]
    sa = jnp.sin(ang_a)
    d = lax.broadcasted_iota(jnp.int32, ang_a.shape, 1) & (ATT_HEAD_DIM - 1)
    half = ROPE_DIM // 2
    ca_ref[...] = jnp.cos(ang_a)
    s1_ref[...] = jnp.where(d < half, -sa, 0.0)
    s2_ref[...] = jnp.where((d >= half) & (d < ROPE_DIM), sa, 0.0)


def _prep(positions, x2):
    t, dm = x2.shape
    tm = TABLES_TM
    pos_l = positions.reshape(t // LANES, LANES)
    inv_r = 1.0 / (RET_THETA ** jnp.linspace(0.0, 1.0, RET_DK // 2, dtype=F32))
    inv_a = 1.0 / (ROPE_THETA ** (jnp.arange(0, ROPE_DIM, 2, dtype=F32) / ROPE_DIM))
    d = jnp.arange(LANES) % ATT_HEAD_DIM
    fa = jnp.where(d < ROPE_DIM, jnp.tile(inv_a, LANES // inv_a.size), 0.0).astype(F32)
    row = pl.BlockSpec((tm, LANES), lambda i: (i, 0))
    xrow = pl.BlockSpec((tm, dm), lambda i: (i, 0))
    vec = pl.BlockSpec((1, LANES), lambda i: (0, 0))
    tab = jax.ShapeDtypeStruct((t, LANES), F32)
    return pl.pallas_call(
        _prep_kernel,
        grid=(t // tm,),
        in_specs=[pl.BlockSpec((tm // LANES, LANES), lambda i: (i, 0)), vec, vec, xrow],
        out_specs=[row] * 5 + [xrow],
        out_shape=[tab] * 5 + [jax.ShapeDtypeStruct((t, dm), BF16)],
        compiler_params=_params(("parallel",), "tables"),
        name="rotary_tables_xcast",
    )(pos_l, inv_r.reshape(1, LANES), fa.reshape(1, LANES), x2)


def _ep_plain(acc, j, *tabs):
    return acc


def _ep_swish(acc, j, *tabs):
    return acc * _sigmoid(acc)


def _ep_sigmoid(acc, j, *tabs):
    return _sigmoid(acc)


def _ep_ret_rotary(acc, j, cos_ref, sin_ref, *att_tabs):
    scale = jnp.where(j == 0, RET_DK ** -0.5, 1.0)
    cos = cos_ref[...] * scale
    sin = sin_ref[...] * scale
    half = RET_DK // 2
    outs = []
    for c0 in range(0, acc.shape[1], RET_DK):
        a1 = acc[:, c0:c0 + half]
        a2 = acc[:, c0 + half:c0 + RET_DK]
        outs += [a1 * cos - a2 * sin, a2 * cos + a1 * sin]
    return jnp.concatenate(outs, axis=-1)


def _rope_lanes(a, ca, s1, s2):
    outs = []
    for c0 in range(0, a.shape[1], LANES):
        ac = a[:, c0:c0 + LANES]
        up = pltpu.roll(ac, LANES - ROPE_DIM // 2, 1)
        dn = pltpu.roll(ac, ROPE_DIM // 2, 1)
        outs.append(ac * ca + up * s1 + dn * s2)
    return jnp.concatenate(outs, axis=-1)


def _ep_att_q(acc, j, cos_ref, sin_ref, ca_ref, s1_ref, s2_ref):
    c = (ATT_HEAD_DIM ** -0.5) * LOG2E
    return _rope_lanes(acc, ca_ref[...] * c, s1_ref[...] * c, s2_ref[...] * c)


def _ep_att_kv(acc, j, ca_ref, s1_ref, s2_ref):
    k = _rope_lanes(acc[:, :KV_W], ca_ref[...], s1_ref[...], s2_ref[...])
    return jnp.concatenate([k, acc[:, KV_W:]], axis=-1)


def _proj_kernel(segments, n_tab, n_ride, x_ref, w_ref, *rest):
    tab_refs = rest[:n_tab]
    ride_refs = rest[n_tab:n_tab + n_ride]
    o_ref = rest[n_tab + n_ride]
    ride_out_refs = rest[n_tab + n_ride + 1:n_tab + 2 * n_ride + 1]
    wb_ref = rest[-1]
    j = pl.program_id(0)

    @pl.when(pl.program_id(1) == 0)
    def _cast_weights():
        wb_ref[...] = w_ref[...].astype(BF16)

    for j0, j1, epilogue, rides in segments:
        @pl.when((j >= j0) & (j < j1))
        def _segment():
            for r in rides:
                src, dst = ride_refs[r], ride_out_refs[r]
                if len(dst.shape) == 2:
                    dst[...] = src[...].astype(BF16)
                else:
                    ct = dst.shape[2]
                    for c in range(dst.shape[0]):
                        dst[c] = src[:, c * ct:(c + 1) * ct].astype(BF16)
            acc = jnp.dot(x_ref[...], wb_ref[...], preferred_element_type=F32)
            o_ref[...] = epilogue(acc, j - j0, *tab_refs).astype(o_ref.dtype)


def _proj_multi(xb, w, segs, tn, tables=(), rides=(), name="in_proj"):
    t, k = xb.shape
    tm = PROJ_TM
    nt = t // tm
    j0s, bounds = [], 0
    for col0, width, _ in segs:
        assert col0 % LANES == 0 and width % tn == 0
        j0s.append(bounds)
        bounds += width // tn
    n_tiles = bounds

    def w_col(j):
        col = segs[0][0] + j * tn
        for (col0, _, _), j0 in zip(segs[1:], j0s[1:]):
            col = jnp.where(j >= j0, col0 + (j - j0) * tn, col)
        return pl.multiple_of(col, LANES)

    tab = pl.BlockSpec((tm, LANES), lambda j, i: (i, 0))
    in_specs = [pl.BlockSpec((tm, k), lambda j, i: (i, 0)),
                pl.BlockSpec((pl.Element(k), pl.Element(tn)), lambda j, i: (0, w_col(j)))]
    in_specs += [tab] * len(tables)
    out_specs = [pl.BlockSpec((tm, tn), lambda j, i: (i, j))]
    out_shape = [jax.ShapeDtypeStruct((t, n_tiles * tn), BF16)]
    ride_steps = []
    for arr, first, n_slabs, col_tile in rides:
        rows_, cols_ = arr.shape
        slab = rows_ // n_slabs
        assert slab * n_slabs == rows_ and slab % 16 == 0 and first + n_slabs <= n_tiles * nt

        def slab_of(j, i, first=first, n=n_slabs):
            return jnp.clip(j * nt + i - first, 0, n - 1)

        in_specs.append(pl.BlockSpec((slab, cols_), lambda j, i, f=slab_of: (f(j, i), 0)))
        if col_tile is None:
            out_specs.append(pl.BlockSpec((slab, cols_), lambda j, i, f=slab_of: (f(j, i), 0)))
            out_shape.append(jax.ShapeDtypeStruct(arr.shape, BF16))
        else:
            assert cols_ % col_tile == 0
            out_specs.append(pl.BlockSpec((cols_ // col_tile, slab, col_tile),
                                          lambda j, i, f=slab_of: (0, f(j, i), 0)))
            out_shape.append(jax.ShapeDtypeStruct((cols_ // col_tile, rows_, col_tile), BF16))
        ride_steps.append((first, first + n_slabs))
    segments = []
    for (_, width, epilogue), j0 in zip(segs, j0s):
        j1 = j0 + width // tn
        active = tuple(r for r, (s0, s1) in enumerate(ride_steps) if s0 < j1 * nt and s1 > j0 * nt)
        segments.append((j0, j1, epilogue, active))
    outs = pl.pallas_call(
        functools.partial(_proj_kernel, tuple(segments), len(tables), len(rides)),
        grid=(n_tiles, nt),
        in_specs=in_specs,
        out_specs=out_specs,
        out_shape=out_shape,
        scratch_shapes=[pltpu.VMEM((k, tn), BF16)],
        compiler_params=_params(("arbitrary", "arbitrary"), "in_proj"),
        name=name,
    )(xb, w, *tables, *[ride[0] for ride in rides])
    return outs[0] if not rides else outs


def _retention_kernel(q_ref, k_ref, v_ref, gs_ref, ga_ref, gng_ref, gnb_ref,
                      o_ref, state_ref, dmask_ref, qdec_ref, kdec_ref, cdec_ref):
    c = RET_CHUNK

    @pl.when(pl.program_id(1) == 0)
    def _init():
        state_ref[...] = jnp.zeros_like(state_ref)
        ii = lax.broadcasted_iota(jnp.int32, (c, c), 0)
        jj = lax.broadcasted_iota(jnp.int32, (c, c), 1)
        diff = (ii - jj).astype(F32)
        idx = lax.broadcasted_iota(jnp.int32, (c, RET_DK), 0).astype(F32)
        for h in range(RET_HEADS):
            def log_gamma(shape):
                return jnp.log(1.0 - jnp.exp2(-5.0 - jnp.full(shape, float(h), F32)))

            dmask_ref[h] = jnp.where(
                diff >= 0.0, jnp.exp(log_gamma((c, c)) * jnp.maximum(diff, 0.0)), 0.0)
            lgl = log_gamma((c, RET_DK))
            qdec_ref[h] = jnp.exp(lgl * (idx + 1.0))
            kdec_ref[h] = jnp.exp(lgl * (c - 1.0 - idx))
            cdec_ref[h] = jnp.exp(log_gamma((8, LANES)) * float(c))

    for r0, h in ((r0, h) for r0 in range(0, q_ref.shape[0], c) for h in range(RET_HEADS)):
        rs = slice(r0, r0 + c)
        ksl = slice(h * RET_DK, (h + 1) * RET_DK)
        vsl = slice(h * RET_DV, (h + 1) * RET_DV)
        qb = q_ref[rs, ksl]
        kb = k_ref[rs, ksl]
        v = v_ref[rs, vsl]
        qd = (qb.astype(F32) * qdec_ref[h]).astype(BF16)
        kd = (kb.astype(F32) * kdec_ref[h]).astype(BF16)

        s = lax.dot_general(qb, kb, (((1,), (1,)), ((), ())), preferred_element_type=F32)
        s = s * dmask_ref[h]
        inner = jnp.dot(s.astype(BF16), v, preferred_element_type=F32)
        state = state_ref[h]
        cross = jnp.dot(qd, state.astype(BF16), preferred_element_type=F32)
        kv = lax.dot_general(kd, v, (((0,), (0,)), ((), ())), preferred_element_type=F32)
        state_ref[h] = state * cdec_ref[h, 0:1, 0:1] + kv

        y = inner + cross
        mu = jnp.mean(y, axis=-1, keepdims=True)
        d = y - mu
        var = jnp.mean(d * d, axis=-1, keepdims=True)
        yn = d * lax.rsqrt(var + GN_EPS) * gng_ref[:, vsl] + gnb_ref[:, vsl]
        gate = ga_ref[rs, vsl].astype(F32) * gs_ref[rs, vsl].astype(F32)
        o_ref[rs, vsl] = (gate * yn).astype(o_ref.dtype)


def _retention(qa, vg, gn_g, gn_b, bsz, seq):
    t = bsz * seq
    c = RET_CHUNK
    rows_per_step = RET_STEP_CHUNKS * c
    nc = seq // rows_per_step
    nh = RET_HEADS

    def rows(width, col):
        return pl.BlockSpec((rows_per_step, width), lambda b, n: (b * nc + n, col))

    gn = pl.BlockSpec((1, D_MODEL), lambda b, n: (0, 0))
    return pl.pallas_call(
        _retention_kernel,
        grid=(bsz, nc),
        in_specs=[rows(nh * RET_DK, 0), rows(nh * RET_DK, 1), rows(D_MODEL, 0), rows(D_MODEL, 1),
                  rows(D_MODEL, 2), gn, gn],
        out_specs=rows(D_MODEL, 0),
        out_shape=jax.ShapeDtypeStruct((t, D_MODEL), BF16),
        scratch_shapes=[pltpu.VMEM((nh, RET_DK, RET_DV), F32), pltpu.VMEM((nh, c, c), F32),
                        pltpu.VMEM((nh, c, RET_DK), F32), pltpu.VMEM((nh, c, RET_DK), F32),
                        pltpu.VMEM((nh, 8, LANES), F32)],
        compiler_params=_params(("parallel", "arbitrary"), "retention"),
        name="retention",
    )(qa, qa, vg, vg, vg, gn_g.reshape(1, -1), gn_b.reshape(1, -1))


SINK_LANES = 8


def _pair_blocks(prev, cur, g):
    c0 = (g // 2) * LANES
    kk = jnp.concatenate([prev[:, c0:c0 + LANES], cur[:, c0:c0 + LANES]], axis=0).astype(F32)
    lane = lax.broadcasted_iota(jnp.int32, kk.shape, 1)
    row = lax.broadcasted_iota(jnp.int32, kk.shape, 0)
    hd = ATT_HEAD_DIM
    own = jnp.where(((lane >= hd) if g % 2 else (lane < hd)) & (row != 0), kk, 0.0)
    other = pltpu.roll(own, hd, 1)
    lo, hi = (other, own) if g % 2 else (own, other)
    return lo.astype(BF16), hi.astype(BF16)


def _attention_consts(sink_ref, qaug_ref, kaug_ref, vones_ref):
    nk = 2 * ATT_BLOCK
    row_q = lax.broadcasted_iota(jnp.int32, qaug_ref.shape, 0)
    lane_q = lax.broadcasted_iota(jnp.int32, qaug_ref.shape, 1)
    pair_of_lane = (lane_q & (SINK_LANES - 1)) >> 1
    onehot = (lane_q < 2 * SINK_LANES) & (pair_of_lane == row_q // ATT_BLOCK)
    qaug_ref[...] = jnp.where(onehot, 1.0, 0.0).astype(BF16)
    row_k = lax.broadcasted_iota(jnp.int32, vones_ref.shape, 0)
    lane_k = lax.broadcasted_iota(jnp.int32, vones_ref.shape, 1)
    vones_ref[...] = jnp.where((row_k < nk) == (lane_k < ATT_HEAD_DIM), 1.0, 0.0).astype(BF16)
    for g in range(ATT_KV_HEADS):
        vals = jnp.zeros(vones_ref.shape, F32)
        for j in range(ATT_GROUP):
            sk = sink_ref[g * ATT_GROUP + j] * LOG2E
            vals = jnp.where((lane_k == j) & (row_k == (j % 2) * nk), sk, vals)
        hi = vals.astype(BF16).astype(F32)
        kaug_ref[g] = (hi + pltpu.roll(vals - hi, SINK_LANES, 1)).astype(BF16)


def _attention_kernel(last, sink_ref, q_ref, kp_ref, kc_ref, vp_ref, vc_ref, gb_ref, ret_ref, o_ref,
                      qaug_ref, kaug_ref, vones_ref, *p_refs):
    i = pl.program_id(1)
    bq = ATT_BLOCK
    nk = 2 * bq
    pairs = ATT_GROUP // 2

    @pl.when((pl.program_id(0) == 0) & (i == 0))
    def _init():
        _attention_consts(sink_ref, qaug_ref, kaug_ref, vones_ref)

    def scores(slot):
        kp, kc = kp_ref[...], kc_ref[...]
        qi = lax.broadcasted_iota(jnp.int32, (bq, nk), 0)
        kj = lax.broadcasted_iota(jnp.int32, (bq, nk), 1)
        dist = qi + bq - kj
        kmin = jnp.where(i > 0, 0, bq)
        valid = ((dist >= 0) & (dist < WINDOW) & (kj >= kmin)) | (kj == 0)
        bias = jnp.where(valid, 0.0, -jnp.inf)
        qaug = qaug_ref[...]
        for g in range(ATT_KV_HEADS):
            k_lo, k_hi = _pair_blocks(kp, kc, g)
            kfull = jnp.concatenate([jnp.concatenate([k_lo, k_hi], axis=0), kaug_ref[g]], axis=1)
            col0 = g * ATT_GROUP * ATT_HEAD_DIM
            qp = jnp.concatenate(
                [q_ref[:, col0 + p * LANES:col0 + (p + 1) * LANES] for p in range(pairs)], axis=0)
            qfull = jnp.concatenate([qp, qaug], axis=1)
            s = lax.dot_general(qfull, kfull, (((1,), (1,)), ((), ())),
                                preferred_element_type=F32)
            for p in range(pairs):
                for e in range(2):
                    sj = s[p * bq:(p + 1) * bq, e * nk:(e + 1) * nk] + bias
                    m = jnp.max(sj, axis=-1, keepdims=True)
                    p_refs[slot][g, p * bq:(p + 1) * bq, e * nk:(e + 1) * nk] = (
                        jnp.exp2(sj - m).astype(BF16))

    def values(slot):
        vp, vc = vp_ref[...], vc_ref[...]
        vones = vones_ref[...]
        for g in range(ATT_KV_HEADS):
            v_lo, v_hi = _pair_blocks(vp, vc, g)
            vfull = jnp.concatenate([jnp.concatenate([v_lo, v_hi], axis=0), vones], axis=1)
            o = jnp.dot(p_refs[slot][g], vfull, preferred_element_type=F32)
            col0 = g * ATT_GROUP * ATT_HEAD_DIM
            for p in range(pairs):
                num = o[p * bq:(p + 1) * bq, :LANES]
                den = o[p * bq:(p + 1) * bq, LANES:]
                c1 = col0 + p * LANES
                gate = gb_ref[:, c1:c1 + LANES].astype(F32)
                ret = ret_ref[:, c1:c1 + LANES].astype(F32)
                o_ref[:, c1:c1 + LANES] = (gate * num / den + ret).astype(o_ref.dtype)

    @pl.when(i == 0)
    def _first():
        scores(0)

    for par in (0, 1):
        @pl.when((i > 0) & (i < last) & (lax.rem(i, 2) == par))
        def _steady():
            values(1 - par)
            scores(par)

    @pl.when(i == last)
    def _last():
        values((last - 1) % 2)


def _attention(qa, akv, vg, ret, sinks, bsz, seq):
    t = bsz * seq
    bq = ATT_BLOCK
    nb = seq // bq
    rows = (ATT_GROUP // 2) * bq

    def spec(width, col, lag, back):
        def index(b, i):
            blk = jnp.clip(i - lag, 0, nb - 1)
            return (b * nb + jnp.maximum(blk - back, 0), col)
        return pl.BlockSpec((bq, width), index)

    return pl.pallas_call(
        functools.partial(_attention_kernel, nb),
        grid=(bsz, nb + 1),
        in_specs=[pl.BlockSpec(memory_space=pltpu.SMEM), spec(D_MODEL, 1, 0, 0),
                  spec(KV_W, 0, 0, 1), spec(KV_W, 0, 0, 0),
                  spec(KV_W, 1, 1, 1), spec(KV_W, 1, 1, 0), spec(D_MODEL, 3, 1, 0),
                  spec(D_MODEL, 0, 1, 0)],
        out_specs=spec(D_MODEL, 0, 1, 0),
        out_shape=jax.ShapeDtypeStruct((t, D_MODEL), BF16),
        scratch_shapes=[pltpu.VMEM((rows, LANES), BF16),
                        pltpu.VMEM((ATT_KV_HEADS, 2 * 2 * bq, LANES), BF16),
                        pltpu.VMEM((2 * 2 * bq, LANES), BF16),
                        pltpu.VMEM((ATT_KV_HEADS, rows, 2 * 2 * bq), BF16),
                        pltpu.VMEM((ATT_KV_HEADS, rows, 2 * 2 * bq), BF16)],
        compiler_params=_params(("arbitrary", "arbitrary"), "attention"),
        name="attention",
    )(sinks, qa, akv, akv, akv, akv, vg, ret)


def _out_proj_kernel(alpha, m_ref, x_ref, w_ref, g_ref, b_ref, h_ref, hb_ref):
    rows = h_ref.shape[0] // OUT_PROJ_ROW_CHUNKS
    for c in range(OUT_PROJ_ROW_CHUNKS):
        rs = slice(c * rows, (c + 1) * rows)
        y = jnp.dot(m_ref[rs, :], w_ref[...], preferred_element_type=F32)
        h = _layer_norm(alpha * x_ref[rs, :] + y, g_ref[...], b_ref[...])
        h_ref[rs, :] = h
        hb_ref[rs, :] = h.astype(BF16)


def _out_proj(merged, x2, wb, ln_g, ln_b, alpha):
    t, d = x2.shape
    tm = OUT_TM
    row = pl.BlockSpec((tm, d), lambda i: (i, 0))
    vec = pl.BlockSpec((1, d), lambda i: (0, 0))
    return pl.pallas_call(
        functools.partial(_out_proj_kernel, alpha),
        grid=(t // tm,),
        in_specs=[row, row, pl.BlockSpec((d, d), lambda i: (0, 0)), vec, vec],
        out_specs=[row, row],
        out_shape=[jax.ShapeDtypeStruct((t, d), F32), jax.ShapeDtypeStruct((t, d), BF16)],
        compiler_params=_params(("parallel",), "out_proj"),
        name="out_proj_ln",
    )(merged, x2, wb, ln_g.reshape(1, d), ln_b.reshape(1, d))


def _ffn_kernel(alpha, hb_ref, h_hbm, wg_ref, wu_ref, wd_ref, g_ref, b_ref, o_ref, hres_ref, sem):
    i = pl.program_id(0)
    f = pl.program_id(1)
    last = pl.num_programs(1) - 1
    tm = o_ref.shape[0]
    res_copy = pltpu.make_async_copy(h_hbm.at[pl.ds(i * tm, tm), :], hres_ref, sem)

    def tile():
        hb = hb_ref[...]
        gt = jnp.dot(hb, wg_ref[...], preferred_element_type=F32)
        ut = jnp.dot(hb, wu_ref[...], preferred_element_type=F32)
        a = (gt * _sigmoid(gt) * ut).astype(BF16)
        return jnp.dot(a, wd_ref[...], preferred_element_type=F32)

    @pl.when(f == 0)
    def _first():
        res_copy.start()
        o_ref[...] = tile()

    @pl.when((f > 0) & (f < last))
    def _middle():
        o_ref[...] += tile()

    @pl.when(f == last)
    def _last():
        res_copy.wait()
        z = alpha * hres_ref[...] + (o_ref[...] + tile())
        o_ref[...] = _layer_norm(z, g_ref[...], b_ref[...])


def _ffn(h1, h1b, wg, wu, wd, ln_g, ln_b, alpha):
    t, d = h1.shape
    n_f, _, tf = wg.shape
    tm = FFN_TM
    row = pl.BlockSpec((tm, d), lambda i, f: (i, 0))
    vec = pl.BlockSpec((1, d), lambda i, f: (0, 0))
    return pl.pallas_call(
        functools.partial(_ffn_kernel, alpha),
        grid=(t // tm, n_f),
        in_specs=[row,
                  pl.BlockSpec(memory_space=pl.ANY),
                  pl.BlockSpec((None, d, tf), lambda i, f: (f, 0, 0)),
                  pl.BlockSpec((None, d, tf), lambda i, f: (f, 0, 0)),
                  pl.BlockSpec((tf, d), lambda i, f: (f, 0)),
                  vec, vec],
        out_specs=row,
        out_shape=jax.ShapeDtypeStruct((t, d), F32),
        scratch_shapes=[pltpu.VMEM((tm, d), F32), pltpu.SemaphoreType.DMA(())],
        compiler_params=_params(("parallel", "arbitrary"), "ffn"),
        name="ffn_ln",
    )(h1b, h1, wg, wu, wd, ln_g.reshape(1, d), ln_b.reshape(1, d))


def kernel(x, positions, w_in, ret_gn_g, ret_gn_b, att_sinks, w_out, ln1_g, ln1_b,
           w_gate, w_up, w_down, ln2_g, ln2_b):
    bsz, seq, d = x.shape
    depth = w_in.shape[0]
    alpha = (2.0 * depth) ** 0.25
    h = x.reshape(bsz * seq, d)
    cos_r, sin_r, ca, s1, s2, hb = _prep(positions, h)
    for l in range(depth):
        if l > 0:
            hb = h.astype(BF16)
        w = w_in[l]
        steps = 4 * (bsz * seq // PROJ_TM)
        qa, wdb = _proj_multi(
            hb, w, [(OFF_RQ, OFF_RV - OFF_RQ, _ep_ret_rotary), (OFF_AQ, D_MODEL, _ep_att_q)], PROJ_TN,
            (cos_r, sin_r, ca, s1, s2), [(w_down[l], 0, steps, None)], "in_proj_qk")
        vg, wgb, wub = _proj_multi(
            hb, w, [(OFF_RV, D_MODEL, _ep_plain), (OFF_RG, D_MODEL, _ep_swish),
                    (OFF_GA, 2 * D_MODEL, _ep_sigmoid)], PROJ_TN,
            (), [(w_gate[l], 0, steps, FFN_TF), (w_up[l], steps, steps, FFN_TF)], "in_proj_vg")
        akv, wob = _proj_multi(hb, w, [(OFF_AK, 2 * KV_W, _ep_att_kv)], 2 * KV_W, (ca, s1, s2),
                               [(w_out[l], 0, steps // 4, None)], "in_proj_att_kv")
        ret = _retention(qa, vg, ret_gn_g[l], ret_gn_b[l], bsz, seq)
        merged = _attention(qa, akv, vg, ret, att_sinks[l], bsz, seq)
        h, hb = _out_proj(merged, h, wob, ln1_g[l], ln1_b[l], alpha)
        h = _ffn(h, hb, wgb, wub, wdb, ln2_g[l], ln2_b[l], alpha)
    return h.reshape(bsz, seq, d)
```

```python
import functools
import math

import jax
import jax.numpy as jnp
from jax import lax
from jax.experimental import pallas as pl
from jax.experimental.pallas import tpu as pltpu

F32 = jnp.float32
BF16 = jnp.bfloat16

D_MODEL = 2048
RET_HEADS = 4
RET_DV = D_MODEL // RET_HEADS
RET_DK = RET_DV // 2
RET_CHUNK = 256
RET_THETA = 10000.0
ATT_HEAD_DIM = 64
ATT_HEADS = D_MODEL // ATT_HEAD_DIM
ATT_KV_HEADS = ATT_HEADS // 8
ATT_GROUP = ATT_HEADS // ATT_KV_HEADS
WINDOW = 128
ATT_BLOCK = WINDOW
ROPE_THETA = 500000.0
ROPE_DIM = ATT_HEAD_DIM // 4
LN_EPS = 1e-5
GN_EPS = 1e-5

LANES = 128
LOG2E = math.log2(math.e)

OFF_RQ = 0
OFF_RK = OFF_RQ + RET_HEADS * RET_DK
OFF_RV = OFF_RK + RET_HEADS * RET_DK
OFF_RG = OFF_RV + RET_HEADS * RET_DV
OFF_AQ = OFF_RG + RET_HEADS * RET_DV
OFF_AK = OFF_AQ + ATT_HEADS * ATT_HEAD_DIM
OFF_AV = OFF_AK + ATT_KV_HEADS * ATT_HEAD_DIM
OFF_GA = OFF_AV + ATT_KV_HEADS * ATT_HEAD_DIM
OFF_GB = OFF_GA + D_MODEL
KV_W = ATT_KV_HEADS * ATT_HEAD_DIM

PREP_TM = 1024
PROJ_TM, PROJ_TN = 1024, 1024
OUT_TM = 512
FFN_TM, FFN_TF = 1024, 512
RET_STEP_CHUNKS = 2
OUT_PROJ_ROW_CHUNKS = 2
VMEM_MIB = {"prep": 60, "in_proj": 56, "retention": 32, "attention": 32, "out_proj": 48,
            "ffn": 60}


def _params(sem, call):
    return pltpu.CompilerParams(dimension_semantics=sem,
                                vmem_limit_bytes=VMEM_MIB[call] * 1024 * 1024)


def _sigmoid(v):
    return 0.5 * jnp.tanh(0.5 * v) + 0.5


def _layer_norm(z, g, b):
    mu = jnp.mean(z, axis=-1, keepdims=True)
    d = z - mu
    var = jnp.mean(d * d, axis=-1, keepdims=True)
    return d * lax.rsqrt(var + LN_EPS) * g + b


def _prep_kernel(pos_ref, fr_ref, fa_ref, x_ref, w_ref,
                 cr_ref, sr_ref, ca_ref, s1_ref, s2_ref, xb_ref, akv_ref, wb_ref):
    @pl.when(pl.program_id(0) == 0)
    def _cast_weights():
        wb_ref[...] = w_ref[...].astype(BF16)

    xb = x_ref[...].astype(BF16)
    xb_ref[...] = xb
    pos_t = pos_ref[...].astype(F32).T
    pos = jnp.concatenate([jnp.broadcast_to(pos_t[:, c:c + 1], (LANES, LANES))
                           for c in range(pos_t.shape[1])], axis=0)
    ang_r = pos * fr_ref[...]
    cr_ref[...] = jnp.cos(ang_r)
    sr_ref[...] = jnp.sin(ang_r)
    ang_a = pos * fa_ref[...]
    sa = jnp.sin(ang_a)
    d = lax.broadcasted_iota(jnp.int32, ang_a.shape, 1) & (ATT_HEAD_DIM - 1)
    half = ROPE_DIM // 2
    ca_ref[...] = jnp.cos(ang_a)
    s1_ref[...] = jnp.where(d < half, -sa, 0.0)
    s2_ref[...] = jnp.where((d >= half) & (d < ROPE_DIM), sa, 0.0)
    acc = jnp.dot(xb, wb_ref[...], preferred_element_type=F32)
    akv_ref[...] = _ep_att_kv(acc, 0, ca_ref, s1_ref, s2_ref).astype(BF16)


def _prep(positions, x2, w):
    t, dm = x2.shape
    tm = PREP_TM
    kvw = OFF_GA - OFF_AK
    pos_l = positions.reshape(t // LANES, LANES)
    inv_r = 1.0 / (RET_THETA ** jnp.linspace(0.0, 1.0, RET_DK // 2, dtype=F32))
    inv_a = 1.0 / (ROPE_THETA ** (jnp.arange(0, ROPE_DIM, 2, dtype=F32) / ROPE_DIM))
    d = jnp.arange(LANES) % ATT_HEAD_DIM
    fa = jnp.where(d < ROPE_DIM, jnp.tile(inv_a, LANES // inv_a.size), 0.0).astype(F32)
    row = pl.BlockSpec((tm, LANES), lambda i: (i, 0))
    xrow = pl.BlockSpec((tm, dm), lambda i: (i, 0))
    vec = pl.BlockSpec((1, LANES), lambda i: (0, 0))
    tab = jax.ShapeDtypeStruct((t, LANES), F32)
    return pl.pallas_call(
        _prep_kernel,
        grid=(t // tm,),
        in_specs=[pl.BlockSpec((tm // LANES, LANES), lambda i: (i, 0)), vec, vec, xrow,
                  pl.BlockSpec((pl.Element(dm), pl.Element(kvw)), lambda i: (0, OFF_AK))],
        out_specs=[row] * 5 + [xrow, pl.BlockSpec((tm, kvw), lambda i: (i, 0))],
        out_shape=[tab] * 5 + [jax.ShapeDtypeStruct((t, dm), BF16),
                               jax.ShapeDtypeStruct((t, kvw), BF16)],
        scratch_shapes=[pltpu.VMEM((dm, kvw), BF16)],
        compiler_params=_params(("arbitrary",), "prep"),
        name="prep_tables_xcast_kv",
    )(pos_l, inv_r.reshape(1, LANES), fa.reshape(1, LANES), x2, w)


def _ep_plain(acc, j, *tabs):
    return acc


def _ep_swish(acc, j, *tabs):
    return acc * _sigmoid(acc)


def _ep_sigmoid(acc, j, *tabs):
    return _sigmoid(acc)


def _ep_ret_rotary(acc, j, cos_ref, sin_ref, *att_tabs):
    scale = jnp.where(j == 0, RET_DK ** -0.5, 1.0)
    cos = cos_ref[...] * scale
    sin = sin_ref[...] * scale
    half = RET_DK // 2
    outs = []
    for c0 in range(0, acc.shape[1], RET_DK):
        a1 = acc[:, c0:c0 + half]
        a2 = acc[:, c0 + half:c0 + RET_DK]
        outs += [a1 * cos - a2 * sin, a2 * cos + a1 * sin]
    return jnp.concatenate(outs, axis=-1)


def _rope_lanes(a, ca, s1, s2):
    outs = []
    for c0 in range(0, a.shape[1], LANES):
        ac = a[:, c0:c0 + LANES]
        up = pltpu.roll(ac, LANES - ROPE_DIM // 2, 1)
        dn = pltpu.roll(ac, ROPE_DIM // 2, 1)
        outs.append(ac * ca + up * s1 + dn * s2)
    return jnp.concatenate(outs, axis=-1)


def _ep_att_q(acc, j, cos_ref, sin_ref, ca_ref, s1_ref, s2_ref):
    c = (ATT_HEAD_DIM ** -0.5) * LOG2E
    return _rope_lanes(acc, ca_ref[...] * c, s1_ref[...] * c, s2_ref[...] * c)


def _ep_att_kv(acc, j, ca_ref, s1_ref, s2_ref):
    k = _rope_lanes(acc[:, :KV_W], ca_ref[...], s1_ref[...], s2_ref[...])
    return jnp.concatenate([k, acc[:, KV_W:]], axis=-1)


def _proj_kernel(segments, n_tab, n_ride, x_ref, w_ref, *rest):
    tab_refs = rest[:n_tab]
    ride_refs = rest[n_tab:n_tab + n_ride]
    o_ref = rest[n_tab + n_ride]
    ride_out_refs = rest[n_tab + n_ride + 1:n_tab + 2 * n_ride + 1]
    wb_ref = rest[-1]
    j = pl.program_id(0)

    @pl.when(pl.program_id(1) == 0)
    def _cast_weights():
        wb_ref[...] = w_ref[...].astype(BF16)

    for j0, j1, epilogue, rides in segments:
        @pl.when((j >= j0) & (j < j1))
        def _segment():
            for r in rides:
                ride_out_refs[r][...] = ride_refs[r][...].astype(BF16)
            acc = jnp.dot(x_ref[...], wb_ref[...], preferred_element_type=F32)
            o_ref[...] = epilogue(acc, j - j0, *tab_refs).astype(o_ref.dtype)


def _proj_multi(xb, w, segs, tn, tables=(), rides=(), name="in_proj"):
    t, k = xb.shape
    tm = PROJ_TM
    nt = t // tm
    j0s, bounds = [], 0
    for col0, width, _ in segs:
        assert col0 % LANES == 0 and width % tn == 0
        j0s.append(bounds)
        bounds += width // tn
    n_tiles = bounds

    def w_col(j):
        col = segs[0][0] + j * tn
        for (col0, _, _), j0 in zip(segs[1:], j0s[1:]):
            col = jnp.where(j >= j0, col0 + (j - j0) * tn, col)
        return pl.multiple_of(col, LANES)

    tab = pl.BlockSpec((tm, LANES), lambda j, i: (i, 0))
    in_specs = [pl.BlockSpec((tm, k), lambda j, i: (i, 0)),
                pl.BlockSpec((pl.Element(k), pl.Element(tn)), lambda j, i: (0, w_col(j)))]
    in_specs += [tab] * len(tables)
    out_specs = [pl.BlockSpec((tm, tn), lambda j, i: (i, j))]
    out_shape = [jax.ShapeDtypeStruct((t, n_tiles * tn), BF16)]
    ride_steps = []
    for arr, first, n_slabs in rides:
        slab = arr.shape[0] // n_slabs
        assert slab * n_slabs == arr.shape[0] and slab % 16 == 0 and first + n_slabs <= n_tiles * nt
        spec = pl.BlockSpec(
            (slab, arr.shape[1]),
            lambda j, i, first=first, n=n_slabs: (jnp.clip(j * nt + i - first, 0, n - 1), 0))
        in_specs.append(spec)
        out_specs.append(spec)
        out_shape.append(jax.ShapeDtypeStruct(arr.shape, BF16))
        ride_steps.append((first, first + n_slabs))
    segments = []
    for (_, width, epilogue), j0 in zip(segs, j0s):
        j1 = j0 + width // tn
        active = tuple(r for r, (s0, s1) in enumerate(ride_steps) if s0 < j1 * nt and s1 > j0 * nt)
        segments.append((j0, j1, epilogue, active))
    outs = pl.pallas_call(
        functools.partial(_proj_kernel, tuple(segments), len(tables), len(rides)),
        grid=(n_tiles, nt),
        in_specs=in_specs,
        out_specs=out_specs,
        out_shape=out_shape,
        scratch_shapes=[pltpu.VMEM((k, tn), BF16)],
        compiler_params=_params(("arbitrary", "arbitrary"), "in_proj"),
        name=name,
    )(xb, w, *tables, *[arr for arr, _, _ in rides])
    return outs[0] if not rides else outs


def _retention_kernel(q_ref, k_ref, v_ref, gs_ref, ga_ref, gng_ref, gnb_ref,
                      o_ref, state_ref, dmask_ref, qdec_ref, kdec_ref, cdec_ref):
    c = RET_CHUNK

    @pl.when(pl.program_id(1) == 0)
    def _init():
        state_ref[...] = jnp.zeros_like(state_ref)
        ii = lax.broadcasted_iota(jnp.int32, (c, c), 0)
        jj = lax.broadcasted_iota(jnp.int32, (c, c), 1)
        diff = (ii - jj).astype(F32)
        idx = lax.broadcasted_iota(jnp.int32, (c, RET_DK), 0).astype(F32)
        for h in range(RET_HEADS):
            def log_gamma(shape):
                return jnp.log(1.0 - jnp.exp2(-5.0 - jnp.full(shape, float(h), F32)))

            dmask_ref[h] = jnp.where(
                diff >= 0.0, jnp.exp(log_gamma((c, c)) * jnp.maximum(diff, 0.0)), 0.0)
            lgl = log_gamma((c, RET_DK))
            qdec_ref[h] = jnp.exp(lgl * (idx + 1.0))
            kdec_ref[h] = jnp.exp(lgl * (c - 1.0 - idx))
            cdec_ref[h] = jnp.exp(log_gamma((8, LANES)) * float(c))

    for r0, h in ((r0, h) for r0 in range(0, q_ref.shape[0], c) for h in range(RET_HEADS)):
        rs = slice(r0, r0 + c)
        ksl = slice(h * RET_DK, (h + 1) * RET_DK)
        vsl = slice(h * RET_DV, (h + 1) * RET_DV)
        qb = q_ref[rs, ksl]
        kb = k_ref[rs, ksl]
        v = v_ref[rs, vsl]
        qd = (qb.astype(F32) * qdec_ref[h]).astype(BF16)
        kd = (kb.astype(F32) * kdec_ref[h]).astype(BF16)

        s = lax.dot_general(qb, kb, (((1,), (1,)), ((), ())), preferred_element_type=F32)
        s = s * dmask_ref[h]
        inner = jnp.dot(s.astype(BF16), v, preferred_element_type=F32)
        state = state_ref[h]
        cross = jnp.dot(qd, state.astype(BF16), preferred_element_type=F32)
        kv = lax.dot_general(kd, v, (((0,), (0,)), ((), ())), preferred_element_type=F32)
        state_ref[h] = state * cdec_ref[h, 0:1, 0:1] + kv

        y = inner + cross
        mu = jnp.mean(y, axis=-1, keepdims=True)
        d = y - mu
        var = jnp.mean(d * d, axis=-1, keepdims=True)
        yn = d * lax.rsqrt(var + GN_EPS) * gng_ref[:, vsl] + gnb_ref[:, vsl]
        gate = ga_ref[rs, vsl].astype(F32) * gs_ref[rs, vsl].astype(F32)
        o_ref[rs, vsl] = (gate * yn).astype(o_ref.dtype)


def _retention(qa, vg, gn_g, gn_b, bsz, seq):
    t = bsz * seq
    c = RET_CHUNK
    rows_per_step = RET_STEP_CHUNKS * c
    nc = seq // rows_per_step
    nh = RET_HEADS

    def rows(width, col):
        return pl.BlockSpec((rows_per_step, width), lambda b, n: (b * nc + n, col))

    gn = pl.BlockSpec((1, D_MODEL), lambda b, n: (0, 0))
    return pl.pallas_call(
        _retention_kernel,
        grid=(bsz, nc),
        in_specs=[rows(nh * RET_DK, 0), rows(nh * RET_DK, 1), rows(D_MODEL, 0), rows(D_MODEL, 1),
                  rows(D_MODEL, 2), gn, gn],
        out_specs=rows(D_MODEL, 0),
        out_shape=jax.ShapeDtypeStruct((t, D_MODEL), BF16),
        scratch_shapes=[pltpu.VMEM((nh, RET_DK, RET_DV), F32), pltpu.VMEM((nh, c, c), F32),
                        pltpu.VMEM((nh, c, RET_DK), F32), pltpu.VMEM((nh, c, RET_DK), F32),
                        pltpu.VMEM((nh, 8, LANES), F32)],
        compiler_params=_params(("parallel", "arbitrary"), "retention"),
        name="retention",
    )(qa, qa, vg, vg, vg, gn_g.reshape(1, -1), gn_b.reshape(1, -1))


SINK_LANES = 8


def _pair_blocks(prev, cur, g):
    c0 = (g // 2) * LANES
    kk = jnp.concatenate([prev[:, c0:c0 + LANES], cur[:, c0:c0 + LANES]], axis=0).astype(F32)
    lane = lax.broadcasted_iota(jnp.int32, kk.shape, 1)
    row = lax.broadcasted_iota(jnp.int32, kk.shape, 0)
    hd = ATT_HEAD_DIM
    own = jnp.where(((lane >= hd) if g % 2 else (lane < hd)) & (row != 0), kk, 0.0)
    other = pltpu.roll(own, hd, 1)
    lo, hi = (other, own) if g % 2 else (own, other)
    return lo.astype(BF16), hi.astype(BF16)


def _attention_consts(sink_ref, qaug_ref, kaug_ref, vones_ref):
    nk = 2 * ATT_BLOCK
    row_q = lax.broadcasted_iota(jnp.int32, qaug_ref.shape, 0)
    lane_q = lax.broadcasted_iota(jnp.int32, qaug_ref.shape, 1)
    pair_of_lane = (lane_q & (SINK_LANES - 1)) >> 1
    onehot = (lane_q < 2 * SINK_LANES) & (pair_of_lane == row_q // ATT_BLOCK)
    qaug_ref[...] = jnp.where(onehot, 1.0, 0.0).astype(BF16)
    row_k = lax.broadcasted_iota(jnp.int32, vones_ref.shape, 0)
    lane_k = lax.broadcasted_iota(jnp.int32, vones_ref.shape, 1)
    vones_ref[...] = jnp.where((row_k < nk) == (lane_k < ATT_HEAD_DIM), 1.0, 0.0).astype(BF16)
    for g in range(ATT_KV_HEADS):
        vals = jnp.zeros(vones_ref.shape, F32)
        for j in range(ATT_GROUP):
            sk = sink_ref[g * ATT_GROUP + j] * LOG2E
            vals = jnp.where((lane_k == j) & (row_k == (j % 2) * nk), sk, vals)
        hi = vals.astype(BF16).astype(F32)
        kaug_ref[g] = (hi + pltpu.roll(vals - hi, SINK_LANES, 1)).astype(BF16)


def _attention_kernel(last, sink_ref, q_ref, kp_ref, kc_ref, vp_ref, vc_ref, gb_ref, ret_ref, o_ref,
                      qaug_ref, kaug_ref, vones_ref, *p_refs):
    i = pl.program_id(1)
    bq = ATT_BLOCK
    nk = 2 * bq
    pairs = ATT_GROUP // 2

    @pl.when((pl.program_id(0) == 0) & (i == 0))
    def _init():
        _attention_consts(sink_ref, qaug_ref, kaug_ref, vones_ref)

    def scores(slot):
        kp, kc = kp_ref[...], kc_ref[...]
        qi = lax.broadcasted_iota(jnp.int32, (bq, nk), 0)
        kj = lax.broadcasted_iota(jnp.int32, (bq, nk), 1)
        dist = qi + bq - kj
        kmin = jnp.where(i > 0, 0, bq)
        valid = ((dist >= 0) & (dist < WINDOW) & (kj >= kmin)) | (kj == 0)
        bias = jnp.where(valid, 0.0, -jnp.inf)
        qaug = qaug_ref[...]
        for g in range(ATT_KV_HEADS):
            k_lo, k_hi = _pair_blocks(kp, kc, g)
            kfull = jnp.concatenate([jnp.concatenate([k_lo, k_hi], axis=0), kaug_ref[g]], axis=1)
            col0 = g * ATT_GROUP * ATT_HEAD_DIM
            qp = jnp.concatenate(
                [q_ref[:, col0 + p * LANES:col0 + (p + 1) * LANES] for p in range(pairs)], axis=0)
            qfull = jnp.concatenate([qp, qaug], axis=1)
            s = lax.dot_general(qfull, kfull, (((1,), (1,)), ((), ())),
                                preferred_element_type=F32)
            for p in range(pairs):
                for e in range(2):
                    sj = s[p * bq:(p + 1) * bq, e * nk:(e + 1) * nk] + bias
                    m = jnp.max(sj, axis=-1, keepdims=True)
                    p_refs[slot][g, p * bq:(p + 1) * bq, e * nk:(e + 1) * nk] = (
                        jnp.exp2(sj - m).astype(BF16))

    def values(slot):
        vp, vc = vp_ref[...], vc_ref[...]
        vones = vones_ref[...]
        for g in range(ATT_KV_HEADS):
            v_lo, v_hi = _pair_blocks(vp, vc, g)
            vfull = jnp.concatenate([jnp.concatenate([v_lo, v_hi], axis=0), vones], axis=1)
            o = jnp.dot(p_refs[slot][g], vfull, preferred_element_type=F32)
            col0 = g * ATT_GROUP * ATT_HEAD_DIM
            for p in range(pairs):
                num = o[p * bq:(p + 1) * bq, :LANES]
                den = o[p * bq:(p + 1) * bq, LANES:]
                c1 = col0 + p * LANES
                gate = gb_ref[:, c1:c1 + LANES].astype(F32)
                ret = ret_ref[:, c1:c1 + LANES].astype(F32)
                o_ref[:, c1:c1 + LANES] = (gate * num / den + ret).astype(o_ref.dtype)

    @pl.when(i == 0)
    def _first():
        scores(0)

    for par in (0, 1):
        @pl.when((i > 0) & (i < last) & (lax.rem(i, 2) == par))
        def _steady():
            values(1 - par)
            scores(par)

    @pl.when(i == last)
    def _last():
        values((last - 1) % 2)


def _attention(qa, akv, vg, ret, sinks, bsz, seq):
    t = bsz * seq
    bq = ATT_BLOCK
    nb = seq // bq
    rows = (ATT_GROUP // 2) * bq

    def spec(width, col, lag, back):
        def index(b, i):
            blk = jnp.clip(i - lag, 0, nb - 1)
            return (b * nb + jnp.maximum(blk - back, 0), col)
        return pl.BlockSpec((bq, width), index)

    return pl.pallas_call(
        functools.partial(_attention_kernel, nb),
        grid=(bsz, nb + 1),
        in_specs=[pl.BlockSpec(memory_space=pltpu.SMEM), spec(D_MODEL, 1, 0, 0),
                  spec(KV_W, 0, 0, 1), spec(KV_W, 0, 0, 0),
                  spec(KV_W, 1, 1, 1), spec(KV_W, 1, 1, 0), spec(D_MODEL, 3, 1, 0),
                  spec(D_MODEL, 0, 1, 0)],
        out_specs=spec(D_MODEL, 0, 1, 0),
        out_shape=jax.ShapeDtypeStruct((t, D_MODEL), BF16),
        scratch_shapes=[pltpu.VMEM((rows, LANES), BF16),
                        pltpu.VMEM((ATT_KV_HEADS, 2 * 2 * bq, LANES), BF16),
                        pltpu.VMEM((2 * 2 * bq, LANES), BF16),
                        pltpu.VMEM((ATT_KV_HEADS, rows, 2 * 2 * bq), BF16),
                        pltpu.VMEM((ATT_KV_HEADS, rows, 2 * 2 * bq), BF16)],
        compiler_params=_params(("arbitrary", "arbitrary"), "attention"),
        name="attention",
    )(sinks, qa, akv, akv, akv, akv, vg, ret)


def _out_proj_kernel(alpha, m_ref, x_ref, w_ref, g_ref, b_ref, h_ref, hb_ref):
    rows = h_ref.shape[0] // OUT_PROJ_ROW_CHUNKS
    for c in range(OUT_PROJ_ROW_CHUNKS):
        rs = slice(c * rows, (c + 1) * rows)
        y = jnp.dot(m_ref[rs, :], w_ref[...], preferred_element_type=F32)
        h = _layer_norm(alpha * x_ref[rs, :] + y, g_ref[...], b_ref[...])
        h_ref[rs, :] = h
        hb_ref[rs, :] = h.astype(BF16)


def _out_proj(merged, x2, wb, ln_g, ln_b, alpha):
    t, d = x2.shape
    tm = OUT_TM
    row = pl.BlockSpec((tm, d), lambda i: (i, 0))
    vec = pl.BlockSpec((1, d), lambda i: (0, 0))
    return pl.pallas_call(
        functools.partial(_out_proj_kernel, alpha),
        grid=(t // tm,),
        in_specs=[row, row, pl.BlockSpec((d, d), lambda i: (0, 0)), vec, vec],
        out_specs=[row, row],
        out_shape=[jax.ShapeDtypeStruct((t, d), F32), jax.ShapeDtypeStruct((t, d), BF16)],
        compiler_params=_params(("parallel",), "out_proj"),
        name="out_proj_ln",
    )(merged, x2, wb, ln_g.reshape(1, d), ln_b.reshape(1, d))


def _ffn_kernel(alpha, hb_ref, h_hbm, wg_ref, wu_ref, wd_ref, g_ref, b_ref, o_ref, hres_ref, sem):
    i = pl.program_id(0)
    f = pl.program_id(1)
    last = pl.num_programs(1) - 1
    tm = o_ref.shape[0]
    res_copy = pltpu.make_async_copy(h_hbm.at[pl.ds(i * tm, tm), :], hres_ref, sem)

    def tile():
        hb = hb_ref[...]
        gt = jnp.dot(hb, wg_ref[...], preferred_element_type=F32)
        ut = jnp.dot(hb, wu_ref[...], preferred_element_type=F32)
        a = (gt * _sigmoid(gt) * ut).astype(BF16)
        return jnp.dot(a, wd_ref[...], preferred_element_type=F32)

    @pl.when(f == 0)
    def _first():
        res_copy.start()
        o_ref[...] = tile()

    @pl.when((f > 0) & (f < last))
    def _middle():
        o_ref[...] += tile()

    @pl.when(f == last)
    def _last():
        res_copy.wait()
        z = alpha * hres_ref[...] + (o_ref[...] + tile())
        o_ref[...] = _layer_norm(z, g_ref[...], b_ref[...])


def _ffn(h1, h1b, wg, wu, wd, ln_g, ln_b, alpha):
    t, d = h1.shape
    dff = wg.shape[1]
    tm, tf = FFN_TM, FFN_TF
    row = pl.BlockSpec((tm, d), lambda i, f: (i, 0))
    vec = pl.BlockSpec((1, d), lambda i, f: (0, 0))
    return pl.pallas_call(
        functools.partial(_ffn_kernel, alpha),
        grid=(t // tm, dff // tf),
        in_specs=[row,
                  pl.BlockSpec(memory_space=pl.ANY),
                  pl.BlockSpec((d, tf), lambda i, f: (0, f)),
                  pl.BlockSpec((d, tf), lambda i, f: (0, f)),
                  pl.BlockSpec((tf, d), lambda i, f: (f, 0)),
                  vec, vec],
        out_specs=row,
        out_shape=jax.ShapeDtypeStruct((t, d), F32),
        scratch_shapes=[pltpu.VMEM((tm, d), F32), pltpu.SemaphoreType.DMA(())],
        compiler_params=_params(("parallel", "arbitrary"), "ffn"),
        name="ffn_ln",
    )(h1b, h1, wg, wu, wd, ln_g.reshape(1, d), ln_b.reshape(1, d))


def kernel(x, positions, w_in, ret_gn_g, ret_gn_b, att_sinks, w_out, ln1_g, ln1_b,
           w_gate, w_up, w_down, ln2_g, ln2_b):
    bsz, seq, d = x.shape
    depth = w_in.shape[0]
    alpha = (2.0 * depth) ** 0.25
    h = x.reshape(bsz * seq, d)
    for l in range(depth):
        w = w_in[l]
        cos_r, sin_r, ca, s1, s2, hb, akv = _prep(positions, h, w)
        steps = 4 * (bsz * seq // PROJ_TM)
        qa, wdb, wob = _proj_multi(
            hb, w, [(OFF_RQ, OFF_RV - OFF_RQ, _ep_ret_rotary), (OFF_AQ, D_MODEL, _ep_att_q)], PROJ_TN,
            (cos_r, sin_r, ca, s1, s2), [(w_down[l], 0, steps), (w_out[l], 0, steps)], "in_proj_qk")
        vg, wgb, wub = _proj_multi(
            hb, w, [(OFF_RV, D_MODEL, _ep_plain), (OFF_RG, D_MODEL, _ep_swish),
                    (OFF_GA, 2 * D_MODEL, _ep_sigmoid)], PROJ_TN,
            (), [(w_gate[l], 0, steps), (w_up[l], steps, steps)], "in_proj_vg")
        ret = _retention(qa, vg, ret_gn_g[l], ret_gn_b[l], bsz, seq)
        merged = _attention(qa, akv, vg, ret, att_sinks[l], bsz, seq)
        h, hb = _out_proj(merged, h, wob, ln1_g[l], ln1_b[l], alpha)
        h = _ffn(h, hb, wgb, wub, wdb, ln2_g[l], ln2_b[l], alpha)
    return h.reshape(bsz, seq, d)
```

```python
import functools
import math

import jax
import jax.numpy as jnp
from jax import lax
from jax.experimental import pallas as pl
from jax.experimental.pallas import tpu as pltpu

F32 = jnp.float32
BF16 = jnp.bfloat16

D_MODEL = 2048
RET_HEADS = 4
RET_DV = D_MODEL // RET_HEADS
RET_DK = RET_DV // 2
RET_CHUNK = 256
RET_THETA = 10000.0
ATT_HEAD_DIM = 64
ATT_HEADS = D_MODEL // ATT_HEAD_DIM
ATT_KV_HEADS = ATT_HEADS // 8
ATT_GROUP = ATT_HEADS // ATT_KV_HEADS
WINDOW = 128
ATT_BLOCK = WINDOW
ROPE_THETA = 500000.0
ROPE_DIM = ATT_HEAD_DIM // 4
LN_EPS = 1e-5
GN_EPS = 1e-5

LANES = 128
LOG2E = math.log2(math.e)

OFF_RQ = 0
OFF_RK = OFF_RQ + RET_HEADS * RET_DK
OFF_RV = OFF_RK + RET_HEADS * RET_DK
OFF_RG = OFF_RV + RET_HEADS * RET_DV
OFF_AQ = OFF_RG + RET_HEADS * RET_DV
OFF_AK = OFF_AQ + ATT_HEADS * ATT_HEAD_DIM
OFF_AV = OFF_AK + ATT_KV_HEADS * ATT_HEAD_DIM
OFF_GA = OFF_AV + ATT_KV_HEADS * ATT_HEAD_DIM
OFF_GB = OFF_GA + D_MODEL
KV_W = ATT_KV_HEADS * ATT_HEAD_DIM

PREP_TM = 1024
PROJ_TM, PROJ_TN = 1024, 1024
OUT_TM = 512
FFN_TM, FFN_TF = 1024, 512
RET_STEP_CHUNKS = 2
OUT_PROJ_ROW_CHUNKS = 2
PROJ_ROW_CHUNKS = 4
VMEM_MIB = {"prep": 60, "in_proj": 56, "retention": 32, "attention": 32, "out_proj": 48,
            "ffn": 60}


def _params(sem, call):
    return pltpu.CompilerParams(dimension_semantics=sem,
                                vmem_limit_bytes=VMEM_MIB[call] * 1024 * 1024)


def _sigmoid(v):
    return 0.5 * jnp.tanh(0.5 * v) + 0.5


def _layer_norm(z, g, b):
    mu = jnp.mean(z, axis=-1, keepdims=True)
    d = z - mu
    var = jnp.mean(d * d, axis=-1, keepdims=True)
    return d * lax.rsqrt(var + LN_EPS) * g + b


ROPE_PACK = ATT_HEAD_DIM // ROPE_DIM
PREP_ROWS = 64


def _prep_kernel(pos_ref, fr_ref, fa_ref, x_ref, w_ref,
                 cr_ref, sr_ref, ca_ref, s1_ref, s2_ref, xb_ref, akv_ref, pos_scr, wb_ref):
    @pl.when(pl.program_id(0) == 0)
    def _cast_weights():
        wb_ref[...] = w_ref[...].astype(BF16)

    tm = pos_scr.shape[0]
    pos_t = pos_ref[...].astype(F32).T
    cols = [jnp.broadcast_to(pos_t[:, c:c + 1], (LANES, LANES)) for c in range(tm // LANES)]
    pos_scr[...] = jnp.concatenate(cols, axis=0)
    fr = fr_ref[...]

    def ret_rows(k, carry):
        rs = pl.ds(pl.multiple_of(k * PREP_ROWS, PREP_ROWS), PREP_ROWS)
        xb_ref[rs, :] = x_ref[rs, :].astype(BF16)
        ang_r = pos_scr[rs, :] * fr
        cr_ref[rs, :] = jnp.cos(ang_r)
        sr_ref[rs, :] = jnp.sin(ang_r)
        return carry

    lax.fori_loop(0, tm // PREP_ROWS, ret_rows, 0)
    rg = tm // ROPE_PACK
    per = rg // LANES
    lane = lax.broadcasted_iota(jnp.int32, (rg, LANES), 1)
    group = (lane & (ATT_HEAD_DIM - 1)) // ROPE_DIM
    pos_p = jnp.concatenate(cols[:per], axis=0)
    for g in range(1, ROPE_PACK):
        pos_p = jnp.where(group == g, jnp.concatenate(cols[g * per:(g + 1) * per], axis=0), pos_p)
    ang_p = pos_p * fa_ref[...]
    cp = jnp.cos(ang_p)
    sp = jnp.sin(ang_p)
    dp = lane & (ROPE_DIM - 1)
    half = ROPE_DIM // 2
    s1p = jnp.where(dp < half, -sp, 0.0)
    s2p = jnp.where(dp >= half, sp, 0.0)
    rotated = group == 0
    for g in range(ROPE_PACK):
        def spread(p, fill):
            a = pltpu.roll(p, LANES - ROPE_DIM * g, 1) if g else p
            return jnp.where(rotated, a, fill)

        rs = slice(g * rg, (g + 1) * rg)
        ca_ref[rs, :] = spread(cp, 1.0)
        s1_ref[rs, :] = spread(s1p, 0.0)
        s2_ref[rs, :] = spread(s2p, 0.0)
    acc = jnp.dot(xb_ref[...], wb_ref[...], preferred_element_type=F32)
    akv_ref[...] = _ep_att_kv(acc, 0, ca_ref, s1_ref, s2_ref).astype(BF16)


def _prep(positions, x2, w):
    t, dm = x2.shape
    tm = PREP_TM
    kvw = OFF_GA - OFF_AK
    pos_l = positions.reshape(t // LANES, LANES)
    inv_r = 1.0 / (RET_THETA ** jnp.linspace(0.0, 1.0, RET_DK // 2, dtype=F32))
    inv_a = 1.0 / (ROPE_THETA ** (jnp.arange(0, ROPE_DIM, 2, dtype=F32) / ROPE_DIM))
    fa = jnp.tile(inv_a, LANES // inv_a.size)
    row = pl.BlockSpec((tm, LANES), lambda i: (i, 0))
    xrow = pl.BlockSpec((tm, dm), lambda i: (i, 0))
    vec = pl.BlockSpec((1, LANES), lambda i: (0, 0))
    tab = jax.ShapeDtypeStruct((t, LANES), F32)
    return pl.pallas_call(
        _prep_kernel,
        grid=(t // tm,),
        in_specs=[pl.BlockSpec((tm // LANES, LANES), lambda i: (i, 0)), vec, vec, xrow,
                  pl.BlockSpec((pl.Element(dm), pl.Element(kvw)), lambda i: (0, OFF_AK))],
        out_specs=[row] * 5 + [xrow, pl.BlockSpec((tm, kvw), lambda i: (i, 0))],
        out_shape=[tab] * 5 + [jax.ShapeDtypeStruct((t, dm), BF16),
                               jax.ShapeDtypeStruct((t, kvw), BF16)],
        scratch_shapes=[pltpu.VMEM((tm, LANES), F32), pltpu.VMEM((dm, kvw), BF16)],
        compiler_params=_params(("arbitrary",), "prep"),
        name="prep_tables_xcast_kv",
    )(pos_l, inv_r.reshape(1, LANES), fa.reshape(1, LANES), x2, w)


def _ep_plain(acc, j, *tabs):
    return acc


def _ep_swish(acc, j, *tabs):
    return acc * _sigmoid(acc)


def _ep_sigmoid(acc, j, *tabs):
    return _sigmoid(acc)


def _ep_ret_rotary(acc, j, cos_ref, sin_ref, *att_tabs):
    scale = jnp.where(j == 0, RET_DK ** -0.5, 1.0)
    cos = cos_ref[...] * scale
    sin = sin_ref[...] * scale
    half = RET_DK // 2
    outs = []
    for c0 in range(0, acc.shape[1], RET_DK):
        a1 = acc[:, c0:c0 + half]
        a2 = acc[:, c0 + half:c0 + RET_DK]
        outs += [a1 * cos - a2 * sin, a2 * cos + a1 * sin]
    return jnp.concatenate(outs, axis=-1)


def _rope_lanes(a, ca, s1, s2):
    outs = []
    for c0 in range(0, a.shape[1], LANES):
        ac = a[:, c0:c0 + LANES]
        up = pltpu.roll(ac, LANES - ROPE_DIM // 2, 1)
        dn = pltpu.roll(ac, ROPE_DIM // 2, 1)
        outs.append(ac * ca + up * s1 + dn * s2)
    return jnp.concatenate(outs, axis=-1)


def _ep_att_q(acc, j, cos_ref, sin_ref, ca_ref, s1_ref, s2_ref):
    c = (ATT_HEAD_DIM ** -0.5) * LOG2E
    return _rope_lanes(acc, ca_ref[...] * c, s1_ref[...] * c, s2_ref[...] * c)


def _ep_att_kv(acc, j, ca_ref, s1_ref, s2_ref):
    k = _rope_lanes(acc[:, :KV_W], ca_ref[...], s1_ref[...], s2_ref[...])
    return jnp.concatenate([k, acc[:, KV_W:]], axis=-1)


def _proj_kernel(segments, n_tab, n_ride, x_ref, w_ref, *rest):
    tab_refs = rest[:n_tab]
    ride_refs = rest[n_tab:n_tab + n_ride]
    o_ref = rest[n_tab + n_ride]
    ride_out_refs = rest[n_tab + n_ride + 1:n_tab + 2 * n_ride + 1]
    wb_ref = rest[-1]
    j = pl.program_id(0)

    @pl.when(pl.program_id(1) == 0)
    def _cast_weights():
        wb_ref[...] = w_ref[...].astype(BF16)

    for j0, j1, epilogue, rides in segments:
        @pl.when((j >= j0) & (j < j1))
        def _segment():
            for r in rides:
                ride_out_refs[r][...] = ride_refs[r][...].astype(BF16)
            rows = x_ref.shape[0] // PROJ_ROW_CHUNKS
            for c in range(PROJ_ROW_CHUNKS):
                rs = slice(c * rows, (c + 1) * rows)
                acc = jnp.dot(x_ref[rs, :], wb_ref[...], preferred_element_type=F32)
                tabs = [t.at[rs, :] for t in tab_refs]
                o_ref[rs, :] = epilogue(acc, j - j0, *tabs).astype(o_ref.dtype)


def _proj_multi(xb, w, segs, tn, tables=(), rides=(), name="in_proj"):
    t, k = xb.shape
    tm = PROJ_TM
    nt = t // tm
    j0s, bounds = [], 0
    for col0, width, _ in segs:
        assert col0 % LANES == 0 and width % tn == 0
        j0s.append(bounds)
        bounds += width // tn
    n_tiles = bounds

    def w_col(j):
        col = segs[0][0] + j * tn
        for (col0, _, _), j0 in zip(segs[1:], j0s[1:]):
            col = jnp.where(j >= j0, col0 + (j - j0) * tn, col)
        return pl.multiple_of(col, LANES)

    tab = pl.BlockSpec((tm, LANES), lambda j, i: (i, 0))
    in_specs = [pl.BlockSpec((tm, k), lambda j, i: (i, 0)),
                pl.BlockSpec((pl.Element(k), pl.Element(tn)), lambda j, i: (0, w_col(j)))]
    in_specs += [tab] * len(tables)
    out_specs = [pl.BlockSpec((tm, tn), lambda j, i: (i, j))]
    out_shape = [jax.ShapeDtypeStruct((t, n_tiles * tn), BF16)]
    ride_steps = []
    for arr, first, n_slabs in rides:
        slab = arr.shape[0] // n_slabs
        assert slab * n_slabs == arr.shape[0] and slab % 16 == 0 and first + n_slabs <= n_tiles * nt
        spec = pl.BlockSpec(
            (slab, arr.shape[1]),
            lambda j, i, first=first, n=n_slabs: (jnp.clip(j * nt + i - first, 0, n - 1), 0))
        in_specs.append(spec)
        out_specs.append(spec)
        out_shape.append(jax.ShapeDtypeStruct(arr.shape, BF16))
        ride_steps.append((first, first + n_slabs))
    segments = []
    for (_, width, epilogue), j0 in zip(segs, j0s):
        j1 = j0 + width // tn
        active = tuple(r for r, (s0, s1) in enumerate(ride_steps) if s0 < j1 * nt and s1 > j0 * nt)
        segments.append((j0, j1, epilogue, active))
    outs = pl.pallas_call(
        functools.partial(_proj_kernel, tuple(segments), len(tables), len(rides)),
        grid=(n_tiles, nt),
        in_specs=in_specs,
        out_specs=out_specs,
        out_shape=out_shape,
        scratch_shapes=[pltpu.VMEM((k, tn), BF16)],
        compiler_params=_params(("arbitrary", "arbitrary"), "in_proj"),
        name=name,
    )(xb, w, *tables, *[arr for arr, _, _ in rides])
    return outs[0] if not rides else outs


def _retention_kernel(q_ref, k_ref, v_ref, gs_ref, ga_ref, gng_ref, gnb_ref,
                      o_ref, state_ref, dmask_ref, qdec_ref, kdec_ref, cdec_ref):
    c = RET_CHUNK

    @pl.when(pl.program_id(1) == 0)
    def _init():
        state_ref[...] = jnp.zeros_like(state_ref)
        ii = lax.broadcasted_iota(jnp.int32, (c, c), 0)
        jj = lax.broadcasted_iota(jnp.int32, (c, c), 1)
        diff = (ii - jj).astype(F32)
        idx = lax.broadcasted_iota(jnp.int32, (c, RET_DK), 0).astype(F32)
        for h in range(RET_HEADS):
            def log_gamma(shape):
                return jnp.log(1.0 - jnp.exp2(-5.0 - jnp.full(shape, float(h), F32)))

            dmask_ref[h] = jnp.where(
                diff >= 0.0, jnp.exp(log_gamma((c, c)) * jnp.maximum(diff, 0.0)), 0.0)
            lgl = log_gamma((c, RET_DK))
            qdec_ref[h] = jnp.exp(lgl * (idx + 1.0))
            kdec_ref[h] = jnp.exp(lgl * (c - 1.0 - idx))
            cdec_ref[h] = jnp.exp(log_gamma((8, LANES)) * float(c))

    for r0, h in ((r0, h) for r0 in range(0, q_ref.shape[0], c) for h in range(RET_HEADS)):
        rs = slice(r0, r0 + c)
        ksl = slice(h * RET_DK, (h + 1) * RET_DK)
        vsl = slice(h * RET_DV, (h + 1) * RET_DV)
        qb = q_ref[rs, ksl]
        kb = k_ref[rs, ksl]
        v = v_ref[rs, vsl]
        qd = (qb.astype(F32) * qdec_ref[h]).astype(BF16)
        kd = (kb.astype(F32) * kdec_ref[h]).astype(BF16)

        s = lax.dot_general(qb, kb, (((1,), (1,)), ((), ())), preferred_element_type=F32)
        s = s * dmask_ref[h]
        inner = jnp.dot(s.astype(BF16), v, preferred_element_type=F32)
        state = state_ref[h]
        cross = jnp.dot(qd, state.astype(BF16), preferred_element_type=F32)
        kv = lax.dot_general(kd, v, (((0,), (0,)), ((), ())), preferred_element_type=F32)
        state_ref[h] = state * cdec_ref[h, 0:1, 0:1] + kv

        y = inner + cross
        mu = jnp.mean(y, axis=-1, keepdims=True)
        d = y - mu
        var = jnp.mean(d * d, axis=-1, keepdims=True)
        yn = d * lax.rsqrt(var + GN_EPS) * gng_ref[:, vsl] + gnb_ref[:, vsl]
        gate = ga_ref[rs, vsl].astype(F32) * gs_ref[rs, vsl].astype(F32)
        o_ref[rs, vsl] = (gate * yn).astype(o_ref.dtype)


def _retention(qa, vg, gn_g, gn_b, bsz, seq):
    t = bsz * seq
    c = RET_CHUNK
    rows_per_step = RET_STEP_CHUNKS * c
    nc = seq // rows_per_step
    nh = RET_HEADS

    def rows(width, col):
        return pl.BlockSpec((rows_per_step, width), lambda b, n: (b * nc + n, col))

    gn = pl.BlockSpec((1, D_MODEL), lambda b, n: (0, 0))
    return pl.pallas_call(
        _retention_kernel,
        grid=(bsz, nc),
        in_specs=[rows(nh * RET_DK, 0), rows(nh * RET_DK, 1), rows(D_MODEL, 0), rows(D_MODEL, 1),
                  rows(D_MODEL, 2), gn, gn],
        out_specs=rows(D_MODEL, 0),
        out_shape=jax.ShapeDtypeStruct((t, D_MODEL), BF16),
        scratch_shapes=[pltpu.VMEM((nh, RET_DK, RET_DV), F32), pltpu.VMEM((nh, c, c), F32),
                        pltpu.VMEM((nh, c, RET_DK), F32), pltpu.VMEM((nh, c, RET_DK), F32),
                        pltpu.VMEM((nh, 8, LANES), F32)],
        compiler_params=_params(("parallel", "arbitrary"), "retention"),
        name="retention",
    )(qa, qa, vg, vg, vg, gn_g.reshape(1, -1), gn_b.reshape(1, -1))


SINK_LANES = 8


def _pair_blocks(prev, cur, g):
    c0 = (g // 2) * LANES
    kk = jnp.concatenate([prev[:, c0:c0 + LANES], cur[:, c0:c0 + LANES]], axis=0).astype(F32)
    lane = lax.broadcasted_iota(jnp.int32, kk.shape, 1)
    row = lax.broadcasted_iota(jnp.int32, kk.shape, 0)
    hd = ATT_HEAD_DIM
    own = jnp.where(((lane >= hd) if g % 2 else (lane < hd)) & (row != 0), kk, 0.0)
    other = pltpu.roll(own, hd, 1)
    lo, hi = (other, own) if g % 2 else (own, other)
    return lo.astype(BF16), hi.astype(BF16)


def _attention_consts(sink_ref, qaug_ref, kaug_ref, vones_ref):
    nk = 2 * ATT_BLOCK
    row_q = lax.broadcasted_iota(jnp.int32, qaug_ref.shape, 0)
    lane_q = lax.broadcasted_iota(jnp.int32, qaug_ref.shape, 1)
    pair_of_lane = (lane_q & (SINK_LANES - 1)) >> 1
    onehot = (lane_q < 2 * SINK_LANES) & (pair_of_lane == row_q // ATT_BLOCK)
    qaug_ref[...] = jnp.where(onehot, 1.0, 0.0).astype(BF16)
    row_k = lax.broadcasted_iota(jnp.int32, vones_ref.shape, 0)
    lane_k = lax.broadcasted_iota(jnp.int32, vones_ref.shape, 1)
    vones_ref[...] = jnp.where((row_k < nk) == (lane_k < ATT_HEAD_DIM), 1.0, 0.0).astype(BF16)
    for g in range(ATT_KV_HEADS):
        vals = jnp.zeros(vones_ref.shape, F32)
        for j in range(ATT_GROUP):
            sk = sink_ref[g * ATT_GROUP + j] * LOG2E
            vals = jnp.where((lane_k == j) & (row_k == (j % 2) * nk), sk, vals)
        hi = vals.astype(BF16).astype(F32)
        kaug_ref[g] = (hi + pltpu.roll(vals - hi, SINK_LANES, 1)).astype(BF16)


def _attention_kernel(last, sink_ref, q_ref, kp_ref, kc_ref, vp_ref, vc_ref, gb_ref, ret_ref, o_ref,
                      qaug_ref, kaug_ref, vones_ref, *p_refs):
    i = pl.program_id(1)
    bq = ATT_BLOCK
    nk = 2 * bq
    pairs = ATT_GROUP // 2

    @pl.when((pl.program_id(0) == 0) & (i == 0))
    def _init():
        _attention_consts(sink_ref, qaug_ref, kaug_ref, vones_ref)

    def scores(slot):
        kp, kc = kp_ref[...], kc_ref[...]
        qi = lax.broadcasted_iota(jnp.int32, (bq, nk), 0)
        kj = lax.broadcasted_iota(jnp.int32, (bq, nk), 1)
        dist = qi + bq - kj
        kmin = jnp.where(i > 0, 0, bq)
        valid = ((dist >= 0) & (dist < WINDOW) & (kj >= kmin)) | (kj == 0)
        bias = jnp.where(valid, 0.0, -jnp.inf)
        qaug = qaug_ref[...]
        for g in range(ATT_KV_HEADS):
            k_lo, k_hi = _pair_blocks(kp, kc, g)
            kfull = jnp.concatenate([jnp.concatenate([k_lo, k_hi], axis=0), kaug_ref[g]], axis=1)
            col0 = g * ATT_GROUP * ATT_HEAD_DIM
            qp = jnp.concatenate(
                [q_ref[:, col0 + p * LANES:col0 + (p + 1) * LANES] for p in range(pairs)], axis=0)
            qfull = jnp.concatenate([qp, qaug], axis=1)
            s = lax.dot_general(qfull, kfull, (((1,), (1,)), ((), ())),
                                preferred_element_type=F32)
            for p in range(pairs):
                for e in range(2):
                    sj = s[p * bq:(p + 1) * bq, e * nk:(e + 1) * nk] + bias
                    m = jnp.max(sj, axis=-1, keepdims=True)
                    p_refs[slot][g, p * bq:(p + 1) * bq, e * nk:(e + 1) * nk] = (
                        jnp.exp2(sj - m).astype(BF16))

    def values(slot):
        vp, vc = vp_ref[...], vc_ref[...]
        vones = vones_ref[...]
        for g in range(ATT_KV_HEADS):
            v_lo, v_hi = _pair_blocks(vp, vc, g)
            vfull = jnp.concatenate([jnp.concatenate([v_lo, v_hi], axis=0), vones], axis=1)
            o = jnp.dot(p_refs[slot][g], vfull, preferred_element_type=F32)
            col0 = g * ATT_GROUP * ATT_HEAD_DIM
            for p in range(pairs):
                num = o[p * bq:(p + 1) * bq, :LANES]
                den = o[p * bq:(p + 1) * bq, LANES:]
                c1 = col0 + p * LANES
                gate = gb_ref[:, c1:c1 + LANES].astype(F32)
                ret = ret_ref[:, c1:c1 + LANES].astype(F32)
                o_ref[:, c1:c1 + LANES] = (gate * num / den + ret).astype(o_ref.dtype)

    @pl.when(i == 0)
    def _first():
        scores(0)

    for par in (0, 1):
        @pl.when((i > 0) & (i < last) & (lax.rem(i, 2) == par))
        def _steady():
            values(1 - par)
            scores(par)

    @pl.when(i == last)
    def _last():
        values((last - 1) % 2)


def _attention(qa, akv, vg, ret, sinks, bsz, seq):
    t = bsz * seq
    bq = ATT_BLOCK
    nb = seq // bq
    rows = (ATT_GROUP // 2) * bq

    def spec(width, col, lag, back):
        def index(b, i):
            blk = jnp.clip(i - lag, 0, nb - 1)
            return (b * nb + jnp.maximum(blk - back, 0), col)
        return pl.BlockSpec((bq, width), index)

    return pl.pallas_call(
        functools.partial(_attention_kernel, nb),
        grid=(bsz, nb + 1),
        in_specs=[pl.BlockSpec(memory_space=pltpu.SMEM), spec(D_MODEL, 1, 0, 0),
                  spec(KV_W, 0, 0, 1), spec(KV_W, 0, 0, 0),
                  spec(KV_W, 1, 1, 1), spec(KV_W, 1, 1, 0), spec(D_MODEL, 3, 1, 0),
                  spec(D_MODEL, 0, 1, 0)],
        out_specs=spec(D_MODEL, 0, 1, 0),
        out_shape=jax.ShapeDtypeStruct((t, D_MODEL), BF16),
        scratch_shapes=[pltpu.VMEM((rows, LANES), BF16),
                        pltpu.VMEM((ATT_KV_HEADS, 2 * 2 * bq, LANES), BF16),
                        pltpu.VMEM((2 * 2 * bq, LANES), BF16),
                        pltpu.VMEM((ATT_KV_HEADS, rows, 2 * 2 * bq), BF16),
                        pltpu.VMEM((ATT_KV_HEADS, rows, 2 * 2 * bq), BF16)],
        compiler_params=_params(("arbitrary", "arbitrary"), "attention"),
        name="attention",
    )(sinks, qa, akv, akv, akv, akv, vg, ret)


def _out_proj_kernel(alpha, m_ref, x_ref, w_ref, g_ref, b_ref, h_ref, hb_ref):
    rows = h_ref.shape[0] // OUT_PROJ_ROW_CHUNKS
    for c in range(OUT_PROJ_ROW_CHUNKS):
        rs = slice(c * rows, (c + 1) * rows)
        y = jnp.dot(m_ref[rs, :], w_ref[...], preferred_element_type=F32)
        h = _layer_norm(alpha * x_ref[rs, :] + y, g_ref[...], b_ref[...])
        h_ref[rs, :] = h
        hb_ref[rs, :] = h.astype(BF16)


def _out_proj(merged, x2, wb, ln_g, ln_b, alpha):
    t, d = x2.shape
    tm = OUT_TM
    row = pl.BlockSpec((tm, d), lambda i: (i, 0))
    vec = pl.BlockSpec((1, d), lambda i: (0, 0))
    return pl.pallas_call(
        functools.partial(_out_proj_kernel, alpha),
        grid=(t // tm,),
        in_specs=[row, row, pl.BlockSpec((d, d), lambda i: (0, 0)), vec, vec],
        out_specs=[row, row],
        out_shape=[jax.ShapeDtypeStruct((t, d), F32), jax.ShapeDtypeStruct((t, d), BF16)],
        compiler_params=_params(("parallel",), "out_proj"),
        name="out_proj_ln",
    )(merged, x2, wb, ln_g.reshape(1, d), ln_b.reshape(1, d))


def _ffn_kernel(alpha, hb_ref, h_hbm, wg_ref, wu_ref, wd_ref, g_ref, b_ref, o_ref, hres_ref, sem):
    i = pl.program_id(0)
    f = pl.program_id(1)
    last = pl.num_programs(1) - 1
    tm = o_ref.shape[0]
    res_copy = pltpu.make_async_copy(h_hbm.at[pl.ds(i * tm, tm), :], hres_ref, sem)

    def tile():
        hb = hb_ref[...]
        gt = jnp.dot(hb, wg_ref[...], preferred_element_type=F32)
        ut = jnp.dot(hb, wu_ref[...], preferred_element_type=F32)
        a = (gt * _sigmoid(gt) * ut).astype(BF16)
        return jnp.dot(a, wd_ref[...], preferred_element_type=F32)

    @pl.when(f == 0)
    def _first():
        res_copy.start()
        o_ref[...] = tile()

    @pl.when((f > 0) & (f < last))
    def _middle():
        o_ref[...] += tile()

    @pl.when(f == last)
    def _last():
        res_copy.wait()
        z = alpha * hres_ref[...] + (o_ref[...] + tile())
        o_ref[...] = _layer_norm(z, g_ref[...], b_ref[...])


def _ffn(h1, h1b, wg, wu, wd, ln_g, ln_b, alpha):
    t, d = h1.shape
    dff = wg.shape[1]
    tm, tf = FFN_TM, FFN_TF
    row = pl.BlockSpec((tm, d), lambda i, f: (i, 0))
    vec = pl.BlockSpec((1, d), lambda i, f: (0, 0))
    return pl.pallas_call(
        functools.partial(_ffn_kernel, alpha),
        grid=(t // tm, dff // tf),
        in_specs=[row,
                  pl.BlockSpec(memory_space=pl.ANY),
                  pl.BlockSpec((d, tf), lambda i, f: (0, f)),
                  pl.BlockSpec((d, tf), lambda i, f: (0, f)),
                  pl.BlockSpec((tf, d), lambda i, f: (f, 0)),
                  vec, vec],
        out_specs=row,
        out_shape=jax.ShapeDtypeStruct((t, d), F32),
        scratch_shapes=[pltpu.VMEM((tm, d), F32), pltpu.SemaphoreType.DMA(())],
        compiler_params=_params(("parallel", "arbitrary"), "ffn"),
        name="ffn_ln",
    )(h1b, h1, wg, wu, wd, ln_g.reshape(1, d), ln_b.reshape(1, d))


def kernel(x, positions, w_in, ret_gn_g, ret_gn_b, att_sinks, w_out, ln1_g, ln1_b,
           w_gate, w_up, w_down, ln2_g, ln2_b):
    bsz, seq, d = x.shape
    depth = w_in.shape[0]
    alpha = (2.0 * depth) ** 0.25
    h = x.reshape(bsz * seq, d)
    for l in range(depth):
        w = w_in[l]
        cos_r, sin_r, ca, s1, s2, hb, akv = _prep(positions, h, w)
        steps = 4 * (bsz * seq // PROJ_TM)
        qa, wdb, wob = _proj_multi(
            hb, w, [(OFF_RQ, OFF_RV - OFF_RQ, _ep_ret_rotary), (OFF_AQ, D_MODEL, _ep_att_q)], PROJ_TN,
            (cos_r, sin_r, ca, s1, s2), [(w_down[l], 0, steps), (w_out[l], 0, steps)], "in_proj_qk")
        vg, wgb, wub = _proj_multi(
            hb, w, [(OFF_RV, D_MODEL, _ep_plain), (OFF_RG, D_MODEL, _ep_swish),
                    (OFF_GA, 2 * D_MODEL, _ep_sigmoid)], PROJ_TN,
            (), [(w_gate[l], 0, steps), (w_up[l], steps, steps)], "in_proj_vg")
        ret = _retention(qa, vg, ret_gn_g[l], ret_gn_b[l], bsz, seq)
        merged = _attention(qa, akv, vg, ret, att_sinks[l], bsz, seq)
        h, hb = _out_proj(merged, h, wob, ln1_g[l], ln1_b[l], alpha)
        h = _ffn(h, hb, wgb, wub, wdb, ln2_g[l], ln2_b[l], alpha)
    return h.reshape(bsz, seq, d)
```

```python
import functools
import math

import jax
import jax.numpy as jnp
from jax import lax
from jax.experimental import pallas as pl
from jax.experimental.pallas import tpu as pltpu

F32 = jnp.float32
BF16 = jnp.bfloat16

D_MODEL = 2048
RET_HEADS = 4
RET_DV = D_MODEL // RET_HEADS
RET_DK = RET_DV // 2
RET_CHUNK = 256
RET_THETA = 10000.0
ATT_HEAD_DIM = 64
ATT_HEADS = D_MODEL // ATT_HEAD_DIM
ATT_KV_HEADS = ATT_HEADS // 8
ATT_GROUP = ATT_HEADS // ATT_KV_HEADS
WINDOW = 128
ATT_BLOCK = WINDOW
ROPE_THETA = 500000.0
ROPE_DIM = ATT_HEAD_DIM // 4
LN_EPS = 1e-5
GN_EPS = 1e-5

LANES = 128
LOG2E = math.log2(math.e)

OFF_RQ = 0
OFF_RK = OFF_RQ + RET_HEADS * RET_DK
OFF_RV = OFF_RK + RET_HEADS * RET_DK
OFF_RG = OFF_RV + RET_HEADS * RET_DV
OFF_AQ = OFF_RG + RET_HEADS * RET_DV
OFF_AK = OFF_AQ + ATT_HEADS * ATT_HEAD_DIM
OFF_AV = OFF_AK + ATT_KV_HEADS * ATT_HEAD_DIM
OFF_GA = OFF_AV + ATT_KV_HEADS * ATT_HEAD_DIM
OFF_GB = OFF_GA + D_MODEL
KV_W = ATT_KV_HEADS * ATT_HEAD_DIM

PREP_TM = 1024
PROJ_TM, PROJ_TN = 1024, 1024
OUT_TM = 512
FFN_TM, FFN_TF = 1024, 512
RET_STEP_CHUNKS = 2
OUT_PROJ_ROW_CHUNKS = 2
ATT_Q_ROW_CHUNKS = 4
VMEM_MIB = {"prep": 60, "in_proj": 56, "retention": 32, "attention": 32, "out_proj": 48,
            "ffn": 60}


def _params(sem, call):
    return pltpu.CompilerParams(dimension_semantics=sem,
                                vmem_limit_bytes=VMEM_MIB[call] * 1024 * 1024)


def _sigmoid(v):
    return 0.5 * jnp.tanh(0.5 * v) + 0.5


def _layer_norm(z, g, b):
    mu = jnp.mean(z, axis=-1, keepdims=True)
    d = z - mu
    var = jnp.mean(d * d, axis=-1, keepdims=True)
    return d * lax.rsqrt(var + LN_EPS) * g + b


def _prep_kernel(pos_ref, fr_ref, fa_ref, x_ref, w_ref,
                 cr_ref, sr_ref, ca_ref, s1_ref, s2_ref, xb_ref, akv_ref, wb_ref):
    @pl.when(pl.program_id(0) == 0)
    def _cast_weights():
        wb_ref[...] = w_ref[...].astype(BF16)

    xb = x_ref[...].astype(BF16)
    xb_ref[...] = xb
    pos_t = pos_ref[...].astype(F32).T
    pos = jnp.concatenate([jnp.broadcast_to(pos_t[:, c:c + 1], (LANES, LANES))
                           for c in range(pos_t.shape[1])], axis=0)
    ang_r = pos * fr_ref[...]
    cr_ref[...] = jnp.cos(ang_r)
    sr_ref[...] = jnp.sin(ang_r)
    ang_a = pos * fa_ref[...]
    sa = jnp.sin(ang_a)
    d = lax.broadcasted_iota(jnp.int32, ang_a.shape, 1) & (ATT_HEAD_DIM - 1)
    half = ROPE_DIM // 2
    ca_ref[...] = jnp.cos(ang_a)
    s1_ref[...] = jnp.where(d < half, -sa, 0.0)
    s2_ref[...] = jnp.where((d >= half) & (d < ROPE_DIM), sa, 0.0)
    acc = jnp.dot(xb, wb_ref[...], preferred_element_type=F32)
    akv_ref[...] = _ep_att_kv(acc, 0, ca_ref, s1_ref, s2_ref).astype(BF16)


def _prep(positions, x2, w):
    t, dm = x2.shape
    tm = PREP_TM
    kvw = OFF_GA - OFF_AK
    pos_l = positions.reshape(t // LANES, LANES)
    inv_r = 1.0 / (RET_THETA ** jnp.linspace(0.0, 1.0, RET_DK // 2, dtype=F32))
    inv_a = 1.0 / (ROPE_THETA ** (jnp.arange(0, ROPE_DIM, 2, dtype=F32) / ROPE_DIM))
    d = jnp.arange(LANES) % ATT_HEAD_DIM
    fa = jnp.where(d < ROPE_DIM, jnp.tile(inv_a, LANES // inv_a.size), 0.0).astype(F32)
    row = pl.BlockSpec((tm, LANES), lambda i: (i, 0))
    xrow = pl.BlockSpec((tm, dm), lambda i: (i, 0))
    vec = pl.BlockSpec((1, LANES), lambda i: (0, 0))
    tab = jax.ShapeDtypeStruct((t, LANES), F32)
    return pl.pallas_call(
        _prep_kernel,
        grid=(t // tm,),
        in_specs=[pl.BlockSpec((tm // LANES, LANES), lambda i: (i, 0)), vec, vec, xrow,
                  pl.BlockSpec((pl.Element(dm), pl.Element(kvw)), lambda i: (0, OFF_AK))],
        out_specs=[row] * 5 + [xrow, pl.BlockSpec((tm, kvw), lambda i: (i, 0))],
        out_shape=[tab] * 5 + [jax.ShapeDtypeStruct((t, dm), BF16),
                               jax.ShapeDtypeStruct((t, kvw), BF16)],
        scratch_shapes=[pltpu.VMEM((dm, kvw), BF16)],
        compiler_params=_params(("arbitrary",), "prep"),
        name="prep_tables_xcast_kv",
    )(pos_l, inv_r.reshape(1, LANES), fa.reshape(1, LANES), x2, w)


def _ep_plain(acc, j, *tabs):
    return acc


def _ep_swish(acc, j, *tabs):
    return acc * _sigmoid(acc)


def _ep_sigmoid(acc, j, *tabs):
    return _sigmoid(acc)


def _ep_ret_rotary(acc, j, cos_ref, sin_ref, *att_tabs):
    scale = jnp.where(j == 0, RET_DK ** -0.5, 1.0)
    cos = cos_ref[...] * scale
    sin = sin_ref[...] * scale
    half = RET_DK // 2
    outs = []
    for c0 in range(0, acc.shape[1], RET_DK):
        a1 = acc[:, c0:c0 + half]
        a2 = acc[:, c0 + half:c0 + RET_DK]
        outs += [a1 * cos - a2 * sin, a2 * cos + a1 * sin]
    return jnp.concatenate(outs, axis=-1)


def _rope_lanes(a, ca, s1, s2):
    outs = []
    for c0 in range(0, a.shape[1], LANES):
        ac = a[:, c0:c0 + LANES]
        up = pltpu.roll(ac, LANES - ROPE_DIM // 2, 1)
        dn = pltpu.roll(ac, ROPE_DIM // 2, 1)
        outs.append(ac * ca + up * s1 + dn * s2)
    return jnp.concatenate(outs, axis=-1)


def _ep_att_q(acc, j, cos_ref, sin_ref, ca_ref, s1_ref, s2_ref):
    c = (ATT_HEAD_DIM ** -0.5) * LOG2E
    return _rope_lanes(acc, ca_ref[...] * c, s1_ref[...] * c, s2_ref[...] * c)


def _ep_att_kv(acc, j, ca_ref, s1_ref, s2_ref):
    k = _rope_lanes(acc[:, :KV_W], ca_ref[...], s1_ref[...], s2_ref[...])
    return jnp.concatenate([k, acc[:, KV_W:]], axis=-1)


def _proj_kernel(segments, n_tab, n_ride, x_ref, w_ref, *rest):
    tab_refs = rest[:n_tab]
    ride_refs = rest[n_tab:n_tab + n_ride]
    o_ref = rest[n_tab + n_ride]
    ride_out_refs = rest[n_tab + n_ride + 1:n_tab + 2 * n_ride + 1]
    wb_ref = rest[-1]
    j = pl.program_id(0)

    @pl.when(pl.program_id(1) == 0)
    def _cast_weights():
        wb_ref[...] = w_ref[...].astype(BF16)

    for j0, j1, epilogue, row_chunks, rides in segments:
        @pl.when((j >= j0) & (j < j1))
        def _segment():
            for r in rides:
                ride_out_refs[r][...] = ride_refs[r][...].astype(BF16)
            rows = x_ref.shape[0] // row_chunks
            for c in range(row_chunks):
                rs = slice(c * rows, (c + 1) * rows)
                acc = jnp.dot(x_ref[rs, :], wb_ref[...], preferred_element_type=F32)
                tabs = [t.at[rs, :] for t in tab_refs]
                o_ref[rs, :] = epilogue(acc, j - j0, *tabs).astype(o_ref.dtype)


def _proj_multi(xb, w, segs, tn, tables=(), rides=(), name="in_proj"):
    t, k = xb.shape
    tm = PROJ_TM
    nt = t // tm
    j0s, bounds = [], 0
    for col0, width, _, row_chunks in segs:
        assert col0 % LANES == 0 and width % tn == 0 and tm % (16 * row_chunks) == 0
        j0s.append(bounds)
        bounds += width // tn
    n_tiles = bounds

    def w_col(j):
        col = segs[0][0] + j * tn
        for (col0, _, _, _), j0 in zip(segs[1:], j0s[1:]):
            col = jnp.where(j >= j0, col0 + (j - j0) * tn, col)
        return pl.multiple_of(col, LANES)

    tab = pl.BlockSpec((tm, LANES), lambda j, i: (i, 0))
    in_specs = [pl.BlockSpec((tm, k), lambda j, i: (i, 0)),
                pl.BlockSpec((pl.Element(k), pl.Element(tn)), lambda j, i: (0, w_col(j)))]
    in_specs += [tab] * len(tables)
    out_specs = [pl.BlockSpec((tm, tn), lambda j, i: (i, j))]
    out_shape = [jax.ShapeDtypeStruct((t, n_tiles * tn), BF16)]
    ride_steps = []
    for arr, first, n_slabs in rides:
        slab = arr.shape[0] // n_slabs
        assert slab * n_slabs == arr.shape[0] and slab % 16 == 0 and first + n_slabs <= n_tiles * nt
        spec = pl.BlockSpec(
            (slab, arr.shape[1]),
            lambda j, i, first=first, n=n_slabs: (jnp.clip(j * nt + i - first, 0, n - 1), 0))
        in_specs.append(spec)
        out_specs.append(spec)
        out_shape.append(jax.ShapeDtypeStruct(arr.shape, BF16))
        ride_steps.append((first, first + n_slabs))
    segments = []
    for (_, width, epilogue, row_chunks), j0 in zip(segs, j0s):
        j1 = j0 + width // tn
        active = tuple(r for r, (s0, s1) in enumerate(ride_steps) if s0 < j1 * nt and s1 > j0 * nt)
        segments.append((j0, j1, epilogue, row_chunks, active))
    outs = pl.pallas_call(
        functools.partial(_proj_kernel, tuple(segments), len(tables), len(rides)),
        grid=(n_tiles, nt),
        in_specs=in_specs,
        out_specs=out_specs,
        out_shape=out_shape,
        scratch_shapes=[pltpu.VMEM((k, tn), BF16)],
        compiler_params=_params(("arbitrary", "arbitrary"), "in_proj"),
        name=name,
    )(xb, w, *tables, *[arr for arr, _, _ in rides])
    return outs[0] if not rides else outs


def _retention_kernel(q_ref, k_ref, v_ref, gs_ref, ga_ref, gng_ref, gnb_ref,
                      o_ref, state_ref, dmask_ref, qdec_ref, kdec_ref, cdec_ref):
    c = RET_CHUNK

    @pl.when(pl.program_id(1) == 0)
    def _init():
        state_ref[...] = jnp.zeros_like(state_ref)
        ii = lax.broadcasted_iota(jnp.int32, (c, c), 0)
        jj = lax.broadcasted_iota(jnp.int32, (c, c), 1)
        diff = (ii - jj).astype(F32)
        idx = lax.broadcasted_iota(jnp.int32, (c, RET_DK), 0).astype(F32)
        for h in range(RET_HEADS):
            def log_gamma(shape):
                return jnp.log(1.0 - jnp.exp2(-5.0 - jnp.full(shape, float(h), F32)))

            dmask_ref[h] = jnp.where(
                diff >= 0.0, jnp.exp(log_gamma((c, c)) * jnp.maximum(diff, 0.0)), 0.0)
            lgl = log_gamma((c, RET_DK))
            qdec_ref[h] = jnp.exp(lgl * (idx + 1.0))
            kdec_ref[h] = jnp.exp(lgl * (c - 1.0 - idx))
            cdec_ref[h] = jnp.exp(log_gamma((8, LANES)) * float(c))

    for r0, h in ((r0, h) for r0 in range(0, q_ref.shape[0], c) for h in range(RET_HEADS)):
        rs = slice(r0, r0 + c)
        ksl = slice(h * RET_DK, (h + 1) * RET_DK)
        vsl = slice(h * RET_DV, (h + 1) * RET_DV)
        qb = q_ref[rs, ksl]
        kb = k_ref[rs, ksl]
        v = v_ref[rs, vsl]
        qd = (qb.astype(F32) * qdec_ref[h]).astype(BF16)
        kd = (kb.astype(F32) * kdec_ref[h]).astype(BF16)

        s = lax.dot_general(qb, kb, (((1,), (1,)), ((), ())), preferred_element_type=F32)
        s = s * dmask_ref[h]
        inner = jnp.dot(s.astype(BF16), v, preferred_element_type=F32)
        state = state_ref[h]
        cross = jnp.dot(qd, state.astype(BF16), preferred_element_type=F32)
        kv = lax.dot_general(kd, v, (((0,), (0,)), ((), ())), preferred_element_type=F32)
        state_ref[h] = state * cdec_ref[h, 0:1, 0:1] + kv

        y = inner + cross
        mu = jnp.mean(y, axis=-1, keepdims=True)
        d = y - mu
        var = jnp.mean(d * d, axis=-1, keepdims=True)
        yn = d * lax.rsqrt(var + GN_EPS) * gng_ref[:, vsl] + gnb_ref[:, vsl]
        gate = ga_ref[rs, vsl].astype(F32) * gs_ref[rs, vsl].astype(F32)
        o_ref[rs, vsl] = (gate * yn).astype(o_ref.dtype)


def _retention(qa, vg, gn_g, gn_b, bsz, seq):
    t = bsz * seq
    c = RET_CHUNK
    rows_per_step = RET_STEP_CHUNKS * c
    nc = seq // rows_per_step
    nh = RET_HEADS

    def rows(width, col):
        return pl.BlockSpec((rows_per_step, width), lambda b, n: (b * nc + n, col))

    gn = pl.BlockSpec((1, D_MODEL), lambda b, n: (0, 0))
    return pl.pallas_call(
        _retention_kernel,
        grid=(bsz, nc),
        in_specs=[rows(nh * RET_DK, 0), rows(nh * RET_DK, 1), rows(D_MODEL, 0), rows(D_MODEL, 1),
                  rows(D_MODEL, 2), gn, gn],
        out_specs=rows(D_MODEL, 0),
        out_shape=jax.ShapeDtypeStruct((t, D_MODEL), BF16),
        scratch_shapes=[pltpu.VMEM((nh, RET_DK, RET_DV), F32), pltpu.VMEM((nh, c, c), F32),
                        pltpu.VMEM((nh, c, RET_DK), F32), pltpu.VMEM((nh, c, RET_DK), F32),
                        pltpu.VMEM((nh, 8, LANES), F32)],
        compiler_params=_params(("parallel", "arbitrary"), "retention"),
        name="retention",
    )(qa, qa, vg, vg, vg, gn_g.reshape(1, -1), gn_b.reshape(1, -1))


SINK_LANES = 8


def _pair_blocks(prev, cur, g):
    c0 = (g // 2) * LANES
    kk = jnp.concatenate([prev[:, c0:c0 + LANES], cur[:, c0:c0 + LANES]], axis=0).astype(F32)
    lane = lax.broadcasted_iota(jnp.int32, kk.shape, 1)
    row = lax.broadcasted_iota(jnp.int32, kk.shape, 0)
    hd = ATT_HEAD_DIM
    own = jnp.where(((lane >= hd) if g % 2 else (lane < hd)) & (row != 0), kk, 0.0)
    other = pltpu.roll(own, hd, 1)
    lo, hi = (other, own) if g % 2 else (own, other)
    return lo.astype(BF16), hi.astype(BF16)


def _attention_consts(sink_ref, qaug_ref, kaug_ref, vones_ref):
    nk = 2 * ATT_BLOCK
    row_q = lax.broadcasted_iota(jnp.int32, qaug_ref.shape, 0)
    lane_q = lax.broadcasted_iota(jnp.int32, qaug_ref.shape, 1)
    pair_of_lane = (lane_q & (SINK_LANES - 1)) >> 1
    onehot = (lane_q < 2 * SINK_LANES) & (pair_of_lane == row_q // ATT_BLOCK)
    qaug_ref[...] = jnp.where(onehot, 1.0, 0.0).astype(BF16)
    row_k = lax.broadcasted_iota(jnp.int32, vones_ref.shape, 0)
    lane_k = lax.broadcasted_iota(jnp.int32, vones_ref.shape, 1)
    vones_ref[...] = jnp.where((row_k < nk) == (lane_k < ATT_HEAD_DIM), 1.0, 0.0).astype(BF16)
    for g in range(ATT_KV_HEADS):
        vals = jnp.zeros(vones_ref.shape, F32)
        for j in range(ATT_GROUP):
            sk = sink_ref[g * ATT_GROUP + j] * LOG2E
            vals = jnp.where((lane_k == j) & (row_k == (j % 2) * nk), sk, vals)
        hi = vals.astype(BF16).astype(F32)
        kaug_ref[g] = (hi + pltpu.roll(vals - hi, SINK_LANES, 1)).astype(BF16)


def _attention_kernel(last, sink_ref, q_ref, kp_ref, kc_ref, vp_ref, vc_ref, gb_ref, ret_ref, o_ref,
                      qaug_ref, kaug_ref, vones_ref, *p_refs):
    i = pl.program_id(1)
    bq = ATT_BLOCK
    nk = 2 * bq
    pairs = ATT_GROUP // 2

    @pl.when((pl.program_id(0) == 0) & (i == 0))
    def _init():
        _attention_consts(sink_ref, qaug_ref, kaug_ref, vones_ref)

    def scores(slot):
        kp, kc = kp_ref[...], kc_ref[...]
        qi = lax.broadcasted_iota(jnp.int32, (bq, nk), 0)
        kj = lax.broadcasted_iota(jnp.int32, (bq, nk), 1)
        dist = qi + bq - kj
        kmin = jnp.where(i > 0, 0, bq)
        valid = ((dist >= 0) & (dist < WINDOW) & (kj >= kmin)) | (kj == 0)
        bias = jnp.where(valid, 0.0, -jnp.inf)
        qaug = qaug_ref[...]
        for g in range(ATT_KV_HEADS):
            k_lo, k_hi = _pair_blocks(kp, kc, g)
            kfull = jnp.concatenate([jnp.concatenate([k_lo, k_hi], axis=0), kaug_ref[g]], axis=1)
            col0 = g * ATT_GROUP * ATT_HEAD_DIM
            qp = jnp.concatenate(
                [q_ref[:, col0 + p * LANES:col0 + (p + 1) * LANES] for p in range(pairs)], axis=0)
            qfull = jnp.concatenate([qp, qaug], axis=1)
            s = lax.dot_general(qfull, kfull, (((1,), (1,)), ((), ())),
                                preferred_element_type=F32)
            for p in range(pairs):
                for e in range(2):
                    sj = s[p * bq:(p + 1) * bq, e * nk:(e + 1) * nk] + bias
                    m = jnp.max(sj, axis=-1, keepdims=True)
                    p_refs[slot][g, p * bq:(p + 1) * bq, e * nk:(e + 1) * nk] = (
                        jnp.exp2(sj - m).astype(BF16))

    def values(slot):
        vp, vc = vp_ref[...], vc_ref[...]
        vones = vones_ref[...]
        for g in range(ATT_KV_HEADS):
            v_lo, v_hi = _pair_blocks(vp, vc, g)
            vfull = jnp.concatenate([jnp.concatenate([v_lo, v_hi], axis=0), vones], axis=1)
            o = jnp.dot(p_refs[slot][g], vfull, preferred_element_type=F32)
            col0 = g * ATT_GROUP * ATT_HEAD_DIM
            for p in range(pairs):
                num = o[p * bq:(p + 1) * bq, :LANES]
                den = o[p * bq:(p + 1) * bq, LANES:]
                c1 = col0 + p * LANES
                gate = gb_ref[:, c1:c1 + LANES].astype(F32)
                ret = ret_ref[:, c1:c1 + LANES].astype(F32)
                o_ref[:, c1:c1 + LANES] = (gate * num / den + ret).astype(o_ref.dtype)

    @pl.when(i == 0)
    def _first():
        scores(0)

    for par in (0, 1):
        @pl.when((i > 0) & (i < last) & (lax.rem(i, 2) == par))
        def _steady():
            values(1 - par)
            scores(par)

    @pl.when(i == last)
    def _last():
        values((last - 1) % 2)


def _attention(qa, akv, vg, ret, sinks, bsz, seq):
    t = bsz * seq
    bq = ATT_BLOCK
    nb = seq // bq
    rows = (ATT_GROUP // 2) * bq

    def spec(width, col, lag, back):
        def index(b, i):
            blk = jnp.clip(i - lag, 0, nb - 1)
            return (b * nb + jnp.maximum(blk - back, 0), col)
        return pl.BlockSpec((bq, width), index)

    return pl.pallas_call(
        functools.partial(_attention_kernel, nb),
        grid=(bsz, nb + 1),
        in_specs=[pl.BlockSpec(memory_space=pltpu.SMEM), spec(D_MODEL, 1, 0, 0),
                  spec(KV_W, 0, 0, 1), spec(KV_W, 0, 0, 0),
                  spec(KV_W, 1, 1, 1), spec(KV_W, 1, 1, 0), spec(D_MODEL, 3, 1, 0),
                  spec(D_MODEL, 0, 1, 0)],
        out_specs=spec(D_MODEL, 0, 1, 0),
        out_shape=jax.ShapeDtypeStruct((t, D_MODEL), BF16),
        scratch_shapes=[pltpu.VMEM((rows, LANES), BF16),
                        pltpu.VMEM((ATT_KV_HEADS, 2 * 2 * bq, LANES), BF16),
                        pltpu.VMEM((2 * 2 * bq, LANES), BF16),
                        pltpu.VMEM((ATT_KV_HEADS, rows, 2 * 2 * bq), BF16),
                        pltpu.VMEM((ATT_KV_HEADS, rows, 2 * 2 * bq), BF16)],
        compiler_params=_params(("arbitrary", "arbitrary"), "attention"),
        name="attention",
    )(sinks, qa, akv, akv, akv, akv, vg, ret)


def _out_proj_kernel(alpha, m_ref, x_ref, w_ref, g_ref, b_ref, h_ref, hb_ref):
    rows = h_ref.shape[0] // OUT_PROJ_ROW_CHUNKS
    for c in range(OUT_PROJ_ROW_CHUNKS):
        rs = slice(c * rows, (c + 1) * rows)
        y = jnp.dot(m_ref[rs, :], w_ref[...], preferred_element_type=F32)
        h = _layer_norm(alpha * x_ref[rs, :] + y, g_ref[...], b_ref[...])
        h_ref[rs, :] = h
        hb_ref[rs, :] = h.astype(BF16)


def _out_proj(merged, x2, wb, ln_g, ln_b, alpha):
    t, d = x2.shape
    tm = OUT_TM
    row = pl.BlockSpec((tm, d), lambda i: (i, 0))
    vec = pl.BlockSpec((1, d), lambda i: (0, 0))
    return pl.pallas_call(
        functools.partial(_out_proj_kernel, alpha),
        grid=(t // tm,),
        in_specs=[row, row, pl.BlockSpec((d, d), lambda i: (0, 0)), vec, vec],
        out_specs=[row, row],
        out_shape=[jax.ShapeDtypeStruct((t, d), F32), jax.ShapeDtypeStruct((t, d), BF16)],
        compiler_params=_params(("parallel",), "out_proj"),
        name="out_proj_ln",
    )(merged, x2, wb, ln_g.reshape(1, d), ln_b.reshape(1, d))


def _ffn_kernel(alpha, hb_ref, h_hbm, wg_ref, wu_ref, wd_ref, g_ref, b_ref, o_ref, hres_ref, sem):
    i = pl.program_id(0)
    f = pl.program_id(1)
    last = pl.num_programs(1) - 1
    tm = o_ref.shape[0]
    res_copy = pltpu.make_async_copy(h_hbm.at[pl.ds(i * tm, tm), :], hres_ref, sem)

    def tile():
        hb = hb_ref[...]
        gt = jnp.dot(hb, wg_ref[...], preferred_element_type=F32)
        ut = jnp.dot(hb, wu_ref[...], preferred_element_type=F32)
        a = (gt * _sigmoid(gt) * ut).astype(BF16)
        return jnp.dot(a, wd_ref[...], preferred_element_type=F32)

    @pl.when(f == 0)
    def _first():
        res_copy.start()
        o_ref[...] = tile()

    @pl.when((f > 0) & (f < last))
    def _middle():
        o_ref[...] += tile()

    @pl.when(f == last)
    def _last():
        res_copy.wait()
        z = alpha * hres_ref[...] + (o_ref[...] + tile())
        o_ref[...] = _layer_norm(z, g_ref[...], b_ref[...])


def _ffn(h1, h1b, wg, wu, wd, ln_g, ln_b, alpha):
    t, d = h1.shape
    dff = wg.shape[1]
    tm, tf = FFN_TM, FFN_TF
    row = pl.BlockSpec((tm, d), lambda i, f: (i, 0))
    vec = pl.BlockSpec((1, d), lambda i, f: (0, 0))
    return pl.pallas_call(
        functools.partial(_ffn_kernel, alpha),
        grid=(t // tm, dff // tf),
        in_specs=[row,
                  pl.BlockSpec(memory_space=pl.ANY),
                  pl.BlockSpec((d, tf), lambda i, f: (0, f)),
                  pl.BlockSpec((d, tf), lambda i, f: (0, f)),
                  pl.BlockSpec((tf, d), lambda i, f: (f, 0)),
                  vec, vec],
        out_specs=row,
        out_shape=jax.ShapeDtypeStruct((t, d), F32),
        scratch_shapes=[pltpu.VMEM((tm, d), F32), pltpu.SemaphoreType.DMA(())],
        compiler_params=_params(("parallel", "arbitrary"), "ffn"),
        name="ffn_ln",
    )(h1b, h1, wg, wu, wd, ln_g.reshape(1, d), ln_b.reshape(1, d))


def kernel(x, positions, w_in, ret_gn_g, ret_gn_b, att_sinks, w_out, ln1_g, ln1_b,
           w_gate, w_up, w_down, ln2_g, ln2_b):
    bsz, seq, d = x.shape
    depth = w_in.shape[0]
    alpha = (2.0 * depth) ** 0.25
    h = x.reshape(bsz * seq, d)
    for l in range(depth):
        w = w_in[l]
        cos_r, sin_r, ca, s1, s2, hb, akv = _prep(positions, h, w)
        steps = 4 * (bsz * seq // PROJ_TM)
        qa, wdb, wob = _proj_multi(
            hb, w, [(OFF_RQ, OFF_RV - OFF_RQ, _ep_ret_rotary, 1),
                    (OFF_AQ, D_MODEL, _ep_att_q, ATT_Q_ROW_CHUNKS)], PROJ_TN,
            (cos_r, sin_r, ca, s1, s2), [(w_down[l], 0, steps), (w_out[l], 0, steps)], "in_proj_qk")
        vg, wgb, wub = _proj_multi(
            hb, w, [(OFF_RV, D_MODEL, _ep_plain, 1), (OFF_RG, D_MODEL, _ep_swish, 1),
                    (OFF_GA, 2 * D_MODEL, _ep_sigmoid, 1)], PROJ_TN,
            (), [(w_gate[l], 0, steps), (w_up[l], steps, steps)], "in_proj_vg")
        ret = _retention(qa, vg, ret_gn_g[l], ret_gn_b[l], bsz, seq)
        merged = _attention(qa, akv, vg, ret, att_sinks[l], bsz, seq)
        h, hb = _out_proj(merged, h, wob, ln1_g[l], ln1_b[l], alpha)
        h = _ffn(h, hb, wgb, wub, wdb, ln2_g[l], ln2_b[l], alpha)
    return h.reshape(bsz, seq, d)
```

```python
import functools
import math

import jax
import jax.numpy as jnp
from jax import lax
from jax.experimental import pallas as pl
from jax.experimental.pallas import tpu as pltpu

F32 = jnp.float32
BF16 = jnp.bfloat16

D_MODEL = 2048
RET_HEADS = 4
RET_DV = D_MODEL // RET_HEADS
RET_DK = RET_DV // 2
RET_CHUNK = 256
RET_THETA = 10000.0
ATT_HEAD_DIM = 64
ATT_HEADS = D_MODEL // ATT_HEAD_DIM
ATT_KV_HEADS = ATT_HEADS // 8
ATT_GROUP = ATT_HEADS // ATT_KV_HEADS
WINDOW = 128
ATT_BLOCK = WINDOW
ROPE_THETA = 500000.0
ROPE_DIM = ATT_HEAD_DIM // 4
LN_EPS = 1e-5
GN_EPS = 1e-5

LANES = 128
LOG2E = math.log2(math.e)

OFF_RQ = 0
OFF_RK = OFF_RQ + RET_HEADS * RET_DK
OFF_RV = OFF_RK + RET_HEADS * RET_DK
OFF_RG = OFF_RV + RET_HEADS * RET_DV
OFF_AQ = OFF_RG + RET_HEADS * RET_DV
OFF_AK = OFF_AQ + ATT_HEADS * ATT_HEAD_DIM
OFF_AV = OFF_AK + ATT_KV_HEADS * ATT_HEAD_DIM
OFF_GA = OFF_AV + ATT_KV_HEADS * ATT_HEAD_DIM
OFF_GB = OFF_GA + D_MODEL
KV_W = ATT_KV_HEADS * ATT_HEAD_DIM

PREP_TM = 1024
PROJ_TM, PROJ_TN = 1024, 1024
OUT_TM = 512
FFN_TM, FFN_TF = 1024, 512
RET_STEP_CHUNKS = 2
OUT_PROJ_ROW_CHUNKS = 2
ATT_Q_ROW_CHUNKS = 4
VMEM_MIB = {"prep": 60, "in_proj": 56, "retention": 32, "attention": 32, "out_proj": 48,
            "ffn": 60}


def _params(sem, call):
    return pltpu.CompilerParams(dimension_semantics=sem,
                                vmem_limit_bytes=VMEM_MIB[call] * 1024 * 1024)


def _sigmoid(v):
    return 0.5 * jnp.tanh(0.5 * v) + 0.5


def _layer_norm(z, g, b):
    mu = jnp.mean(z, axis=-1, keepdims=True)
    d = z - mu
    var = jnp.mean(d * d, axis=-1, keepdims=True)
    return d * lax.rsqrt(var + LN_EPS) * g + b


def _prep_kernel(pos_ref, fr_ref, fa_ref, x_ref, w_ref,
                 cr_ref, sr_ref, ca_ref, s1_ref, s2_ref, xb_ref, akv_ref, wb_ref):
    @pl.when(pl.program_id(0) == 0)
    def _cast_weights():
        wb_ref[...] = w_ref[...].astype(BF16)

    xb = x_ref[...].astype(BF16)
    xb_ref[...] = xb
    pos_t = pos_ref[...].astype(F32).T
    pos = jnp.concatenate([jnp.broadcast_to(pos_t[:, c:c + 1], (LANES, LANES))
                           for c in range(pos_t.shape[1])], axis=0)
    ang_r = pos * fr_ref[...]
    cr_ref[...] = jnp.cos(ang_r)
    sr_ref[...] = jnp.sin(ang_r)
    ang_a = pos * fa_ref[...]
    sa = jnp.sin(ang_a)
    d = lax.broadcasted_iota(jnp.int32, ang_a.shape, 1) & (ATT_HEAD_DIM - 1)
    half = ROPE_DIM // 2
    ca_ref[...] = jnp.cos(ang_a)
    s1_ref[...] = jnp.where(d < half, -sa, 0.0)
    s2_ref[...] = jnp.where((d >= half) & (d < ROPE_DIM), sa, 0.0)
    acc = jnp.dot(xb, wb_ref[...], preferred_element_type=F32)
    akv_ref[...] = _ep_att_kv(acc, 0, ca_ref, s1_ref, s2_ref).astype(BF16)


def _prep(positions, x2, w):
    t, dm = x2.shape
    tm = PREP_TM
    kvw = OFF_GA - OFF_AK
    pos_l = positions.reshape(t // LANES, LANES)
    inv_r = 1.0 / (RET_THETA ** jnp.linspace(0.0, 1.0, RET_DK // 2, dtype=F32))
    inv_a = 1.0 / (ROPE_THETA ** (jnp.arange(0, ROPE_DIM, 2, dtype=F32) / ROPE_DIM))
    d = jnp.arange(LANES) % ATT_HEAD_DIM
    fa = jnp.where(d < ROPE_DIM, jnp.tile(inv_a, LANES // inv_a.size), 0.0).astype(F32)
    row = pl.BlockSpec((tm, LANES), lambda i: (i, 0))
    xrow = pl.BlockSpec((tm, dm), lambda i: (i, 0))
    vec = pl.BlockSpec((1, LANES), lambda i: (0, 0))
    tab = jax.ShapeDtypeStruct((t, LANES), F32)
    return pl.pallas_call(
        _prep_kernel,
        grid=(t // tm,),
        in_specs=[pl.BlockSpec((tm // LANES, LANES), lambda i: (i, 0)), vec, vec, xrow,
                  pl.BlockSpec((pl.Element(dm), pl.Element(kvw)), lambda i: (0, OFF_AK))],
        out_specs=[row] * 5 + [xrow, pl.BlockSpec((tm, kvw), lambda i: (i, 0))],
        out_shape=[tab] * 5 + [jax.ShapeDtypeStruct((t, dm), BF16),
                               jax.ShapeDtypeStruct((t, kvw), BF16)],
        scratch_shapes=[pltpu.VMEM((dm, kvw), BF16)],
        compiler_params=_params(("arbitrary",), "prep"),
        name="prep_tables_xcast_kv",
    )(pos_l, inv_r.reshape(1, LANES), fa.reshape(1, LANES), x2, w)


def _ep_plain(acc, j, *tabs):
    return acc


def _ep_swish(acc, j, *tabs):
    return acc * _sigmoid(acc)


def _ep_sigmoid(acc, j, *tabs):
    return _sigmoid(acc)


def _ep_ret_rotary(acc, j, cos_ref, sin_ref, *att_tabs):
    scale = jnp.where(j == 0, RET_DK ** -0.5, 1.0)
    cos = cos_ref[...] * scale
    sin = sin_ref[...] * scale
    half = RET_DK // 2
    outs = []
    for c0 in range(0, acc.shape[1], RET_DK):
        a1 = acc[:, c0:c0 + half]
        a2 = acc[:, c0 + half:c0 + RET_DK]
        outs += [a1 * cos - a2 * sin, a2 * cos + a1 * sin]
    return jnp.concatenate(outs, axis=-1)


def _rope_lanes(a, ca, s1, s2):
    outs = []
    for c0 in range(0, a.shape[1], LANES):
        ac = a[:, c0:c0 + LANES]
        up = pltpu.roll(ac, LANES - ROPE_DIM // 2, 1)
        dn = pltpu.roll(ac, ROPE_DIM // 2, 1)
        outs.append(ac * ca + up * s1 + dn * s2)
    return jnp.concatenate(outs, axis=-1)


def _ep_att_q(acc, j, cos_ref, sin_ref, ca_ref, s1_ref, s2_ref):
    c = (ATT_HEAD_DIM ** -0.5) * LOG2E
    return _rope_lanes(acc, ca_ref[...] * c, s1_ref[...] * c, s2_ref[...] * c)


def _ep_att_kv(acc, j, ca_ref, s1_ref, s2_ref):
    k = _rope_lanes(acc[:, :KV_W], ca_ref[...], s1_ref[...], s2_ref[...])
    return jnp.concatenate([k, acc[:, KV_W:]], axis=-1)


def _proj_kernel(segments, n_tab, n_ride, x_ref, w_ref, *rest):
    tab_refs = rest[:n_tab]
    ride_refs = rest[n_tab:n_tab + n_ride]
    o_ref = rest[n_tab + n_ride]
    ride_out_refs = rest[n_tab + n_ride + 1:n_tab + 2 * n_ride + 1]
    wb_ref = rest[-1]
    j = pl.program_id(0)

    @pl.when(pl.program_id(1) == 0)
    def _cast_weights():
        wb_ref[...] = w_ref[...].astype(BF16)

    for j0, j1, epilogue, row_chunks, rides in segments:
        @pl.when((j >= j0) & (j < j1))
        def _segment():
            for r in rides:
                ride_out_refs[r][...] = ride_refs[r][...].astype(BF16)
            rows = x_ref.shape[0] // row_chunks
            for c in range(row_chunks):
                rs = slice(c * rows, (c + 1) * rows)
                acc = jnp.dot(x_ref[rs, :], wb_ref[...], preferred_element_type=F32)
                tabs = [t.at[rs, :] for t in tab_refs]
                o_ref[rs, :] = epilogue(acc, j - j0, *tabs).astype(o_ref.dtype)


def _proj_multi(xb, w, segs, tn, tables, table_tiles, rides, name):
    t, k = xb.shape
    tm = PROJ_TM
    nt = t // tm
    j0s, bounds = [], 0
    for col0, width, _, row_chunks in segs:
        assert col0 % LANES == 0 and width % tn == 0 and tm % (16 * row_chunks) == 0
        j0s.append(bounds)
        bounds += width // tn
    n_tiles = bounds

    def w_col(j):
        col = segs[0][0] + j * tn
        for (col0, _, _, _), j0 in zip(segs[1:], j0s[1:]):
            col = jnp.where(j >= j0, col0 + (j - j0) * tn, col)
        return pl.multiple_of(col, LANES)

    tab = pl.BlockSpec((tm, LANES), lambda j, i: (jnp.where(j < table_tiles, i, 0), 0))
    in_specs = [pl.BlockSpec((tm, k), lambda j, i: (i, 0)),
                pl.BlockSpec((pl.Element(k), pl.Element(tn)), lambda j, i: (0, w_col(j)))]
    in_specs += [tab] * len(tables)
    out_specs = [pl.BlockSpec((tm, tn), lambda j, i: (i, j))]
    out_shape = [jax.ShapeDtypeStruct((t, n_tiles * tn), BF16)]
    ride_steps = []
    for arr, first, n_slabs in rides:
        slab = arr.shape[0] // n_slabs
        assert slab * n_slabs == arr.shape[0] and slab % 16 == 0 and first + n_slabs <= n_tiles * nt
        spec = pl.BlockSpec(
            (slab, arr.shape[1]),
            lambda j, i, first=first, n=n_slabs: (jnp.clip(j * nt + i - first, 0, n - 1), 0))
        in_specs.append(spec)
        out_specs.append(spec)
        out_shape.append(jax.ShapeDtypeStruct(arr.shape, BF16))
        ride_steps.append((first, first + n_slabs))
    segments = []
    for (_, width, epilogue, row_chunks), j0 in zip(segs, j0s):
        j1 = j0 + width // tn
        active = tuple(r for r, (s0, s1) in enumerate(ride_steps) if s0 < j1 * nt and s1 > j0 * nt)
        segments.append((j0, j1, epilogue, row_chunks, active))
    outs = pl.pallas_call(
        functools.partial(_proj_kernel, tuple(segments), len(tables), len(rides)),
        grid=(n_tiles, nt),
        in_specs=in_specs,
        out_specs=out_specs,
        out_shape=out_shape,
        scratch_shapes=[pltpu.VMEM((k, tn), BF16)],
        compiler_params=_params(("arbitrary", "arbitrary"), "in_proj"),
        name=name,
    )(xb, w, *tables, *[arr for arr, _, _ in rides])
    return outs[0] if not rides else outs


def _retention_kernel(q_ref, k_ref, v_ref, gs_ref, ga_ref, gng_ref, gnb_ref,
                      o_ref, state_ref, dmask_ref, qdec_ref, kdec_ref, cdec_ref):
    c = RET_CHUNK

    @pl.when(pl.program_id(1) == 0)
    def _init():
        state_ref[...] = jnp.zeros_like(state_ref)
        ii = lax.broadcasted_iota(jnp.int32, (c, c), 0)
        jj = lax.broadcasted_iota(jnp.int32, (c, c), 1)
        diff = (ii - jj).astype(F32)
        idx = lax.broadcasted_iota(jnp.int32, (c, RET_DK), 0).astype(F32)
        for h in range(RET_HEADS):
            def log_gamma(shape):
                return jnp.log(1.0 - jnp.exp2(-5.0 - jnp.full(shape, float(h), F32)))

            dmask_ref[h] = jnp.where(
                diff >= 0.0, jnp.exp(log_gamma((c, c)) * jnp.maximum(diff, 0.0)), 0.0)
            lgl = log_gamma((c, RET_DK))
            qdec_ref[h] = jnp.exp(lgl * (idx + 1.0))
            kdec_ref[h] = jnp.exp(lgl * (c - 1.0 - idx))
            cdec_ref[h] = jnp.exp(log_gamma((8, LANES)) * float(c))

    for r0, h in ((r0, h) for r0 in range(0, q_ref.shape[0], c) for h in range(RET_HEADS)):
        rs = slice(r0, r0 + c)
        ksl = slice(h * RET_DK, (h + 1) * RET_DK)
        vsl = slice(h * RET_DV, (h + 1) * RET_DV)
        qb = q_ref[rs, ksl]
        kb = k_ref[rs, ksl]
        v = v_ref[rs, vsl]
        qd = (qb.astype(F32) * qdec_ref[h]).astype(BF16)
        kd = (kb.astype(F32) * kdec_ref[h]).astype(BF16)

        s = lax.dot_general(qb, kb, (((1,), (1,)), ((), ())), preferred_element_type=F32)
        s = s * dmask_ref[h]
        inner = jnp.dot(s.astype(BF16), v, preferred_element_type=F32)
        state = state_ref[h]
        cross = jnp.dot(qd, state.astype(BF16), preferred_element_type=F32)
        kv = lax.dot_general(kd, v, (((0,), (0,)), ((), ())), preferred_element_type=F32)
        state_ref[h] = state * cdec_ref[h, 0:1, 0:1] + kv

        y = inner + cross
        mu = jnp.mean(y, axis=-1, keepdims=True)
        d = y - mu
        var = jnp.mean(d * d, axis=-1, keepdims=True)
        yn = d * lax.rsqrt(var + GN_EPS) * gng_ref[:, vsl] + gnb_ref[:, vsl]
        gate = ga_ref[rs, vsl].astype(F32) * gs_ref[rs, vsl].astype(F32)
        o_ref[rs, vsl] = (gate * yn).astype(o_ref.dtype)


def _retention(proj, gn_g, gn_b, bsz, seq):
    t = bsz * seq
    c = RET_CHUNK
    rows_per_step = RET_STEP_CHUNKS * c
    nc = seq // rows_per_step
    nh = RET_HEADS

    def rows(width, col):
        return pl.BlockSpec((rows_per_step, width), lambda b, n: (b * nc + n, col))

    gn = pl.BlockSpec((1, D_MODEL), lambda b, n: (0, 0))
    return pl.pallas_call(
        _retention_kernel,
        grid=(bsz, nc),
        in_specs=[rows(nh * RET_DK, 0), rows(nh * RET_DK, 1), rows(D_MODEL, 2), rows(D_MODEL, 3),
                  rows(D_MODEL, 4), gn, gn],
        out_specs=rows(D_MODEL, 0),
        out_shape=jax.ShapeDtypeStruct((t, D_MODEL), BF16),
        scratch_shapes=[pltpu.VMEM((nh, RET_DK, RET_DV), F32), pltpu.VMEM((nh, c, c), F32),
                        pltpu.VMEM((nh, c, RET_DK), F32), pltpu.VMEM((nh, c, RET_DK), F32),
                        pltpu.VMEM((nh, 8, LANES), F32)],
        compiler_params=_params(("parallel", "arbitrary"), "retention"),
        name="retention",
    )(proj, proj, proj, proj, proj, gn_g.reshape(1, -1), gn_b.reshape(1, -1))


SINK_LANES = 8


def _pair_blocks(prev, cur, g):
    c0 = (g // 2) * LANES
    kk = jnp.concatenate([prev[:, c0:c0 + LANES], cur[:, c0:c0 + LANES]], axis=0).astype(F32)
    lane = lax.broadcasted_iota(jnp.int32, kk.shape, 1)
    row = lax.broadcasted_iota(jnp.int32, kk.shape, 0)
    hd = ATT_HEAD_DIM
    own = jnp.where(((lane >= hd) if g % 2 else (lane < hd)) & (row != 0), kk, 0.0)
    other = pltpu.roll(own, hd, 1)
    lo, hi = (other, own) if g % 2 else (own, other)
    return lo.astype(BF16), hi.astype(BF16)


def _attention_consts(sink_ref, qaug_ref, kaug_ref, vones_ref):
    nk = 2 * ATT_BLOCK
    row_q = lax.broadcasted_iota(jnp.int32, qaug_ref.shape, 0)
    lane_q = lax.broadcasted_iota(jnp.int32, qaug_ref.shape, 1)
    pair_of_lane = (lane_q & (SINK_LANES - 1)) >> 1
    onehot = (lane_q < 2 * SINK_LANES) & (pair_of_lane == row_q // ATT_BLOCK)
    qaug_ref[...] = jnp.where(onehot, 1.0, 0.0).astype(BF16)
    row_k = lax.broadcasted_iota(jnp.int32, vones_ref.shape, 0)
    lane_k = lax.broadcasted_iota(jnp.int32, vones_ref.shape, 1)
    vones_ref[...] = jnp.where((row_k < nk) == (lane_k < ATT_HEAD_DIM), 1.0, 0.0).astype(BF16)
    for g in range(ATT_KV_HEADS):
        vals = jnp.zeros(vones_ref.shape, F32)
        for j in range(ATT_GROUP):
            sk = sink_ref[g * ATT_GROUP + j] * LOG2E
            vals = jnp.where((lane_k == j) & (row_k == (j % 2) * nk), sk, vals)
        hi = vals.astype(BF16).astype(F32)
        kaug_ref[g] = (hi + pltpu.roll(vals - hi, SINK_LANES, 1)).astype(BF16)


def _attention_kernel(last, sink_ref, q_ref, kp_ref, kc_ref, vp_ref, vc_ref, gb_ref, ret_ref, o_ref,
                      qaug_ref, kaug_ref, vones_ref, *p_refs):
    i = pl.program_id(1)
    bq = ATT_BLOCK
    nk = 2 * bq
    pairs = ATT_GROUP // 2

    @pl.when((pl.program_id(0) == 0) & (i == 0))
    def _init():
        _attention_consts(sink_ref, qaug_ref, kaug_ref, vones_ref)

    def scores(slot):
        kp, kc = kp_ref[...], kc_ref[...]
        qi = lax.broadcasted_iota(jnp.int32, (bq, nk), 0)
        kj = lax.broadcasted_iota(jnp.int32, (bq, nk), 1)
        dist = qi + bq - kj
        kmin = jnp.where(i > 0, 0, bq)
        valid = ((dist >= 0) & (dist < WINDOW) & (kj >= kmin)) | (kj == 0)
        bias = jnp.where(valid, 0.0, -jnp.inf)
        qaug = qaug_ref[...]
        for g in range(ATT_KV_HEADS):
            k_lo, k_hi = _pair_blocks(kp, kc, g)
            kfull = jnp.concatenate([jnp.concatenate([k_lo, k_hi], axis=0), kaug_ref[g]], axis=1)
            col0 = g * ATT_GROUP * ATT_HEAD_DIM
            qp = jnp.concatenate(
                [q_ref[:, col0 + p * LANES:col0 + (p + 1) * LANES] for p in range(pairs)], axis=0)
            qfull = jnp.concatenate([qp, qaug], axis=1)
            s = lax.dot_general(qfull, kfull, (((1,), (1,)), ((), ())),
                                preferred_element_type=F32)
            for p in range(pairs):
                for e in range(2):
                    sj = s[p * bq:(p + 1) * bq, e * nk:(e + 1) * nk] + bias
                    m = jnp.max(sj, axis=-1, keepdims=True)
                    p_refs[slot][g, p * bq:(p + 1) * bq, e * nk:(e + 1) * nk] = (
                        jnp.exp2(sj - m).astype(BF16))

    def values(slot):
        vp, vc = vp_ref[...], vc_ref[...]
        vones = vones_ref[...]
        for g in range(ATT_KV_HEADS):
            v_lo, v_hi = _pair_blocks(vp, vc, g)
            vfull = jnp.concatenate([jnp.concatenate([v_lo, v_hi], axis=0), vones], axis=1)
            o = jnp.dot(p_refs[slot][g], vfull, preferred_element_type=F32)
            col0 = g * ATT_GROUP * ATT_HEAD_DIM
            for p in range(pairs):
                num = o[p * bq:(p + 1) * bq, :LANES]
                den = o[p * bq:(p + 1) * bq, LANES:]
                c1 = col0 + p * LANES
                gate = gb_ref[:, c1:c1 + LANES].astype(F32)
                ret = ret_ref[:, c1:c1 + LANES].astype(F32)
                o_ref[:, c1:c1 + LANES] = (gate * num / den + ret).astype(o_ref.dtype)

    @pl.when(i == 0)
    def _first():
        scores(0)

    for par in (0, 1):
        @pl.when((i > 0) & (i < last) & (lax.rem(i, 2) == par))
        def _steady():
            values(1 - par)
            scores(par)

    @pl.when(i == last)
    def _last():
        values((last - 1) % 2)


def _attention(proj, akv, ret, sinks, bsz, seq):
    t = bsz * seq
    bq = ATT_BLOCK
    nb = seq // bq
    rows = (ATT_GROUP // 2) * bq

    def spec(width, col, lag, back):
        def index(b, i):
            blk = jnp.clip(i - lag, 0, nb - 1)
            return (b * nb + jnp.maximum(blk - back, 0), col)
        return pl.BlockSpec((bq, width), index)

    return pl.pallas_call(
        functools.partial(_attention_kernel, nb),
        grid=(bsz, nb + 1),
        in_specs=[pl.BlockSpec(memory_space=pltpu.SMEM), spec(D_MODEL, 1, 0, 0),
                  spec(KV_W, 0, 0, 1), spec(KV_W, 0, 0, 0),
                  spec(KV_W, 1, 1, 1), spec(KV_W, 1, 1, 0), spec(D_MODEL, 5, 1, 0),
                  spec(D_MODEL, 0, 1, 0)],
        out_specs=spec(D_MODEL, 0, 1, 0),
        out_shape=jax.ShapeDtypeStruct((t, D_MODEL), BF16),
        scratch_shapes=[pltpu.VMEM((rows, LANES), BF16),
                        pltpu.VMEM((ATT_KV_HEADS, 2 * 2 * bq, LANES), BF16),
                        pltpu.VMEM((2 * 2 * bq, LANES), BF16),
                        pltpu.VMEM((ATT_KV_HEADS, rows, 2 * 2 * bq), BF16),
                        pltpu.VMEM((ATT_KV_HEADS, rows, 2 * 2 * bq), BF16)],
        compiler_params=_params(("arbitrary", "arbitrary"), "attention"),
        name="attention",
    )(sinks, proj, akv, akv, akv, akv, proj, ret)


def _out_proj_kernel(alpha, m_ref, x_ref, w_ref, g_ref, b_ref, h_ref, hb_ref):
    rows = h_ref.shape[0] // OUT_PROJ_ROW_CHUNKS
    for c in range(OUT_PROJ_ROW_CHUNKS):
        rs = slice(c * rows, (c + 1) * rows)
        y = jnp.dot(m_ref[rs, :], w_ref[...], preferred_element_type=F32)
        h = _layer_norm(alpha * x_ref[rs, :] + y, g_ref[...], b_ref[...])
        h_ref[rs, :] = h
        hb_ref[rs, :] = h.astype(BF16)


def _out_proj(merged, x2, wb, ln_g, ln_b, alpha):
    t, d = x2.shape
    tm = OUT_TM
    row = pl.BlockSpec((tm, d), lambda i: (i, 0))
    vec = pl.BlockSpec((1, d), lambda i: (0, 0))
    return pl.pallas_call(
        functools.partial(_out_proj_kernel, alpha),
        grid=(t // tm,),
        in_specs=[row, row, pl.BlockSpec((d, d), lambda i: (0, 0)), vec, vec],
        out_specs=[row, row],
        out_shape=[jax.ShapeDtypeStruct((t, d), F32), jax.ShapeDtypeStruct((t, d), BF16)],
        compiler_params=_params(("parallel",), "out_proj"),
        name="out_proj_ln",
    )(merged, x2, wb, ln_g.reshape(1, d), ln_b.reshape(1, d))


def _ffn_kernel(alpha, hb_ref, h_hbm, wg_ref, wu_ref, wd_ref, g_ref, b_ref, o_ref, hres_ref, sem):
    i = pl.program_id(0)
    f = pl.program_id(1)
    last = pl.num_programs(1) - 1
    tm = o_ref.shape[0]
    res_copy = pltpu.make_async_copy(h_hbm.at[pl.ds(i * tm, tm), :], hres_ref, sem)

    def tile():
        hb = hb_ref[...]
        gt = jnp.dot(hb, wg_ref[...], preferred_element_type=F32)
        ut = jnp.dot(hb, wu_ref[...], preferred_element_type=F32)
        a = (gt * _sigmoid(gt) * ut).astype(BF16)
        return jnp.dot(a, wd_ref[...], preferred_element_type=F32)

    @pl.when(f == 0)
    def _first():
        res_copy.start()
        o_ref[...] = tile()

    @pl.when((f > 0) & (f < last))
    def _middle():
        o_ref[...] += tile()

    @pl.when(f == last)
    def _last():
        res_copy.wait()
        z = alpha * hres_ref[...] + (o_ref[...] + tile())
        o_ref[...] = _layer_norm(z, g_ref[...], b_ref[...])


def _ffn(h1, h1b, wg, wu, wd, ln_g, ln_b, alpha):
    t, d = h1.shape
    dff = wg.shape[1]
    tm, tf = FFN_TM, FFN_TF
    row = pl.BlockSpec((tm, d), lambda i, f: (i, 0))
    vec = pl.BlockSpec((1, d), lambda i, f: (0, 0))
    return pl.pallas_call(
        functools.partial(_ffn_kernel, alpha),
        grid=(t // tm, dff // tf),
        in_specs=[row,
                  pl.BlockSpec(memory_space=pl.ANY),
                  pl.BlockSpec((d, tf), lambda i, f: (0, f)),
                  pl.BlockSpec((d, tf), lambda i, f: (0, f)),
                  pl.BlockSpec((tf, d), lambda i, f: (f, 0)),
                  vec, vec],
        out_specs=row,
        out_shape=jax.ShapeDtypeStruct((t, d), F32),
        scratch_shapes=[pltpu.VMEM((tm, d), F32), pltpu.SemaphoreType.DMA(())],
        compiler_params=_params(("parallel", "arbitrary"), "ffn"),
        name="ffn_ln",
    )(h1b, h1, wg, wu, wd, ln_g.reshape(1, d), ln_b.reshape(1, d))


def kernel(x, positions, w_in, ret_gn_g, ret_gn_b, att_sinks, w_out, ln1_g, ln1_b,
           w_gate, w_up, w_down, ln2_g, ln2_b):
    bsz, seq, d = x.shape
    depth = w_in.shape[0]
    alpha = (2.0 * depth) ** 0.25
    h = x.reshape(bsz * seq, d)
    for l in range(depth):
        w = w_in[l]
        cos_r, sin_r, ca, s1, s2, hb, akv = _prep(positions, h, w)
        steps = 4 * (bsz * seq // PROJ_TM)
        proj, wdb, wob, wgb, wub = _proj_multi(
            hb, w, [(OFF_RQ, OFF_RV - OFF_RQ, _ep_ret_rotary, 1),
                    (OFF_AQ, D_MODEL, _ep_att_q, ATT_Q_ROW_CHUNKS),
                    (OFF_RV, D_MODEL, _ep_plain, 1), (OFF_RG, D_MODEL, _ep_swish, 1),
                    (OFF_GA, 2 * D_MODEL, _ep_sigmoid, 1)], PROJ_TN,
            (cos_r, sin_r, ca, s1, s2), 2 * D_MODEL // PROJ_TN,
            [(w_down[l], 0, steps), (w_out[l], 0, steps), (w_gate[l], steps, steps),
             (w_up[l], 2 * steps, steps)], "in_proj")
        ret = _retention(proj, ret_gn_g[l], ret_gn_b[l], bsz, seq)
        merged = _attention(proj, akv, ret, att_sinks[l], bsz, seq)
        h, hb = _out_proj(merged, h, wob, ln1_g[l], ln1_b[l], alpha)
        h = _ffn(h, hb, wgb, wub, wdb, ln2_g[l], ln2_b[l], alpha)
    return h.reshape(bsz, seq, d)
```

```python
import functools
import math

import jax
import jax.numpy as jnp
from jax import lax
from jax.experimental import pallas as pl
from jax.experimental.pallas import tpu as pltpu

F32 = jnp.float32
BF16 = jnp.bfloat16

D_MODEL = 2048
RET_HEADS = 4
RET_DV = D_MODEL // RET_HEADS
RET_DK = RET_DV // 2
RET_CHUNK = 256
RET_THETA = 10000.0
ATT_HEAD_DIM = 64
ATT_HEADS = D_MODEL // ATT_HEAD_DIM
ATT_KV_HEADS = ATT_HEADS // 8
ATT_GROUP = ATT_HEADS // ATT_KV_HEADS
WINDOW = 128
ATT_BLOCK = WINDOW
ROPE_THETA = 500000.0
ROPE_DIM = ATT_HEAD_DIM // 4
LN_EPS = 1e-5
GN_EPS = 1e-5

LANES = 128
LOG2E = math.log2(math.e)

OFF_RQ = 0
OFF_RK = OFF_RQ + RET_HEADS * RET_DK
OFF_RV = OFF_RK + RET_HEADS * RET_DK
OFF_RG = OFF_RV + RET_HEADS * RET_DV
OFF_AQ = OFF_RG + RET_HEADS * RET_DV
OFF_AK = OFF_AQ + ATT_HEADS * ATT_HEAD_DIM
OFF_AV = OFF_AK + ATT_KV_HEADS * ATT_HEAD_DIM
OFF_GA = OFF_AV + ATT_KV_HEADS * ATT_HEAD_DIM
OFF_GB = OFF_GA + D_MODEL
KV_W = ATT_KV_HEADS * ATT_HEAD_DIM

PREP_TM = 1024
PROJ_TM, PROJ_TN = 1024, 1024
OUT_TM = 512
FFN_TM, FFN_TF = 1024, 512
RET_STEP_CHUNKS = 2
OUT_PROJ_ROW_CHUNKS = 2
ATT_Q_ROW_CHUNKS = 4
VMEM_MIB = {"prep": 60, "in_proj": 56, "retention": 32, "attention": 32, "out_proj": 48,
            "ffn": 60}


def _params(sem, call):
    return pltpu.CompilerParams(dimension_semantics=sem,
                                vmem_limit_bytes=VMEM_MIB[call] * 1024 * 1024)


def _sigmoid(v):
    return 0.5 * jnp.tanh(0.5 * v) + 0.5


def _layer_norm(z, g, b):
    mu = jnp.mean(z, axis=-1, keepdims=True)
    d = z - mu
    var = jnp.mean(d * d, axis=-1, keepdims=True)
    return d * lax.rsqrt(var + LN_EPS) * g + b


def _prep_kernel(pos_ref, fr_ref, fa_ref, x_ref, w_ref,
                 ang_ref, ca_ref, s1_ref, s2_ref, xb_ref, akv_ref, wb_ref):
    @pl.when(pl.program_id(0) == 0)
    def _cast_weights():
        wb_ref[...] = w_ref[...].astype(BF16)

    xb = x_ref[...].astype(BF16)
    xb_ref[...] = xb
    pos_t = pos_ref[...].astype(F32).T
    pos = jnp.concatenate([jnp.broadcast_to(pos_t[:, c:c + 1], (LANES, LANES))
                           for c in range(pos_t.shape[1])], axis=0)
    ang_ref[...] = pos * fr_ref[...]
    ang_a = pos * fa_ref[...]
    sa = jnp.sin(ang_a)
    d = lax.broadcasted_iota(jnp.int32, ang_a.shape, 1) & (ATT_HEAD_DIM - 1)
    half = ROPE_DIM // 2
    ca_ref[...] = jnp.cos(ang_a)
    s1_ref[...] = jnp.where(d < half, -sa, 0.0)
    s2_ref[...] = jnp.where((d >= half) & (d < ROPE_DIM), sa, 0.0)
    acc = jnp.dot(xb, wb_ref[...], preferred_element_type=F32)
    akv_ref[...] = _ep_att_kv(acc, 0, ca_ref, s1_ref, s2_ref).astype(BF16)


def _prep(positions, x2, w):
    t, dm = x2.shape
    tm = PREP_TM
    kvw = OFF_GA - OFF_AK
    pos_l = positions.reshape(t // LANES, LANES)
    inv_r = 1.0 / (RET_THETA ** jnp.linspace(0.0, 1.0, RET_DK // 2, dtype=F32))
    inv_a = 1.0 / (ROPE_THETA ** (jnp.arange(0, ROPE_DIM, 2, dtype=F32) / ROPE_DIM))
    d = jnp.arange(LANES) % ATT_HEAD_DIM
    fa = jnp.where(d < ROPE_DIM, jnp.tile(inv_a, LANES // inv_a.size), 0.0).astype(F32)
    row = pl.BlockSpec((tm, LANES), lambda i: (i, 0))
    xrow = pl.BlockSpec((tm, dm), lambda i: (i, 0))
    vec = pl.BlockSpec((1, LANES), lambda i: (0, 0))
    tab = jax.ShapeDtypeStruct((t, LANES), F32)
    return pl.pallas_call(
        _prep_kernel,
        grid=(t // tm,),
        in_specs=[pl.BlockSpec((tm // LANES, LANES), lambda i: (i, 0)), vec, vec, xrow,
                  pl.BlockSpec((pl.Element(dm), pl.Element(kvw)), lambda i: (0, OFF_AK))],
        out_specs=[row] * 4 + [xrow, pl.BlockSpec((tm, kvw), lambda i: (i, 0))],
        out_shape=[tab] * 4 + [jax.ShapeDtypeStruct((t, dm), BF16),
                               jax.ShapeDtypeStruct((t, kvw), BF16)],
        scratch_shapes=[pltpu.VMEM((dm, kvw), BF16)],
        compiler_params=_params(("arbitrary",), "prep"),
        name="prep_tables_xcast_kv",
    )(pos_l, inv_r.reshape(1, LANES), fa.reshape(1, LANES), x2, w)


def _ep_plain(acc, j, *tabs):
    return acc


def _ep_swish(acc, j, *tabs):
    return acc * _sigmoid(acc)


def _ep_sigmoid(acc, j, *tabs):
    return _sigmoid(acc)


def _ep_ret_rotary(acc, j, ang_ref, *att_tabs):
    scale = jnp.where(j == 0, RET_DK ** -0.5, 1.0)
    ang = ang_ref[...]
    cos = jnp.cos(ang) * scale
    sin = jnp.sin(ang) * scale
    half = RET_DK // 2
    outs = []
    for c0 in range(0, acc.shape[1], RET_DK):
        a1 = acc[:, c0:c0 + half]
        a2 = acc[:, c0 + half:c0 + RET_DK]
        outs += [a1 * cos - a2 * sin, a2 * cos + a1 * sin]
    return jnp.concatenate(outs, axis=-1)


def _rope_lanes(a, ca, s1, s2):
    outs = []
    for c0 in range(0, a.shape[1], LANES):
        ac = a[:, c0:c0 + LANES]
        up = pltpu.roll(ac, LANES - ROPE_DIM // 2, 1)
        dn = pltpu.roll(ac, ROPE_DIM // 2, 1)
        outs.append(ac * ca + up * s1 + dn * s2)
    return jnp.concatenate(outs, axis=-1)


def _ep_att_q(acc, j, ang_ref, ca_ref, s1_ref, s2_ref):
    c = (ATT_HEAD_DIM ** -0.5) * LOG2E
    return _rope_lanes(acc, ca_ref[...] * c, s1_ref[...] * c, s2_ref[...] * c)


def _ep_att_kv(acc, j, ca_ref, s1_ref, s2_ref):
    k = _rope_lanes(acc[:, :KV_W], ca_ref[...], s1_ref[...], s2_ref[...])
    return jnp.concatenate([k, acc[:, KV_W:]], axis=-1)


def _proj_kernel(segments, n_tab, n_ride, x_ref, w_ref, *rest):
    tab_refs = rest[:n_tab]
    ride_refs = rest[n_tab:n_tab + n_ride]
    o_ref = rest[n_tab + n_ride]
    ride_out_refs = rest[n_tab + n_ride + 1:n_tab + 2 * n_ride + 1]
    wb_ref = rest[-1]
    j = pl.program_id(0)

    @pl.when(pl.program_id(1) == 0)
    def _cast_weights():
        wb_ref[...] = w_ref[...].astype(BF16)

    for j0, j1, epilogue, row_chunks, rides in segments:
        @pl.when((j >= j0) & (j < j1))
        def _segment():
            for r in rides:
                ride_out_refs[r][...] = ride_refs[r][...].astype(BF16)
            rows = x_ref.shape[0] // row_chunks
            for c in range(row_chunks):
                rs = slice(c * rows, (c + 1) * rows)
                acc = jnp.dot(x_ref[rs, :], wb_ref[...], preferred_element_type=F32)
                tabs = [t.at[rs, :] for t in tab_refs]
                o_ref[rs, :] = epilogue(acc, j - j0, *tabs).astype(o_ref.dtype)


def _proj_multi(xb, w, segs, tn, tables=(), rides=(), name="in_proj"):
    t, k = xb.shape
    tm = PROJ_TM
    nt = t // tm
    j0s, bounds = [], 0
    for col0, width, _, row_chunks in segs:
        assert col0 % LANES == 0 and width % tn == 0 and tm % (16 * row_chunks) == 0
        j0s.append(bounds)
        bounds += width // tn
    n_tiles = bounds

    def w_col(j):
        col = segs[0][0] + j * tn
        for (col0, _, _, _), j0 in zip(segs[1:], j0s[1:]):
            col = jnp.where(j >= j0, col0 + (j - j0) * tn, col)
        return pl.multiple_of(col, LANES)

    tab = pl.BlockSpec((tm, LANES), lambda j, i: (i, 0))
    in_specs = [pl.BlockSpec((tm, k), lambda j, i: (i, 0)),
                pl.BlockSpec((pl.Element(k), pl.Element(tn)), lambda j, i: (0, w_col(j)))]
    in_specs += [tab] * len(tables)
    out_specs = [pl.BlockSpec((tm, tn), lambda j, i: (i, j))]
    out_shape = [jax.ShapeDtypeStruct((t, n_tiles * tn), BF16)]
    ride_steps = []
    for arr, first, n_slabs in rides:
        slab = arr.shape[0] // n_slabs
        assert slab * n_slabs == arr.shape[0] and slab % 16 == 0 and first + n_slabs <= n_tiles * nt
        spec = pl.BlockSpec(
            (slab, arr.shape[1]),
            lambda j, i, first=first, n=n_slabs: (jnp.clip(j * nt + i - first, 0, n - 1), 0))
        in_specs.append(spec)
        out_specs.append(spec)
        out_shape.append(jax.ShapeDtypeStruct(arr.shape, BF16))
        ride_steps.append((first, first + n_slabs))
    segments = []
    for (_, width, epilogue, row_chunks), j0 in zip(segs, j0s):
        j1 = j0 + width // tn
        active = tuple(r for r, (s0, s1) in enumerate(ride_steps) if s0 < j1 * nt and s1 > j0 * nt)
        segments.append((j0, j1, epilogue, row_chunks, active))
    outs = pl.pallas_call(
        functools.partial(_proj_kernel, tuple(segments), len(tables), len(rides)),
        grid=(n_tiles, nt),
        in_specs=in_specs,
        out_specs=out_specs,
        out_shape=out_shape,
        scratch_shapes=[pltpu.VMEM((k, tn), BF16)],
        compiler_params=_params(("arbitrary", "arbitrary"), "in_proj"),
        name=name,
    )(xb, w, *tables, *[arr for arr, _, _ in rides])
    return outs[0] if not rides else outs


def _retention_kernel(q_ref, k_ref, v_ref, gs_ref, ga_ref, gng_ref, gnb_ref,
                      o_ref, state_ref, dmask_ref, qdec_ref, kdec_ref, cdec_ref):
    c = RET_CHUNK

    @pl.when(pl.program_id(1) == 0)
    def _init():
        state_ref[...] = jnp.zeros_like(state_ref)
        ii = lax.broadcasted_iota(jnp.int32, (c, c), 0)
        jj = lax.broadcasted_iota(jnp.int32, (c, c), 1)
        diff = (ii - jj).astype(F32)
        idx = lax.broadcasted_iota(jnp.int32, (c, RET_DK), 0).astype(F32)
        for h in range(RET_HEADS):
            def log_gamma(shape):
                return jnp.log(1.0 - jnp.exp2(-5.0 - jnp.full(shape, float(h), F32)))

            dmask_ref[h] = jnp.where(
                diff >= 0.0, jnp.exp(log_gamma((c, c)) * jnp.maximum(diff, 0.0)), 0.0)
            lgl = log_gamma((c, RET_DK))
            qdec_ref[h] = jnp.exp(lgl * (idx + 1.0))
            kdec_ref[h] = jnp.exp(lgl * (c - 1.0 - idx))
            cdec_ref[h] = jnp.exp(log_gamma((8, LANES)) * float(c))

    for r0, h in ((r0, h) for r0 in range(0, q_ref.shape[0], c) for h in range(RET_HEADS)):
        rs = slice(r0, r0 + c)
        ksl = slice(h * RET_DK, (h + 1) * RET_DK)
        vsl = slice(h * RET_DV, (h + 1) * RET_DV)
        qb = q_ref[rs, ksl]
        kb = k_ref[rs, ksl]
        v = v_ref[rs, vsl]
        qd = (qb.astype(F32) * qdec_ref[h]).astype(BF16)
        kd = (kb.astype(F32) * kdec_ref[h]).astype(BF16)

        s = lax.dot_general(qb, kb, (((1,), (1,)), ((), ())), preferred_element_type=F32)
        s = s * dmask_ref[h]
        inner = jnp.dot(s.astype(BF16), v, preferred_element_type=F32)
        state = state_ref[h]
        cross = jnp.dot(qd, state.astype(BF16), preferred_element_type=F32)
        kv = lax.dot_general(kd, v, (((0,), (0,)), ((), ())), preferred_element_type=F32)
        state_ref[h] = state * cdec_ref[h, 0:1, 0:1] + kv

        y = inner + cross
        mu = jnp.mean(y, axis=-1, keepdims=True)
        d = y - mu
        var = jnp.mean(d * d, axis=-1, keepdims=True)
        yn = d * lax.rsqrt(var + GN_EPS) * gng_ref[:, vsl] + gnb_ref[:, vsl]
        gate = ga_ref[rs, vsl].astype(F32) * gs_ref[rs, vsl].astype(F32)
        o_ref[rs, vsl] = (gate * yn).astype(o_ref.dtype)


def _retention(qa, vg, gn_g, gn_b, bsz, seq):
    t = bsz * seq
    c = RET_CHUNK
    rows_per_step = RET_STEP_CHUNKS * c
    nc = seq // rows_per_step
    nh = RET_HEADS

    def rows(width, col):
        return pl.BlockSpec((rows_per_step, width), lambda b, n: (b * nc + n, col))

    gn = pl.BlockSpec((1, D_MODEL), lambda b, n: (0, 0))
    return pl.pallas_call(
        _retention_kernel,
        grid=(bsz, nc),
        in_specs=[rows(nh * RET_DK, 0), rows(nh * RET_DK, 1), rows(D_MODEL, 0), rows(D_MODEL, 1),
                  rows(D_MODEL, 2), gn, gn],
        out_specs=rows(D_MODEL, 0),
        out_shape=jax.ShapeDtypeStruct((t, D_MODEL), BF16),
        scratch_shapes=[pltpu.VMEM((nh, RET_DK, RET_DV), F32), pltpu.VMEM((nh, c, c), F32),
                        pltpu.VMEM((nh, c, RET_DK), F32), pltpu.VMEM((nh, c, RET_DK), F32),
                        pltpu.VMEM((nh, 8, LANES), F32)],
        compiler_params=_params(("parallel", "arbitrary"), "retention"),
        name="retention",
    )(qa, qa, vg, vg, vg, gn_g.reshape(1, -1), gn_b.reshape(1, -1))


SINK_LANES = 8


def _pair_blocks(prev, cur, g):
    c0 = (g // 2) * LANES
    kk = jnp.concatenate([prev[:, c0:c0 + LANES], cur[:, c0:c0 + LANES]], axis=0).astype(F32)
    lane = lax.broadcasted_iota(jnp.int32, kk.shape, 1)
    row = lax.broadcasted_iota(jnp.int32, kk.shape, 0)
    hd = ATT_HEAD_DIM
    own = jnp.where(((lane >= hd) if g % 2 else (lane < hd)) & (row != 0), kk, 0.0)
    other = pltpu.roll(own, hd, 1)
    lo, hi = (other, own) if g % 2 else (own, other)
    return lo.astype(BF16), hi.astype(BF16)


def _attention_consts(sink_ref, qaug_ref, kaug_ref, vones_ref):
    nk = 2 * ATT_BLOCK
    row_q = lax.broadcasted_iota(jnp.int32, qaug_ref.shape, 0)
    lane_q = lax.broadcasted_iota(jnp.int32, qaug_ref.shape, 1)
    pair_of_lane = (lane_q & (SINK_LANES - 1)) >> 1
    onehot = (lane_q < 2 * SINK_LANES) & (pair_of_lane == row_q // ATT_BLOCK)
    qaug_ref[...] = jnp.where(onehot, 1.0, 0.0).astype(BF16)
    row_k = lax.broadcasted_iota(jnp.int32, vones_ref.shape, 0)
    lane_k = lax.broadcasted_iota(jnp.int32, vones_ref.shape, 1)
    vones_ref[...] = jnp.where((row_k < nk) == (lane_k < ATT_HEAD_DIM), 1.0, 0.0).astype(BF16)
    for g in range(ATT_KV_HEADS):
        vals = jnp.zeros(vones_ref.shape, F32)
        for j in range(ATT_GROUP):
            sk = sink_ref[g * ATT_GROUP + j] * LOG2E
            vals = jnp.where((lane_k == j) & (row_k == (j % 2) * nk), sk, vals)
        hi = vals.astype(BF16).astype(F32)
        kaug_ref[g] = (hi + pltpu.roll(vals - hi, SINK_LANES, 1)).astype(BF16)


def _attention_kernel(last, sink_ref, q_ref, kp_ref, kc_ref, vp_ref, vc_ref, gb_ref, ret_ref, o_ref,
                      qaug_ref, kaug_ref, vones_ref, *p_refs):
    i = pl.program_id(1)
    bq = ATT_BLOCK
    nk = 2 * bq
    pairs = ATT_GROUP // 2

    @pl.when((pl.program_id(0) == 0) & (i == 0))
    def _init():
        _attention_consts(sink_ref, qaug_ref, kaug_ref, vones_ref)

    def scores(slot):
        kp, kc = kp_ref[...], kc_ref[...]
        qi = lax.broadcasted_iota(jnp.int32, (bq, nk), 0)
        kj = lax.broadcasted_iota(jnp.int32, (bq, nk), 1)
        dist = qi + bq - kj
        kmin = jnp.where(i > 0, 0, bq)
        valid = ((dist >= 0) & (dist < WINDOW) & (kj >= kmin)) | (kj == 0)
        bias = jnp.where(valid, 0.0, -jnp.inf)
        qaug = qaug_ref[...]
        for g in range(ATT_KV_HEADS):
            k_lo, k_hi = _pair_blocks(kp, kc, g)
            kfull = jnp.concatenate([jnp.concatenate([k_lo, k_hi], axis=0), kaug_ref[g]], axis=1)
            col0 = g * ATT_GROUP * ATT_HEAD_DIM
            qp = jnp.concatenate(
                [q_ref[:, col0 + p * LANES:col0 + (p + 1) * LANES] for p in range(pairs)], axis=0)
            qfull = jnp.concatenate([qp, qaug], axis=1)
            s = lax.dot_general(qfull, kfull, (((1,), (1,)), ((), ())),
                                preferred_element_type=F32)
            for p in range(pairs):
                for e in range(2):
                    sj = s[p * bq:(p + 1) * bq, e * nk:(e + 1) * nk] + bias
                    m = jnp.max(sj, axis=-1, keepdims=True)
                    p_refs[slot][g, p * bq:(p + 1) * bq, e * nk:(e + 1) * nk] = (
                        jnp.exp2(sj - m).astype(BF16))

    def values(slot):
        vp, vc = vp_ref[...], vc_ref[...]
        vones = vones_ref[...]
        for g in range(ATT_KV_HEADS):
            v_lo, v_hi = _pair_blocks(vp, vc, g)
            vfull = jnp.concatenate([jnp.concatenate([v_lo, v_hi], axis=0), vones], axis=1)
            o = jnp.dot(p_refs[slot][g], vfull, preferred_element_type=F32)
            col0 = g * ATT_GROUP * ATT_HEAD_DIM
            for p in range(pairs):
                num = o[p * bq:(p + 1) * bq, :LANES]
                den = o[p * bq:(p + 1) * bq, LANES:]
                c1 = col0 + p * LANES
                gate = gb_ref[:, c1:c1 + LANES].astype(F32)
                ret = ret_ref[:, c1:c1 + LANES].astype(F32)
                o_ref[:, c1:c1 + LANES] = (gate * num / den + ret).astype(o_ref.dtype)

    @pl.when(i == 0)
    def _first():
        scores(0)

    for par in (0, 1):
        @pl.when((i > 0) & (i < last) & (lax.rem(i, 2) == par))
        def _steady():
            values(1 - par)
            scores(par)

    @pl.when(i == last)
    def _last():
        values((last - 1) % 2)


def _attention(qa, akv, vg, ret, sinks, bsz, seq):
    t = bsz * seq
    bq = ATT_BLOCK
    nb = seq // bq
    rows = (ATT_GROUP // 2) * bq

    def spec(width, col, lag, back):
        def index(b, i):
            blk = jnp.clip(i - lag, 0, nb - 1)
            return (b * nb + jnp.maximum(blk - back, 0), col)
        return pl.BlockSpec((bq, width), index)

    return pl.pallas_call(
        functools.partial(_attention_kernel, nb),
        grid=(bsz, nb + 1),
        in_specs=[pl.BlockSpec(memory_space=pltpu.SMEM), spec(D_MODEL, 1, 0, 0),
                  spec(KV_W, 0, 0, 1), spec(KV_W, 0, 0, 0),
                  spec(KV_W, 1, 1, 1), spec(KV_W, 1, 1, 0), spec(D_MODEL, 3, 1, 0),
                  spec(D_MODEL, 0, 1, 0)],
        out_specs=spec(D_MODEL, 0, 1, 0),
        out_shape=jax.ShapeDtypeStruct((t, D_MODEL), BF16),
        scratch_shapes=[pltpu.VMEM((rows, LANES), BF16),
                        pltpu.VMEM((ATT_KV_HEADS, 2 * 2 * bq, LANES), BF16),
                        pltpu.VMEM((2 * 2 * bq, LANES), BF16),
                        pltpu.VMEM((ATT_KV_HEADS, rows, 2 * 2 * bq), BF16),
                        pltpu.VMEM((ATT_KV_HEADS, rows, 2 * 2 * bq), BF16)],
        compiler_params=_params(("arbitrary", "arbitrary"), "attention"),
        name="attention",
    )(sinks, qa, akv, akv, akv, akv, vg, ret)


def _out_proj_kernel(alpha, m_ref, x_ref, w_ref, g_ref, b_ref, h_ref, hb_ref):
    rows = h_ref.shape[0] // OUT_PROJ_ROW_CHUNKS
    for c in range(OUT_PROJ_ROW_CHUNKS):
        rs = slice(c * rows, (c + 1) * rows)
        y = jnp.dot(m_ref[rs, :], w_ref[...], preferred_element_type=F32)
        h = _layer_norm(alpha * x_ref[rs, :] + y, g_ref[...], b_ref[...])
        h_ref[rs, :] = h
        hb_ref[rs, :] = h.astype(BF16)


def _out_proj(merged, x2, wb, ln_g, ln_b, alpha):
    t, d = x2.shape
    tm = OUT_TM
    row = pl.BlockSpec((tm, d), lambda i: (i, 0))
    vec = pl.BlockSpec((1, d), lambda i: (0, 0))
    return pl.pallas_call(
        functools.partial(_out_proj_kernel, alpha),
        grid=(t // tm,),
        in_specs=[row, row, pl.BlockSpec((d, d), lambda i: (0, 0)), vec, vec],
        out_specs=[row, row],
        out_shape=[jax.ShapeDtypeStruct((t, d), F32), jax.ShapeDtypeStruct((t, d), BF16)],
        compiler_params=_params(("parallel",), "out_proj"),
        name="out_proj_ln",
    )(merged, x2, wb, ln_g.reshape(1, d), ln_b.reshape(1, d))


def _ffn_kernel(alpha, hb_ref, h_hbm, wg_ref, wu_ref, wd_ref, g_ref, b_ref, o_ref, hres_ref, sem):
    i = pl.program_id(0)
    f = pl.program_id(1)
    last = pl.num_programs(1) - 1
    tm = o_ref.shape[0]
    res_copy = pltpu.make_async_copy(h_hbm.at[pl.ds(i * tm, tm), :], hres_ref, sem)

    def tile():
        hb = hb_ref[...]
        gt = jnp.dot(hb, wg_ref[...], preferred_element_type=F32)
        ut = jnp.dot(hb, wu_ref[...], preferred_element_type=F32)
        a = (gt * _sigmoid(gt) * ut).astype(BF16)
        return jnp.dot(a, wd_ref[...], preferred_element_type=F32)

    @pl.when(f == 0)
    def _first():
        res_copy.start()
        o_ref[...] = tile()

    @pl.when((f > 0) & (f < last))
    def _middle():
        o_ref[...] += tile()

    @pl.when(f == last)
    def _last():
        res_copy.wait()
        z = alpha * hres_ref[...] + (o_ref[...] + tile())
        o_ref[...] = _layer_norm(z, g_ref[...], b_ref[...])


def _ffn(h1, h1b, wg, wu, wd, ln_g, ln_b, alpha):
    t, d = h1.shape
    dff = wg.shape[1]
    tm, tf = FFN_TM, FFN_TF
    row = pl.BlockSpec((tm, d), lambda i, f: (i, 0))
    vec = pl.BlockSpec((1, d), lambda i, f: (0, 0))
    return pl.pallas_call(
        functools.partial(_ffn_kernel, alpha),
        grid=(t // tm, dff // tf),
        in_specs=[row,
                  pl.BlockSpec(memory_space=pl.ANY),
                  pl.BlockSpec((d, tf), lambda i, f: (0, f)),
                  pl.BlockSpec((d, tf), lambda i, f: (0, f)),
                  pl.BlockSpec((tf, d), lambda i, f: (f, 0)),
                  vec, vec],
        out_specs=row,
        out_shape=jax.ShapeDtypeStruct((t, d), F32),
        scratch_shapes=[pltpu.VMEM((tm, d), F32), pltpu.SemaphoreType.DMA(())],
        compiler_params=_params(("parallel", "arbitrary"), "ffn"),
        name="ffn_ln",
    )(h1b, h1, wg, wu, wd, ln_g.reshape(1, d), ln_b.reshape(1, d))


def kernel(x, positions, w_in, ret_gn_g, ret_gn_b, att_sinks, w_out, ln1_g, ln1_b,
           w_gate, w_up, w_down, ln2_g, ln2_b):
    bsz, seq, d = x.shape
    depth = w_in.shape[0]
    alpha = (2.0 * depth) ** 0.25
    h = x.reshape(bsz * seq, d)
    for l in range(depth):
        w = w_in[l]
        ang_r, ca, s1, s2, hb, akv = _prep(positions, h, w)
        steps = 4 * (bsz * seq // PROJ_TM)
        qa, wdb, wob = _proj_multi(
            hb, w, [(OFF_RQ, OFF_RV - OFF_RQ, _ep_ret_rotary, 2 * ATT_Q_ROW_CHUNKS),
                    (OFF_AQ, D_MODEL, _ep_att_q, ATT_Q_ROW_CHUNKS)], PROJ_TN,
            (ang_r, ca, s1, s2), [(w_down[l], 0, steps), (w_out[l], 0, steps)], "in_proj_qk")
        vg, wgb, wub = _proj_multi(
            hb, w, [(OFF_RV, D_MODEL, _ep_plain, 1), (OFF_RG, D_MODEL, _ep_swish, 1),
                    (OFF_GA, 2 * D_MODEL, _ep_sigmoid, 1)], PROJ_TN,
            (), [(w_gate[l], 0, steps), (w_up[l], steps, steps)], "in_proj_vg")
        ret = _retention(qa, vg, ret_gn_g[l], ret_gn_b[l], bsz, seq)
        merged = _attention(qa, akv, vg, ret, att_sinks[l], bsz, seq)
        h, hb = _out_proj(merged, h, wob, ln1_g[l], ln1_b[l], alpha)
        h = _ffn(h, hb, wgb, wub, wdb, ln2_g[l], ln2_b[l], alpha)
    return h.reshape(bsz, seq, d)
```

```python
import functools
import math

import jax
import jax.numpy as jnp
from jax import lax
from jax.experimental import pallas as pl
from jax.experimental.pallas import tpu as pltpu

F32 = jnp.float32
BF16 = jnp.bfloat16

D_MODEL = 2048
RET_HEADS = 4
RET_DV = D_MODEL // RET_HEADS
RET_DK = RET_DV // 2
RET_CHUNK = 256
RET_THETA = 10000.0
ATT_HEAD_DIM = 64
ATT_HEADS = D_MODEL // ATT_HEAD_DIM
ATT_KV_HEADS = ATT_HEADS // 8
ATT_GROUP = ATT_HEADS // ATT_KV_HEADS
WINDOW = 128
ATT_BLOCK = WINDOW
ROPE_THETA = 500000.0
ROPE_DIM = ATT_HEAD_DIM // 4
LN_EPS = 1e-5
GN_EPS = 1e-5

LANES = 128
LOG2E = math.log2(math.e)

OFF_RQ = 0
OFF_RK = OFF_RQ + RET_HEADS * RET_DK
OFF_RV = OFF_RK + RET_HEADS * RET_DK
OFF_RG = OFF_RV + RET_HEADS * RET_DV
OFF_AQ = OFF_RG + RET_HEADS * RET_DV
OFF_AK = OFF_AQ + ATT_HEADS * ATT_HEAD_DIM
OFF_AV = OFF_AK + ATT_KV_HEADS * ATT_HEAD_DIM
OFF_GA = OFF_AV + ATT_KV_HEADS * ATT_HEAD_DIM
OFF_GB = OFF_GA + D_MODEL
KV_W = ATT_KV_HEADS * ATT_HEAD_DIM

PREP_TM = 1024
PROJ_TM, PROJ_TN = 1024, 1024
OUT_TM = 512
FFN_TM, FFN_TF = 1024, 512
RET_STEP_CHUNKS = 2
RET_RING = 3
OUT_PROJ_ROW_CHUNKS = 2
ATT_Q_ROW_CHUNKS = 4
VMEM_MIB = {"prep": 60, "in_proj": 56, "retention": 40, "attention": 32, "out_proj": 48,
            "ffn": 60}


def _params(sem, call):
    return pltpu.CompilerParams(dimension_semantics=sem,
                                vmem_limit_bytes=VMEM_MIB[call] * 1024 * 1024)


def _sigmoid(v):
    return 0.5 * jnp.tanh(0.5 * v) + 0.5


def _layer_norm(z, g, b):
    mu = jnp.mean(z, axis=-1, keepdims=True)
    d = z - mu
    var = jnp.mean(d * d, axis=-1, keepdims=True)
    return d * lax.rsqrt(var + LN_EPS) * g + b


def _prep_kernel(pos_ref, fr_ref, fa_ref, x_ref, w_ref,
                 cr_ref, sr_ref, ca_ref, s1_ref, s2_ref, xb_ref, akv_ref, wb_ref):
    @pl.when(pl.program_id(0) == 0)
    def _cast_weights():
        wb_ref[...] = w_ref[...].astype(BF16)

    xb = x_ref[...].astype(BF16)
    xb_ref[...] = xb
    pos_t = pos_ref[...].astype(F32).T
    pos = jnp.concatenate([jnp.broadcast_to(pos_t[:, c:c + 1], (LANES, LANES))
                           for c in range(pos_t.shape[1])], axis=0)
    ang_r = pos * fr_ref[...]
    cr_ref[...] = jnp.cos(ang_r)
    sr_ref[...] = jnp.sin(ang_r)
    ang_a = pos * fa_ref[...]
    sa = jnp.sin(ang_a)
    d = lax.broadcasted_iota(jnp.int32, ang_a.shape, 1) & (ATT_HEAD_DIM - 1)
    half = ROPE_DIM // 2
    ca_ref[...] = jnp.cos(ang_a)
    s1_ref[...] = jnp.where(d < half, -sa, 0.0)
    s2_ref[...] = jnp.where((d >= half) & (d < ROPE_DIM), sa, 0.0)
    acc = jnp.dot(xb, wb_ref[...], preferred_element_type=F32)
    akv_ref[...] = _ep_att_kv(acc, 0, ca_ref, s1_ref, s2_ref).astype(BF16)


def _prep(positions, x2, w):
    t, dm = x2.shape
    tm = PREP_TM
    kvw = OFF_GA - OFF_AK
    pos_l = positions.reshape(t // LANES, LANES)
    inv_r = 1.0 / (RET_THETA ** jnp.linspace(0.0, 1.0, RET_DK // 2, dtype=F32))
    inv_a = 1.0 / (ROPE_THETA ** (jnp.arange(0, ROPE_DIM, 2, dtype=F32) / ROPE_DIM))
    d = jnp.arange(LANES) % ATT_HEAD_DIM
    fa = jnp.where(d < ROPE_DIM, jnp.tile(inv_a, LANES // inv_a.size), 0.0).astype(F32)
    row = pl.BlockSpec((tm, LANES), lambda i: (i, 0))
    xrow = pl.BlockSpec((tm, dm), lambda i: (i, 0))
    vec = pl.BlockSpec((1, LANES), lambda i: (0, 0))
    tab = jax.ShapeDtypeStruct((t, LANES), F32)
    return pl.pallas_call(
        _prep_kernel,
        grid=(t // tm,),
        in_specs=[pl.BlockSpec((tm // LANES, LANES), lambda i: (i, 0)), vec, vec, xrow,
                  pl.BlockSpec((pl.Element(dm), pl.Element(kvw)), lambda i: (0, OFF_AK))],
        out_specs=[row] * 5 + [xrow, pl.BlockSpec((tm, kvw), lambda i: (i, 0))],
        out_shape=[tab] * 5 + [jax.ShapeDtypeStruct((t, dm), BF16),
                               jax.ShapeDtypeStruct((t, kvw), BF16)],
        scratch_shapes=[pltpu.VMEM((dm, kvw), BF16)],
        compiler_params=_params(("arbitrary",), "prep"),
        name="prep_tables_xcast_kv",
    )(pos_l, inv_r.reshape(1, LANES), fa.reshape(1, LANES), x2, w)


def _ep_plain(acc, j, *tabs):
    return acc


def _ep_swish(acc, j, *tabs):
    return acc * _sigmoid(acc)


def _ep_sigmoid(acc, j, *tabs):
    return _sigmoid(acc)


def _ep_ret_rotary(acc, j, cos_ref, sin_ref, *att_tabs):
    scale = jnp.where(j == 0, RET_DK ** -0.5, 1.0)
    cos = cos_ref[...] * scale
    sin = sin_ref[...] * scale
    half = RET_DK // 2
    outs = []
    for c0 in range(0, acc.shape[1], RET_DK):
        a1 = acc[:, c0:c0 + half]
        a2 = acc[:, c0 + half:c0 + RET_DK]
        outs += [a1 * cos - a2 * sin, a2 * cos + a1 * sin]
    return jnp.concatenate(outs, axis=-1)


def _rope_lanes(a, ca, s1, s2):
    outs = []
    for c0 in range(0, a.shape[1], LANES):
        ac = a[:, c0:c0 + LANES]
        up = pltpu.roll(ac, LANES - ROPE_DIM // 2, 1)
        dn = pltpu.roll(ac, ROPE_DIM // 2, 1)
        outs.append(ac * ca + up * s1 + dn * s2)
    return jnp.concatenate(outs, axis=-1)


def _ep_att_q(acc, j, cos_ref, sin_ref, ca_ref, s1_ref, s2_ref):
    c = (ATT_HEAD_DIM ** -0.5) * LOG2E
    return _rope_lanes(acc, ca_ref[...] * c, s1_ref[...] * c, s2_ref[...] * c)


def _ep_att_kv(acc, j, ca_ref, s1_ref, s2_ref):
    k = _rope_lanes(acc[:, :KV_W], ca_ref[...], s1_ref[...], s2_ref[...])
    return jnp.concatenate([k, acc[:, KV_W:]], axis=-1)


def _proj_kernel(segments, n_tab, n_ride, x_ref, w_ref, *rest):
    tab_refs = rest[:n_tab]
    ride_refs = rest[n_tab:n_tab + n_ride]
    o_ref = rest[n_tab + n_ride]
    ride_out_refs = rest[n_tab + n_ride + 1:n_tab + 2 * n_ride + 1]
    wb_ref = rest[-1]
    j = pl.program_id(0)

    @pl.when(pl.program_id(1) == 0)
    def _cast_weights():
        wb_ref[...] = w_ref[...].astype(BF16)

    for j0, j1, epilogue, row_chunks, rides in segments:
        @pl.when((j >= j0) & (j < j1))
        def _segment():
            for r in rides:
                ride_out_refs[r][...] = ride_refs[r][...].astype(BF16)
            rows = x_ref.shape[0] // row_chunks
            for c in range(row_chunks):
                rs = slice(c * rows, (c + 1) * rows)
                acc = jnp.dot(x_ref[rs, :], wb_ref[...], preferred_element_type=F32)
                tabs = [t.at[rs, :] for t in tab_refs]
                o_ref[rs, :] = epilogue(acc, j - j0, *tabs).astype(o_ref.dtype)


def _proj_multi(xb, w, segs, tn, tables=(), rides=(), name="in_proj"):
    t, k = xb.shape
    tm = PROJ_TM
    nt = t // tm
    j0s, bounds = [], 0
    for col0, width, _, row_chunks in segs:
        assert col0 % LANES == 0 and width % tn == 0 and tm % (16 * row_chunks) == 0
        j0s.append(bounds)
        bounds += width // tn
    n_tiles = bounds

    def w_col(j):
        col = segs[0][0] + j * tn
        for (col0, _, _, _), j0 in zip(segs[1:], j0s[1:]):
            col = jnp.where(j >= j0, col0 + (j - j0) * tn, col)
        return pl.multiple_of(col, LANES)

    tab = pl.BlockSpec((tm, LANES), lambda j, i: (i, 0))
    in_specs = [pl.BlockSpec((tm, k), lambda j, i: (i, 0)),
                pl.BlockSpec((pl.Element(k), pl.Element(tn)), lambda j, i: (0, w_col(j)))]
    in_specs += [tab] * len(tables)
    out_specs = [pl.BlockSpec((tm, tn), lambda j, i: (i, j))]
    out_shape = [jax.ShapeDtypeStruct((t, n_tiles * tn), BF16)]
    ride_steps = []
    for arr, first, n_slabs in rides:
        slab = arr.shape[0] // n_slabs
        assert slab * n_slabs == arr.shape[0] and slab % 16 == 0 and first + n_slabs <= n_tiles * nt
        spec = pl.BlockSpec(
            (slab, arr.shape[1]),
            lambda j, i, first=first, n=n_slabs: (jnp.clip(j * nt + i - first, 0, n - 1), 0))
        in_specs.append(spec)
        out_specs.append(spec)
        out_shape.append(jax.ShapeDtypeStruct(arr.shape, BF16))
        ride_steps.append((first, first + n_slabs))
    segments = []
    for (_, width, epilogue, row_chunks), j0 in zip(segs, j0s):
        j1 = j0 + width // tn
        active = tuple(r for r, (s0, s1) in enumerate(ride_steps) if s0 < j1 * nt and s1 > j0 * nt)
        segments.append((j0, j1, epilogue, row_chunks, active))
    outs = pl.pallas_call(
        functools.partial(_proj_kernel, tuple(segments), len(tables), len(rides)),
        grid=(n_tiles, nt),
        in_specs=in_specs,
        out_specs=out_specs,
        out_shape=out_shape,
        scratch_shapes=[pltpu.VMEM((k, tn), BF16)],
        compiler_params=_params(("arbitrary", "arbitrary"), "in_proj"),
        name=name,
    )(xb, w, *tables, *[arr for arr, _, _ in rides])
    return outs[0] if not rides else outs


def _retention_kernel(q_ref, k_ref, vg_hbm, gng_ref, gnb_ref,
                      o_ref, state_ref, dmask_ref, qdec_ref, kdec_ref, cdec_ref, ring_ref, sem):
    c = RET_CHUNK
    rows = q_ref.shape[0]
    step = pl.program_id(0) * pl.num_programs(1) + pl.program_id(1)
    n_steps = pl.num_programs(0) * pl.num_programs(1)

    def vg_copy(s):
        slot = s % RET_RING
        return pltpu.make_async_copy(vg_hbm.at[pl.ds(s * rows, rows), pl.ds(0, 3 * D_MODEL)],
                                     ring_ref.at[slot], sem.at[slot])

    @pl.when(step == 0)
    def _prime():
        for s in range(RET_RING - 1):
            vg_copy(s).start()

    @pl.when(step + RET_RING - 1 < n_steps)
    def _ahead():
        vg_copy(step + RET_RING - 1).start()

    vg_copy(step).wait()
    vg_ref = ring_ref.at[step % RET_RING]

    @pl.when(pl.program_id(1) == 0)
    def _init():
        state_ref[...] = jnp.zeros_like(state_ref)
        ii = lax.broadcasted_iota(jnp.int32, (c, c), 0)
        jj = lax.broadcasted_iota(jnp.int32, (c, c), 1)
        diff = (ii - jj).astype(F32)
        idx = lax.broadcasted_iota(jnp.int32, (c, RET_DK), 0).astype(F32)
        for h in range(RET_HEADS):
            def log_gamma(shape):
                return jnp.log(1.0 - jnp.exp2(-5.0 - jnp.full(shape, float(h), F32)))

            dmask_ref[h] = jnp.where(
                diff >= 0.0, jnp.exp(log_gamma((c, c)) * jnp.maximum(diff, 0.0)), 0.0)
            lgl = log_gamma((c, RET_DK))
            qdec_ref[h] = jnp.exp(lgl * (idx + 1.0))
            kdec_ref[h] = jnp.exp(lgl * (c - 1.0 - idx))
            cdec_ref[h] = jnp.exp(log_gamma((8, LANES)) * float(c))

    for r0, h in ((r0, h) for r0 in range(0, q_ref.shape[0], c) for h in range(RET_HEADS)):
        rs = slice(r0, r0 + c)
        ksl = slice(h * RET_DK, (h + 1) * RET_DK)
        vsl = slice(h * RET_DV, (h + 1) * RET_DV)
        qb = q_ref[rs, ksl]
        kb = k_ref[rs, ksl]
        v = vg_ref[rs, vsl]
        qd = (qb.astype(F32) * qdec_ref[h]).astype(BF16)
        kd = (kb.astype(F32) * kdec_ref[h]).astype(BF16)

        s = lax.dot_general(qb, kb, (((1,), (1,)), ((), ())), preferred_element_type=F32)
        s = s * dmask_ref[h]
        inner = jnp.dot(s.astype(BF16), v, preferred_element_type=F32)
        state = state_ref[h]
        cross = jnp.dot(qd, state.astype(BF16), preferred_element_type=F32)
        kv = lax.dot_general(kd, v, (((0,), (0,)), ((), ())), preferred_element_type=F32)
        state_ref[h] = state * cdec_ref[h, 0:1, 0:1] + kv

        y = inner + cross
        mu = jnp.mean(y, axis=-1, keepdims=True)
        d = y - mu
        var = jnp.mean(d * d, axis=-1, keepdims=True)
        yn = d * lax.rsqrt(var + GN_EPS) * gng_ref[:, vsl] + gnb_ref[:, vsl]
        gsl = slice(D_MODEL + h * RET_DV, D_MODEL + (h + 1) * RET_DV)
        asl = slice(2 * D_MODEL + h * RET_DV, 2 * D_MODEL + (h + 1) * RET_DV)
        gate = vg_ref[rs, asl].astype(F32) * vg_ref[rs, gsl].astype(F32)
        o_ref[rs, vsl] = (gate * yn).astype(o_ref.dtype)


def _retention(qa, vg, gn_g, gn_b, bsz, seq):
    t = bsz * seq
    c = RET_CHUNK
    rows_per_step = RET_STEP_CHUNKS * c
    nc = seq // rows_per_step
    nh = RET_HEADS

    def rows(width, col):
        return pl.BlockSpec((rows_per_step, width), lambda b, n: (b * nc + n, col))

    gn = pl.BlockSpec((1, D_MODEL), lambda b, n: (0, 0))
    return pl.pallas_call(
        _retention_kernel,
        grid=(bsz, nc),
        in_specs=[rows(nh * RET_DK, 0), rows(nh * RET_DK, 1), pl.BlockSpec(memory_space=pl.ANY),
                  gn, gn],
        out_specs=rows(D_MODEL, 0),
        out_shape=jax.ShapeDtypeStruct((t, D_MODEL), BF16),
        scratch_shapes=[pltpu.VMEM((nh, RET_DK, RET_DV), F32), pltpu.VMEM((nh, c, c), F32),
                        pltpu.VMEM((nh, c, RET_DK), F32), pltpu.VMEM((nh, c, RET_DK), F32),
                        pltpu.VMEM((nh, 8, LANES), F32),
                        pltpu.VMEM((RET_RING, rows_per_step, 3 * D_MODEL), BF16),
                        pltpu.SemaphoreType.DMA((RET_RING,))],
        compiler_params=_params(("arbitrary", "arbitrary"), "retention"),
        name="retention",
    )(qa, qa, vg, gn_g.reshape(1, -1), gn_b.reshape(1, -1))


SINK_LANES = 8


def _pair_blocks(prev, cur, g):
    c0 = (g // 2) * LANES
    kk = jnp.concatenate([prev[:, c0:c0 + LANES], cur[:, c0:c0 + LANES]], axis=0).astype(F32)
    lane = lax.broadcasted_iota(jnp.int32, kk.shape, 1)
    row = lax.broadcasted_iota(jnp.int32, kk.shape, 0)
    hd = ATT_HEAD_DIM
    own = jnp.where(((lane >= hd) if g % 2 else (lane < hd)) & (row != 0), kk, 0.0)
    other = pltpu.roll(own, hd, 1)
    lo, hi = (other, own) if g % 2 else (own, other)
    return lo.astype(BF16), hi.astype(BF16)


def _attention_consts(sink_ref, qaug_ref, kaug_ref, vones_ref):
    nk = 2 * ATT_BLOCK
    row_q = lax.broadcasted_iota(jnp.int32, qaug_ref.shape, 0)
    lane_q = lax.broadcasted_iota(jnp.int32, qaug_ref.shape, 1)
    pair_of_lane = (lane_q & (SINK_LANES - 1)) >> 1
    onehot = (lane_q < 2 * SINK_LANES) & (pair_of_lane == row_q // ATT_BLOCK)
    qaug_ref[...] = jnp.where(onehot, 1.0, 0.0).astype(BF16)
    row_k = lax.broadcasted_iota(jnp.int32, vones_ref.shape, 0)
    lane_k = lax.broadcasted_iota(jnp.int32, vones_ref.shape, 1)
    vones_ref[...] = jnp.where((row_k < nk) == (lane_k < ATT_HEAD_DIM), 1.0, 0.0).astype(BF16)
    for g in range(ATT_KV_HEADS):
        vals = jnp.zeros(vones_ref.shape, F32)
        for j in range(ATT_GROUP):
            sk = sink_ref[g * ATT_GROUP + j] * LOG2E
            vals = jnp.where((lane_k == j) & (row_k == (j % 2) * nk), sk, vals)
        hi = vals.astype(BF16).astype(F32)
        kaug_ref[g] = (hi + pltpu.roll(vals - hi, SINK_LANES, 1)).astype(BF16)


def _attention_kernel(last, sink_ref, q_ref, kp_ref, kc_ref, vp_ref, vc_ref, gb_ref, ret_ref, o_ref,
                      qaug_ref, kaug_ref, vones_ref, *p_refs):
    i = pl.program_id(1)
    bq = ATT_BLOCK
    nk = 2 * bq
    pairs = ATT_GROUP // 2

    @pl.when((pl.program_id(0) == 0) & (i == 0))
    def _init():
        _attention_consts(sink_ref, qaug_ref, kaug_ref, vones_ref)

    def scores(slot):
        kp, kc = kp_ref[...], kc_ref[...]
        qi = lax.broadcasted_iota(jnp.int32, (bq, nk), 0)
        kj = lax.broadcasted_iota(jnp.int32, (bq, nk), 1)
        dist = qi + bq - kj
        kmin = jnp.where(i > 0, 0, bq)
        valid = ((dist >= 0) & (dist < WINDOW) & (kj >= kmin)) | (kj == 0)
        bias = jnp.where(valid, 0.0, -jnp.inf)
        qaug = qaug_ref[...]
        for g in range(ATT_KV_HEADS):
            k_lo, k_hi = _pair_blocks(kp, kc, g)
            kfull = jnp.concatenate([jnp.concatenate([k_lo, k_hi], axis=0), kaug_ref[g]], axis=1)
            col0 = g * ATT_GROUP * ATT_HEAD_DIM
            qp = jnp.concatenate(
                [q_ref[:, col0 + p * LANES:col0 + (p + 1) * LANES] for p in range(pairs)], axis=0)
            qfull = jnp.concatenate([qp, qaug], axis=1)
            s = lax.dot_general(qfull, kfull, (((1,), (1,)), ((), ())),
                                preferred_element_type=F32)
            for p in range(pairs):
                for e in range(2):
                    sj = s[p * bq:(p + 1) * bq, e * nk:(e + 1) * nk] + bias
                    m = jnp.max(sj, axis=-1, keepdims=True)
                    p_refs[slot][g, p * bq:(p + 1) * bq, e * nk:(e + 1) * nk] = (
                        jnp.exp2(sj - m).astype(BF16))

    def values(slot):
        vp, vc = vp_ref[...], vc_ref[...]
        vones = vones_ref[...]
        for g in range(ATT_KV_HEADS):
            v_lo, v_hi = _pair_blocks(vp, vc, g)
            vfull = jnp.concatenate([jnp.concatenate([v_lo, v_hi], axis=0), vones], axis=1)
            o = jnp.dot(p_refs[slot][g], vfull, preferred_element_type=F32)
            col0 = g * ATT_GROUP * ATT_HEAD_DIM
            for p in range(pairs):
                num = o[p * bq:(p + 1) * bq, :LANES]
                den = o[p * bq:(p + 1) * bq, LANES:]
                c1 = col0 + p * LANES
                gate = gb_ref[:, c1:c1 + LANES].astype(F32)
                ret = ret_ref[:, c1:c1 + LANES].astype(F32)
                o_ref[:, c1:c1 + LANES] = (gate * num / den + ret).astype(o_ref.dtype)

    @pl.when(i == 0)
    def _first():
        scores(0)

    for par in (0, 1):
        @pl.when((i > 0) & (i < last) & (lax.rem(i, 2) == par))
        def _steady():
            values(1 - par)
            scores(par)

    @pl.when(i == last)
    def _last():
        values((last - 1) % 2)


def _attention(qa, akv, vg, ret, sinks, bsz, seq):
    t = bsz * seq
    bq = ATT_BLOCK
    nb = seq // bq
    rows = (ATT_GROUP // 2) * bq

    def spec(width, col, lag, back):
        def index(b, i):
            blk = jnp.clip(i - lag, 0, nb - 1)
            return (b * nb + jnp.maximum(blk - back, 0), col)
        return pl.BlockSpec((bq, width), index)

    return pl.pallas_call(
        functools.partial(_attention_kernel, nb),
        grid=(bsz, nb + 1),
        in_specs=[pl.BlockSpec(memory_space=pltpu.SMEM), spec(D_MODEL, 1, 0, 0),
                  spec(KV_W, 0, 0, 1), spec(KV_W, 0, 0, 0),
                  spec(KV_W, 1, 1, 1), spec(KV_W, 1, 1, 0), spec(D_MODEL, 3, 1, 0),
                  spec(D_MODEL, 0, 1, 0)],
        out_specs=spec(D_MODEL, 0, 1, 0),
        out_shape=jax.ShapeDtypeStruct((t, D_MODEL), BF16),
        scratch_shapes=[pltpu.VMEM((rows, LANES), BF16),
                        pltpu.VMEM((ATT_KV_HEADS, 2 * 2 * bq, LANES), BF16),
                        pltpu.VMEM((2 * 2 * bq, LANES), BF16),
                        pltpu.VMEM((ATT_KV_HEADS, rows, 2 * 2 * bq), BF16),
                        pltpu.VMEM((ATT_KV_HEADS, rows, 2 * 2 * bq), BF16)],
        compiler_params=_params(("arbitrary", "arbitrary"), "attention"),
        name="attention",
    )(sinks, qa, akv, akv, akv, akv, vg, ret)


def _out_proj_kernel(alpha, m_ref, x_ref, w_ref, g_ref, b_ref, h_ref, hb_ref):
    rows = h_ref.shape[0] // OUT_PROJ_ROW_CHUNKS
    for c in range(OUT_PROJ_ROW_CHUNKS):
        rs = slice(c * rows, (c + 1) * rows)
        y = jnp.dot(m_ref[rs, :], w_ref[...], preferred_element_type=F32)
        h = _layer_norm(alpha * x_ref[rs, :] + y, g_ref[...], b_ref[...])
        h_ref[rs, :] = h
        hb_ref[rs, :] = h.astype(BF16)


def _out_proj(merged, x2, wb, ln_g, ln_b, alpha):
    t, d = x2.shape
    tm = OUT_TM
    row = pl.BlockSpec((tm, d), lambda i: (i, 0))
    vec = pl.BlockSpec((1, d), lambda i: (0, 0))
    return pl.pallas_call(
        functools.partial(_out_proj_kernel, alpha),
        grid=(t // tm,),
        in_specs=[row, row, pl.BlockSpec((d, d), lambda i: (0, 0)), vec, vec],
        out_specs=[row, row],
        out_shape=[jax.ShapeDtypeStruct((t, d), F32), jax.ShapeDtypeStruct((t, d), BF16)],
        compiler_params=_params(("parallel",), "out_proj"),
        name="out_proj_ln",
    )(merged, x2, wb, ln_g.reshape(1, d), ln_b.reshape(1, d))


def _ffn_kernel(alpha, hb_ref, h_hbm, wg_ref, wu_ref, wd_ref, g_ref, b_ref, o_ref, hres_ref, sem):
    i = pl.program_id(0)
    f = pl.program_id(1)
    last = pl.num_programs(1) - 1
    tm = o_ref.shape[0]
    res_copy = pltpu.make_async_copy(h_hbm.at[pl.ds(i * tm, tm), :], hres_ref, sem)

    def tile():
        hb = hb_ref[...]
        gt = jnp.dot(hb, wg_ref[...], preferred_element_type=F32)
        ut = jnp.dot(hb, wu_ref[...], preferred_element_type=F32)
        a = (gt * _sigmoid(gt) * ut).astype(BF16)
        return jnp.dot(a, wd_ref[...], preferred_element_type=F32)

    @pl.when(f == 0)
    def _first():
        res_copy.start()
        o_ref[...] = tile()

    @pl.when((f > 0) & (f < last))
    def _middle():
        o_ref[...] += tile()

    @pl.when(f == last)
    def _last():
        res_copy.wait()
        z = alpha * hres_ref[...] + (o_ref[...] + tile())
        o_ref[...] = _layer_norm(z, g_ref[...], b_ref[...])


def _ffn(h1, h1b, wg, wu, wd, ln_g, ln_b, alpha):
    t, d = h1.shape
    dff = wg.shape[1]
    tm, tf = FFN_TM, FFN_TF
    row = pl.BlockSpec((tm, d), lambda i, f: (i, 0))
    vec = pl.BlockSpec((1, d), lambda i, f: (0, 0))
    return pl.pallas_call(
        functools.partial(_ffn_kernel, alpha),
        grid=(t // tm, dff // tf),
        in_specs=[row,
                  pl.BlockSpec(memory_space=pl.ANY),
                  pl.BlockSpec((d, tf), lambda i, f: (0, f)),
                  pl.BlockSpec((d, tf), lambda i, f: (0, f)),
                  pl.BlockSpec((tf, d), lambda i, f: (f, 0)),
                  vec, vec],
        out_specs=row,
        out_shape=jax.ShapeDtypeStruct((t, d), F32),
        scratch_shapes=[pltpu.VMEM((tm, d), F32), pltpu.SemaphoreType.DMA(())],
        compiler_params=_params(("parallel", "arbitrary"), "ffn"),
        name="ffn_ln",
    )(h1b, h1, wg, wu, wd, ln_g.reshape(1, d), ln_b.reshape(1, d))


def kernel(x, positions, w_in, ret_gn_g, ret_gn_b, att_sinks, w_out, ln1_g, ln1_b,
           w_gate, w_up, w_down, ln2_g, ln2_b):
    bsz, seq, d = x.shape
    depth = w_in.shape[0]
    alpha = (2.0 * depth) ** 0.25
    h = x.reshape(bsz * seq, d)
    for l in range(depth):
        w = w_in[l]
        cos_r, sin_r, ca, s1, s2, hb, akv = _prep(positions, h, w)
        steps = 4 * (bsz * seq // PROJ_TM)
        qa, wdb, wob = _proj_multi(
            hb, w, [(OFF_RQ, OFF_RV - OFF_RQ, _ep_ret_rotary, 1),
                    (OFF_AQ, D_MODEL, _ep_att_q, ATT_Q_ROW_CHUNKS)], PROJ_TN,
            (cos_r, sin_r, ca, s1, s2), [(w_down[l], 0, steps), (w_out[l], 0, steps)], "in_proj_qk")
        vg, wgb, wub = _proj_multi(
            hb, w, [(OFF_RV, D_MODEL, _ep_plain, 1), (OFF_RG, D_MODEL, _ep_swish, 1),
                    (OFF_GA, 2 * D_MODEL, _ep_sigmoid, 1)], PROJ_TN,
            (), [(w_gate[l], 0, steps), (w_up[l], steps, steps)], "in_proj_vg")
        ret = _retention(qa, vg, ret_gn_g[l], ret_gn_b[l], bsz, seq)
        merged = _attention(qa, akv, vg, ret, att_sinks[l], bsz, seq)
        h, hb = _out_proj(merged, h, wob, ln1_g[l], ln1_b[l], alpha)
        h = _ffn(h, hb, wgb, wub, wdb, ln2_g[l], ln2_b[l], alpha)
    return h.reshape(bsz, seq, d)
```
